```python
import jax, jax.numpy as jnp
from jax import lax
import numpy as np

D_MODEL = 4096
BATCH = 1
SEQ = 8192
DEPTH = 1
DEC_BATCH = 128
DEC_SEQ = 4
PAST_LEN = 2048
PAGE_SIZE = 128

HEAD_DIM = 128
N_HEADS = D_MODEL // HEAD_DIM
MOBA_HEADS = N_HEADS // 2
DSA_HEADS = N_HEADS - MOBA_HEADS
MOBA_KV_HEADS = MOBA_HEADS // 4
DSA_KV_HEADS = DSA_HEADS // 4
MOBA_GROUP = MOBA_HEADS // MOBA_KV_HEADS
DSA_GROUP = DSA_HEADS // DSA_KV_HEADS
MIX_WIDTH = (MOBA_HEADS + DSA_HEADS) * HEAD_DIM
MOBA_BLOCK = 256
MOBA_TOPK = 3
IDX_HEADS = 32
IDX_DIM = 128
DSA_TOPK_MAX = 256
ROPE_THETA = 10000.0
ATTN_SCALE = HEAD_DIM ** -0.5
IDX_W_SCALE = (IDX_HEADS ** -0.5) * (IDX_DIM ** -0.5)
PEER_HEADS = 8
PEER_N_KEYS = 128
PEER_N_EXPERTS = PEER_N_KEYS * PEER_N_KEYS
PEER_DK = 256
PEER_TOPK = 16
NORM_EPS = 1e-6
Q_CHUNK = 64
PEER_CHUNK = 64
PROJ_SIZES = (MOBA_HEADS * HEAD_DIM, MOBA_KV_HEADS * HEAD_DIM, MOBA_KV_HEADS * HEAD_DIM,
              DSA_HEADS * HEAD_DIM, DSA_KV_HEADS * HEAD_DIM, DSA_KV_HEADS * HEAD_DIM,
              IDX_HEADS * IDX_DIM, IDX_DIM, IDX_HEADS)
PROJ_WIDTH = sum(PROJ_SIZES)
PROJ_SPLIT = tuple(int(s) for s in np.cumsum(PROJ_SIZES)[:-1])

kernel_name = 'hymba_moba_dsa_peer_step'


def rms_norm(x, g):
    xf = x.astype(jnp.float32)
    y = xf * lax.rsqrt(jnp.mean(xf * xf, axis=-1, keepdims=True) + NORM_EPS)
    return (y * g.astype(jnp.float32)).astype(x.dtype)


def rope(x, pos):
    d = x.shape[-1]
    half = d // 2
    inv = 1.0 / (ROPE_THETA ** (jnp.arange(half, dtype=jnp.float32) * (2.0 / d)))
    ang = pos.astype(jnp.float32)[:, None] * inv[None, :]
    cos = jnp.cos(ang)[:, None, :]
    sin = jnp.sin(ang)[:, None, :]
    x1 = x[..., :half].astype(jnp.float32)
    x2 = x[..., half:].astype(jnp.float32)
    return jnp.concatenate([x1 * cos - x2 * sin, x2 * cos + x1 * sin], axis=-1).astype(x.dtype)


def project(u, w_in, pos):
    B, T, _ = u.shape
    p = jnp.split(u @ w_in, PROJ_SPLIT, axis=-1)
    q_m = rope(p[0].reshape(B, T, MOBA_HEADS, HEAD_DIM), pos)
    k_m = rope(p[1].reshape(B, T, MOBA_KV_HEADS, HEAD_DIM), pos)
    v_m = p[2].reshape(B, T, MOBA_KV_HEADS, HEAD_DIM)
    q_d = rope(p[3].reshape(B, T, DSA_HEADS, HEAD_DIM), pos)
    k_d = rope(p[4].reshape(B, T, DSA_KV_HEADS, HEAD_DIM), pos)
    v_d = p[5].reshape(B, T, DSA_KV_HEADS, HEAD_DIM)
    q_i = rope(p[6].reshape(B, T, IDX_HEADS, IDX_DIM), pos)
    k_i = rope(p[7].reshape(B, T, 1, IDX_DIM), pos)[:, :, 0]
    w_i = p[8] * IDX_W_SCALE
    return (q_m, k_m, v_m, q_d, k_d, v_d, q_i, k_i, w_i)


def mix_sequence(q_m, k_m, v_m, q_d, k_d, v_d, q_i, k_i, w_i, q_pos):
    L = k_m.shape[0]
    n_blk = -(-L // MOBA_BLOCK)
    pad = n_blk * MOBA_BLOCK - L
    kb = jnp.pad(k_m, ((0, pad), (0, 0), (0, 0))).reshape(n_blk, MOBA_BLOCK, MOBA_KV_HEADS, HEAD_DIM)
    vb = jnp.pad(v_m, ((0, pad), (0, 0), (0, 0))).reshape(n_blk, MOBA_BLOCK, MOBA_KV_HEADS, HEAD_DIM)
    k_mean = jnp.mean(kb.astype(jnp.float32), axis=1).astype(k_m.dtype)
    kb_h = jnp.transpose(kb, (2, 0, 1, 3))
    vb_h = jnp.transpose(vb, (2, 0, 1, 3))
    kv_of_head = jnp.arange(MOBA_HEADS) // MOBA_GROUP
    n_sel = min(MOBA_TOPK, n_blk)
    k_keep = max(1, min(DSA_TOPK_MAX, L // 4))
    key_pos = jnp.arange(L)
    blk_ids = jnp.arange(n_blk)
    in_blk = jnp.arange(MOBA_BLOCK)

    def chunk(qm, qd, qi, wi, pos):
        C = qm.shape[0]
        own = pos // MOBA_BLOCK
        qm_g = qm.reshape(C, MOBA_KV_HEADS, MOBA_GROUP, HEAD_DIM)
        gate = jnp.einsum('cngd,bnd->cngb', qm_g, k_mean).reshape(C, MOBA_HEADS, n_blk).astype(jnp.float32)
        gate = jnp.where(blk_ids[None, None, :] < own[:, None, None], gate, -jnp.inf)
        _, sel = lax.top_k(gate, n_sel)
        sel_ok = sel < own[:, None, None]
        k_sel = kb_h[kv_of_head[None, :, None], sel]
        v_sel = vb_h[kv_of_head[None, :, None], sel]
        s_sel = jnp.einsum('chd,chjkd->chjk', qm, k_sel).astype(jnp.float32) * ATTN_SCALE
        s_sel = jnp.where(sel_ok[..., None], s_sel, -jnp.inf).reshape(C, MOBA_HEADS, n_sel * MOBA_BLOCK)
        k_own = kb[own]
        v_own = vb[own]
        s_own = jnp.einsum('cngd,cknd->cngk', qm_g, k_own).reshape(C, MOBA_HEADS, MOBA_BLOCK).astype(jnp.float32) * ATTN_SCALE
        own_pos = own[:, None] * MOBA_BLOCK + in_blk[None, :]
        s_own = jnp.where((own_pos <= pos[:, None])[:, None, :], s_own, -jnp.inf)
        p_m = jax.nn.softmax(jnp.concatenate([s_sel, s_own], axis=-1), axis=-1).astype(qm.dtype)
        p_sel = p_m[..., :n_sel * MOBA_BLOCK].reshape(C, MOBA_HEADS, n_sel, MOBA_BLOCK)
        p_own = p_m[..., n_sel * MOBA_BLOCK:].reshape(C, MOBA_KV_HEADS, MOBA_GROUP, MOBA_BLOCK)
        o_m = (jnp.einsum('chjk,chjkd->chd', p_sel, v_sel)
               + jnp.einsum('cngk,cknd->cngd', p_own, v_own).reshape(C, MOBA_HEADS, HEAD_DIM))
        rel = jax.nn.relu(jnp.einsum('chd,sd->chs', qi, k_i))
        idx_s = jnp.einsum('chs,ch->cs', rel, wi).astype(jnp.float32)
        idx_s = jnp.where(key_pos[None, :] <= pos[:, None], idx_s, -jnp.inf)
        _, kid = lax.top_k(idx_s, k_keep)
        kid_ok = kid <= pos[:, None]
        k_g = k_d[kid]
        v_g = v_d[kid]
        qd_g = qd.reshape(C, DSA_KV_HEADS, DSA_GROUP, HEAD_DIM)
        s_d = jnp.einsum('cngd,cknd->cngk', qd_g, k_g).astype(jnp.float32) * ATTN_SCALE
        s_d = jnp.where(kid_ok[:, None, None, :], s_d, -jnp.inf)
        p_d = jax.nn.softmax(s_d, axis=-1).astype(qd.dtype)
        o_d = jnp.einsum('cngk,cknd->cngd', p_d, v_g).reshape(C, DSA_HEADS * HEAD_DIM)
        return jnp.concatenate([o_m.reshape(C, MOBA_HEADS * HEAD_DIM), o_d], axis=-1)

    Tq = q_m.shape[0]
    if Tq > Q_CHUNK and Tq % Q_CHUNK == 0:
        n = Tq // Q_CHUNK
        r = lambda a: a.reshape((n, Q_CHUNK) + a.shape[1:])
        out = lax.map(lambda a: chunk(*a), (r(q_m), r(q_d), r(q_i), r(w_i), r(q_pos)))
        return out.reshape(Tq, MIX_WIDTH)
    return chunk(q_m, q_d, q_i, w_i, q_pos)


def peer_ffn(x, w_q, subkeys, u_tab, v_tab):
    B, T, D = x.shape
    N = B * T
    n_chunk = -(-N // PEER_CHUNK)
    pad = n_chunk * PEER_CHUNK - N
    xc = jnp.pad(x.reshape(N, D), ((0, pad), (0, 0))).reshape(n_chunk, PEER_CHUNK, D)

    def chunk(xb):
        n = xb.shape[0]
        q = (xb @ w_q).reshape(n, PEER_HEADS, 2, PEER_DK // 2)
        s = jnp.einsum('nhpd,hpkd->nhpk', q, subkeys).astype(jnp.float32)
        s_half, i_half = lax.top_k(s, PEER_TOPK)
        cand = (s_half[:, :, 0, :, None] + s_half[:, :, 1, None, :]).reshape(n, PEER_HEADS, PEER_TOPK * PEER_TOPK)
        top_s, top_c = lax.top_k(cand, PEER_TOPK)
        e = (jnp.take_along_axis(i_half[:, :, 0], top_c // PEER_TOPK, axis=-1) * PEER_N_KEYS
             + jnp.take_along_axis(i_half[:, :, 1], top_c % PEER_TOPK, axis=-1))
        g = jax.nn.softmax(top_s, axis=-1).astype(xb.dtype)
        a = jax.nn.gelu(jnp.einsum('nd,nhkd->nhk', xb, u_tab[e]), approximate=False)
        return jnp.einsum('nhk,nhkd->nd', g * a, v_tab[e])

    out = lax.map(chunk, xc).reshape(n_chunk * PEER_CHUNK, D)[:N]
    return out.reshape(B, T, D)


def trunk_layer(x, pos, mix_fn, norm_mix, w_in, w_out, norm_ffn, peer_w_q, peer_subkeys, peer_u, peer_v):
    proj = project(rms_norm(x, norm_mix), w_in, pos)
    h = x + mix_fn(proj) @ w_out
    h = h + peer_ffn(rms_norm(h, norm_ffn), peer_w_q, peer_subkeys, peer_u, peer_v)
    return h, (proj[1], proj[2], proj[4], proj[5], proj[7])


def with_past(cache, pt, new):
    rows = cache[pt]
    rows = rows.reshape((rows.shape[0] * rows.shape[1],) + rows.shape[2:])
    return jnp.concatenate([rows.astype(new.dtype), new], axis=0)


def setup_inputs(seed: int = 0) -> dict:
    key = jax.random.key(seed)
    ks = jax.random.split(key, 20)
    n_pages = PAST_LEN // PAGE_SIZE
    used = DEC_BATCH * n_pages
    n_pool = used + max(1, used // 4)
    f32 = jnp.float32
    nrm = lambda k, shape, s: jax.random.normal(k, shape, f32) * s
    kv_shape = (DEPTH, n_pool, PAGE_SIZE, MOBA_KV_HEADS, HEAD_DIM)
    dkv_shape = (DEPTH, n_pool, PAGE_SIZE, DSA_KV_HEADS, HEAD_DIM)
    page_table = jax.random.permutation(ks[7], n_pool)[:used].reshape(DEC_BATCH, n_pages).astype(jnp.int32)
    return {
        'x_prompt': nrm(ks[0], (BATCH, SEQ, D_MODEL), 1.0),
        'x_sample': nrm(ks[1], (DEC_BATCH, DEC_SEQ, D_MODEL), 1.0),
        'cache_moba_k': nrm(ks[2], kv_shape, 1.0),
        'cache_moba_v': nrm(ks[3], kv_shape, 1.0),
        'cache_dsa_k': nrm(ks[4], dkv_shape, 1.0),
        'cache_dsa_v': nrm(ks[5], dkv_shape, 1.0),
        'cache_idx_k': nrm(ks[6], (DEPTH, n_pool, PAGE_SIZE, IDX_DIM), 1.0),
        'page_table': page_table,
        'norm_mix': 1.0 + nrm(ks[8], (DEPTH, D_MODEL), 0.02),
        'w_in': nrm(ks[9], (DEPTH, D_MODEL, PROJ_WIDTH), D_MODEL ** -0.5),
        'w_out': nrm(ks[10], (DEPTH, MIX_WIDTH, D_MODEL), MIX_WIDTH ** -0.5),
        'norm_ffn': 1.0 + nrm(ks[11], (DEPTH, D_MODEL), 0.02),
        'peer_w_q': nrm(ks[12], (DEPTH, D_MODEL, PEER_HEADS * PEER_DK), D_MODEL ** -0.5),
        'peer_subkeys': nrm(ks[13], (DEPTH, PEER_HEADS, 2, PEER_N_KEYS, PEER_DK // 2), (PEER_DK // 2) ** -0.5),
        'peer_u': nrm(ks[14], (DEPTH, PEER_N_EXPERTS, D_MODEL), D_MODEL ** -0.5),
        'peer_v': nrm(ks[15], (DEPTH, PEER_N_EXPERTS, D_MODEL), PEER_HEADS ** -0.5),
        'norm_final': 1.0 + nrm(ks[16], (D_MODEL,), 0.02),
    }


def reference(x_prompt, x_sample, cache_moba_k, cache_moba_v, cache_dsa_k, cache_dsa_v, cache_idx_k,
              page_table, norm_mix, w_in, w_out, norm_ffn, peer_w_q, peer_subkeys, peer_u, peer_v, norm_final):
    pos_p = jnp.arange(x_prompt.shape[1], dtype=jnp.int32)
    pos_s = PAST_LEN + jnp.arange(x_sample.shape[1], dtype=jnp.int32)
    h_p, h_s = x_prompt, x_sample
    rows_p, rows_s = [], []
    for l in range(DEPTH):
        weights = (norm_mix[l], w_in[l], w_out[l], norm_ffn[l], peer_w_q[l], peer_subkeys[l], peer_u[l], peer_v[l])
        caches = (cache_moba_k[l], cache_moba_v[l], cache_dsa_k[l], cache_dsa_v[l], cache_idx_k[l])

        def prompt_mix(proj):
            return lax.map(lambda a: mix_sequence(*a, pos_p), proj)

        def sample_mix(proj, caches=caches):
            c_mk, c_mv, c_dk, c_dv, c_ik = caches

            def one(a):
                q_m, k_m, v_m, q_d, k_d, v_d, q_i, k_i, w_i, pt = a
                return mix_sequence(q_m, with_past(c_mk, pt, k_m), with_past(c_mv, pt, v_m),
                                    q_d, with_past(c_dk, pt, k_d), with_past(c_dv, pt, v_d),
                                    q_i, with_past(c_ik, pt, k_i), w_i, pos_s)
            return lax.map(one, proj + (page_table,))

        h_p, new_p = trunk_layer(h_p, pos_p, prompt_mix, *weights)
        h_s, new_s = trunk_layer(h_s, pos_s, sample_mix, *weights)
        rows_p.append(new_p)
        rows_s.append(new_s)
    y_prompt = rms_norm(h_p, norm_final)
    y_sample = rms_norm(h_s, norm_final)
    st = lambda rows, i: jnp.stack([r[i] for r in rows], axis=0)
    return (y_prompt, y_sample,
            st(rows_p, 0), st(rows_p, 1), st(rows_p, 2), st(rows_p, 3), st(rows_p, 4),
            st(rows_s, 0), st(rows_s, 1), st(rows_s, 2), st(rows_s, 3), st(rows_s, 4))
```

```python
import functools

import numpy as np
import jax
import jax.numpy as jnp
from jax import lax
from jax.experimental import pallas as pl
from jax.experimental.pallas import tpu as pltpu

F32 = jnp.float32
BF16 = jnp.bfloat16

D_MODEL = 4096
HEAD_DIM = 128
MOBA_HEADS = 16
DSA_HEADS = 16
KV_HEADS = 4
GROUP = 4
MOBA_BLOCK = 256
MOBA_TOPK = 3
IDX_HEADS = 32
IDX_DIM = 128
DSA_TOPK_MAX = 256
ROPE_THETA = 10000.0
ATTN_SCALE = HEAD_DIM ** -0.5
IDX_W_SCALE = (IDX_HEADS ** -0.5) * (IDX_DIM ** -0.5)
PEER_HEADS = 8
PEER_N_KEYS = 128
PEER_DK = 256
PEER_TOPK = 16
NORM_EPS = 1e-6
PAST_LEN = 2048
PAGE_SIZE = 128

LANES = 128
NEG = -1e30
INT_MIN = -2 ** 31
VMEM_LIMIT = 56 * 1024 * 1024

C_QI = 0
C_QM = C_QI + IDX_HEADS * IDX_DIM
C_QD = C_QM + MOBA_HEADS * HEAD_DIM
C_KM = C_QD + DSA_HEADS * HEAD_DIM
C_VM = C_KM + KV_HEADS * HEAD_DIM
C_KD = C_VM + KV_HEADS * HEAD_DIM
C_VD = C_KD + KV_HEADS * HEAD_DIM
C_KI = C_VD + KV_HEADS * HEAD_DIM
C_WI = C_KI + IDX_DIM
PROJ_TN = 512
PROJ_PAD = 10752
KVW = KV_HEADS * HEAD_DIM


def _cparams(sem):
    return pltpu.CompilerParams(dimension_semantics=sem, vmem_limit_bytes=VMEM_LIMIT)


def _dot_nt(a, b):
    return lax.dot_general(a, b, (((1,), (1,)), ((), ())), preferred_element_type=F32)


def _rmsnorm_kernel(x_ref, g_ref, o_ref):
    x = x_ref[...]
    y = x * lax.rsqrt(jnp.mean(x * x, axis=-1, keepdims=True) + NORM_EPS)
    o_ref[...] = (y * g_ref[...]).astype(o_ref.dtype)


def _rmsnorm(x, g, out_dtype, tm):
    n, d = x.shape
    return pl.pallas_call(
        _rmsnorm_kernel,
        grid=(n // tm,),
        in_specs=[pl.BlockSpec((tm, d), lambda i: (i, 0)), pl.BlockSpec((1, d), lambda i: (0, 0))],
        out_specs=pl.BlockSpec((tm, d), lambda i: (i, 0)),
        out_shape=jax.ShapeDtypeStruct((n, d), out_dtype),
        compiler_params=_cparams(("parallel",)),
        name="rmsnorm",
    )(x, g.reshape(1, d))


def _rope_group(a, cos, sin):
    return a * cos + pltpu.roll(a, HEAD_DIM // 2, axis=1) * sin


def _proj_kernel(u_ref, w_ref, cos_ref, sin_ref, o_ref):
    j = pl.program_id(1)
    acc = jnp.dot(u_ref[...], w_ref[...], preferred_element_type=F32)
    plain = jnp.logical_or(j == C_VM // PROJ_TN, j == C_VD // PROJ_TN)
    last = j == C_KI // PROJ_TN
    groups = PROJ_TN // HEAD_DIM

    @pl.when(plain)
    def _():
        o_ref[...] = acc

    @pl.when(jnp.logical_not(jnp.logical_or(plain, last)))
    def _():
        cos = cos_ref[...]
        sin = sin_ref[...]
        for c in range(groups):
            o_ref[:, c * HEAD_DIM:(c + 1) * HEAD_DIM] = _rope_group(acc[:, c * HEAD_DIM:(c + 1) * HEAD_DIM], cos, sin)

    @pl.when(last)
    def _():
        o_ref[:, 0:HEAD_DIM] = _rope_group(acc[:, 0:HEAD_DIM], cos_ref[...], sin_ref[...])
        o_ref[:, HEAD_DIM:2 * HEAD_DIM] = acc[:, HEAD_DIM:2 * HEAD_DIM] * IDX_W_SCALE
        o_ref[:, 2 * HEAD_DIM:] = acc[:, 2 * HEAD_DIM:]


def _project(u, w, cos, sin, tm):
    n, d = u.shape
    return pl.pallas_call(
        _proj_kernel,
        grid=(n // tm, PROJ_PAD // PROJ_TN),
        in_specs=[
            pl.BlockSpec((tm, d), lambda i, j: (i, 0)),
            pl.BlockSpec((d, PROJ_TN), lambda i, j: (0, j)),
            pl.BlockSpec((tm, HEAD_DIM), lambda i, j: (i, 0)),
            pl.BlockSpec((tm, HEAD_DIM), lambda i, j: (i, 0)),
        ],
        out_specs=pl.BlockSpec((tm, PROJ_TN), lambda i, j: (i, j)),
        out_shape=jax.ShapeDtypeStruct((n, PROJ_PAD), F32),
        compiler_params=_cparams(("parallel", "arbitrary")),
        name="proj_rope",
    )(u, w, cos, sin)


def _rope_tables(pos):
    half = HEAD_DIM // 2
    inv = 1.0 / (ROPE_THETA ** (jnp.arange(half, dtype=F32) * (2.0 / HEAD_DIM)))
    ang = pos.astype(F32)[:, None] * inv[None, :]
    c, s = jnp.cos(ang), jnp.sin(ang)
    return jnp.concatenate([c, c], axis=-1), jnp.concatenate([-s, s], axis=-1)


def _kmean_kernel(k_ref, o_ref):
    nb = o_ref.shape[0]
    k = k_ref[...].reshape(nb, MOBA_BLOCK, KVW)
    o_ref[...] = jnp.sum(k, axis=1) * (1.0 / MOBA_BLOCK)


def _kmean(proj, t):
    nb = t // MOBA_BLOCK
    return pl.pallas_call(
        _kmean_kernel,
        grid=(1,),
        in_specs=[pl.BlockSpec((t, KVW), lambda i: (0, C_KM // KVW))],
        out_specs=pl.BlockSpec((nb, KVW), lambda i: (0, 0)),
        out_shape=jax.ShapeDtypeStruct((nb, KVW), F32),
        compiler_params=_cparams(("arbitrary",)),
        name="moba_kmean",
    )(proj)


def _top_blocks(gate, lane, n_sel):
    sel = jnp.zeros(gate.shape, F32)
    for _ in range(n_sel):
        mx = jnp.max(gate, axis=-1, keepdims=True)
        idx = jnp.min(jnp.where(gate == mx, lane, LANES), axis=-1, keepdims=True)
        hit = lane == idx
        sel = jnp.where(jnp.logical_and(hit, mx > -jnp.inf), 1.0, sel)
        gate = jnp.where(hit, -jnp.inf, gate)
    return sel


def _softmax_step(s, v, m_ref, l_ref, acc_ref, n):
    m_prev = m_ref[n]
    m_new = jnp.maximum(m_prev, jnp.max(s, axis=-1, keepdims=True))
    alpha = jnp.exp(m_prev - m_new)
    p = jnp.exp(s - m_new)
    l_ref[n] = alpha * l_ref[n] + jnp.sum(p, axis=-1, keepdims=True)
    acc_ref[n] = alpha * acc_ref[n] + jnp.dot(p.astype(BF16), v, preferred_element_type=F32)
    m_ref[n] = m_new


def _sortable_key(x):
    bits = pltpu.bitcast(x + 0.0, jnp.int32)
    return bits ^ ((bits >> 31) & 0x7FFFFFFF)


def _kth_largest_key(count_ge, shape, k):
    t = jnp.where(count_ge(jnp.zeros(shape, jnp.int32)) >= k, 0, INT_MIN).astype(jnp.int32)

    def body(it, t):
        cand = t + (jnp.int32(1) << (30 - it))
        return jnp.where(count_ge(cand) >= k, cand, t)

    return lax.fori_loop(0, 31, body, t)


PTQ = 256


def _prompt_attn_kernel(moba, q_ref, k_ref, v_ref, aux_ref, o_ref, qs_ref, m_ref, l_ref, acc_ref, sel_ref):
    i = pl.program_id(0)
    j = pl.program_id(1)
    rows = GROUP * PTQ

    @pl.when(j == 0)
    def _():
        for n in range(KV_HEADS):
            for g in range(GROUP):
                h = n * GROUP + g
                qs_ref[n, g * PTQ:(g + 1) * PTQ, :] = q_ref[:, h * HEAD_DIM:(h + 1) * HEAD_DIM].astype(BF16)
        m_ref[...] = jnp.full(m_ref.shape, NEG, F32)
        l_ref[...] = jnp.zeros(l_ref.shape, F32)
        acc_ref[...] = jnp.zeros(acc_ref.shape, F32)
        if moba:
            lane = lax.broadcasted_iota(jnp.int32, (rows, LANES), 1)
            for n in range(KV_HEADS):
                km = aux_ref[:, n * HEAD_DIM:(n + 1) * HEAD_DIM].astype(BF16)
                gate = jnp.where(lane < i, _dot_nt(qs_ref[n], km), -jnp.inf)
                sel_ref[n] = _top_blocks(gate, lane, MOBA_TOPK)

    @pl.when(j <= i)
    def _():
        k = k_ref[...].astype(BF16)
        v = v_ref[...].astype(BF16)
        if moba:
            lane = lax.broadcasted_iota(jnp.int32, (rows, LANES), 1)
            col = lax.broadcasted_iota(jnp.int32, (rows, PTQ), 1)
            t_in = lax.broadcasted_iota(jnp.int32, (rows, 1), 0) & (PTQ - 1)
        else:
            bias = jnp.concatenate([aux_ref[...]] * GROUP, axis=0)
        for n in range(KV_HEADS):
            s = _dot_nt(qs_ref[n], k[:, n * HEAD_DIM:(n + 1) * HEAD_DIM]) * ATTN_SCALE
            if moba:
                picked = jnp.sum(jnp.where(lane == j, sel_ref[n], 0.0), axis=-1, keepdims=True) > 0.0
                limit = jnp.where(i == j, t_in, jnp.where(picked, PTQ, -1))
                s = jnp.where(col <= limit, s, NEG)
            else:
                s = s + bias
            _softmax_step(s, v[:, n * HEAD_DIM:(n + 1) * HEAD_DIM], m_ref, l_ref, acc_ref, n)

    @pl.when(j == i)
    def _():
        for n in range(KV_HEADS):
            for g in range(GROUP):
                h = n * GROUP + g
                r = slice(g * PTQ, (g + 1) * PTQ)
                o_ref[:, h * HEAD_DIM:(h + 1) * HEAD_DIM] = (acc_ref[n, r, :] / l_ref[n, r, :]).astype(o_ref.dtype)


def _prompt_attn(moba, proj, aux, t):
    nq = t // PTQ
    qw = MOBA_HEADS * HEAD_DIM
    c_q, c_k, c_v = (C_QM, C_KM, C_VM) if moba else (C_QD, C_KD, C_VD)
    kv_map = lambda i, j: (jnp.minimum(i, j), 0)
    if moba:
        aux_spec = pl.BlockSpec((LANES, KVW), lambda i, j: (0, 0))
    else:
        aux_spec = pl.BlockSpec((PTQ, PTQ), lambda i, j: (i, jnp.minimum(i, j)))
    rows = GROUP * PTQ
    return pl.pallas_call(
        functools.partial(_prompt_attn_kernel, moba),
        grid=(nq, nq),
        in_specs=[
            pl.BlockSpec((PTQ, qw), lambda i, j: (i, c_q // qw)),
            pl.BlockSpec((PTQ, KVW), lambda i, j: (jnp.minimum(i, j), c_k // KVW)),
            pl.BlockSpec((PTQ, KVW), lambda i, j: (jnp.minimum(i, j), c_v // KVW)),
            aux_spec,
        ],
        out_specs=pl.BlockSpec((PTQ, qw), lambda i, j: (i, 0)),
        out_shape=jax.ShapeDtypeStruct((t, qw), BF16),
        scratch_shapes=[
            pltpu.VMEM((KV_HEADS, rows, HEAD_DIM), BF16),
            pltpu.VMEM((KV_HEADS, rows, 1), F32),
            pltpu.VMEM((KV_HEADS, rows, 1), F32),
            pltpu.VMEM((KV_HEADS, rows, HEAD_DIM), F32),
            pltpu.VMEM((KV_HEADS, rows, LANES), F32),
        ],
        compiler_params=_cparams(("parallel", "arbitrary")),
        name="moba_prompt" if moba else "dsa_prompt",
    )(proj, proj, proj, aux)


DTK = 512


def _dsa_select_kernel(k_keep, qi_ref, ki_ref, wi_ref, bias_ref, qs_ref, key_ref):
    i = pl.program_id(0)
    j = pl.program_id(1)
    nj = pl.num_programs(1)
    length = key_ref.shape[1]

    @pl.when(j == 0)
    def _():
        qs_ref[...] = qi_ref[...].astype(BF16)
        key_ref[...] = jnp.full(key_ref.shape, INT_MIN, jnp.int32)

    @pl.when(j * DTK <= i * PTQ + PTQ - 1)
    def _():
        kb = ki_ref[...].astype(BF16)
        w = wi_ref[...]
        acc = jnp.zeros((PTQ, DTK), F32)
        for h in range(IDX_HEADS):
            s = _dot_nt(qs_ref[:, h * IDX_DIM:(h + 1) * IDX_DIM], kb)
            acc = acc + w[:, h:h + 1] * jnp.maximum(s, 0.0)
        row = i * PTQ + lax.broadcasted_iota(jnp.int32, (PTQ, DTK), 0)
        col = j * DTK + lax.broadcasted_iota(jnp.int32, (PTQ, DTK), 1)
        x = jnp.where(col <= row, acc, -jnp.inf)
        key_ref[:, pl.ds(pl.multiple_of(j * DTK, DTK), DTK)] = _sortable_key(x)

    @pl.when(j == nj - 1)
    def _():
        def count_ge(cand):
            return jnp.sum((key_ref[...] >= cand).astype(jnp.int32), axis=-1, keepdims=True)

        thr = _kth_largest_key(count_ge, (PTQ, 1), k_keep)
        row = i * PTQ + lax.broadcasted_iota(jnp.int32, (PTQ, length), 0)
        col = lax.broadcasted_iota(jnp.int32, (PTQ, length), 1)
        ok = jnp.logical_and(key_ref[...] >= thr, col <= row)
        bias_ref[...] = jnp.where(ok, 0.0, NEG)


def _dsa_select(proj, t):
    nq, nj = t // PTQ, t // DTK
    k_keep = max(1, min(DSA_TOPK_MAX, t // 4))
    qw = IDX_HEADS * IDX_DIM
    last_j = lambda i: (i * PTQ + PTQ - 1) // DTK
    return pl.pallas_call(
        functools.partial(_dsa_select_kernel, k_keep),
        grid=(nq, nj),
        in_specs=[
            pl.BlockSpec((PTQ, qw), lambda i, j: (i, C_QI // qw)),
            pl.BlockSpec((DTK, IDX_DIM), lambda i, j: (jnp.minimum(j, last_j(i)), C_KI // IDX_DIM)),
            pl.BlockSpec((PTQ, LANES), lambda i, j: (i, C_WI // LANES)),
        ],
        out_specs=pl.BlockSpec((PTQ, t), lambda i, j: (i, 0)),
        out_shape=jax.ShapeDtypeStruct((t, t), F32),
        scratch_shapes=[pltpu.VMEM((PTQ, qw), BF16), pltpu.VMEM((PTQ, t), jnp.int32)],
        compiler_params=_cparams(("parallel", "arbitrary")),
        name="dsa_select",
    )(proj, proj, proj)


N_PAGES = PAST_LEN // PAGE_SIZE
PAGES_PER_BLOCK = MOBA_BLOCK // PAGE_SIZE
SROWS = GROUP * 4


def _stack_heads(q_ref, qs_ref, dec):
    for n in range(KV_HEADS):
        for g in range(GROUP):
            h = n * GROUP + g
            qs_ref[n, g * dec:(g + 1) * dec, :] = q_ref[:, h * HEAD_DIM:(h + 1) * HEAD_DIM]


def _unstack_heads(o_ref, acc_ref, l_ref, dec):
    for n in range(KV_HEADS):
        for g in range(GROUP):
            h = n * GROUP + g
            r = slice(g * dec, (g + 1) * dec)
            o_ref[:, h * HEAD_DIM:(h + 1) * HEAD_DIM] = acc_ref[n, r, :] / l_ref[n, r, :]


def _sample_moba_kernel(pt_ref, q_ref, kn_ref, vn_ref, ck_ref, cv_ref, o_ref,
                        qs_ref, ksum_ref, sel_ref, m_ref, l_ref, acc_ref, nk_ref, nv_ref):
    s = pl.program_id(1)
    dec = q_ref.shape[0]
    rows = GROUP * dec
    own = PAST_LEN // MOBA_BLOCK
    lane = lax.broadcasted_iota(jnp.int32, (rows, LANES), 1)

    @pl.when(s == 0)
    def _():
        _stack_heads(q_ref, qs_ref, dec)
        ksum_ref[...] = jnp.zeros(ksum_ref.shape, F32)
        m_ref[...] = jnp.full(m_ref.shape, NEG, F32)
        l_ref[...] = jnp.zeros(l_ref.shape, F32)
        acc_ref[...] = jnp.zeros(acc_ref.shape, F32)
        nk_ref[...] = jnp.zeros(nk_ref.shape, F32)
        nv_ref[...] = jnp.zeros(nv_ref.shape, F32)
        nk_ref[0:dec, :] = kn_ref[...]
        nv_ref[0:dec, :] = vn_ref[...]

    @pl.when(s < N_PAGES)
    def _():
        blk = s // PAGES_PER_BLOCK
        ksum_ref[pl.ds(blk, 1), :] += jnp.sum(ck_ref[...], axis=0, keepdims=True)

    @pl.when(s == N_PAGES)
    def _():
        kmean = (ksum_ref[...] * (1.0 / MOBA_BLOCK)).astype(BF16)
        for n in range(KV_HEADS):
            gate = _dot_nt(qs_ref[n].astype(BF16), kmean[:, n * HEAD_DIM:(n + 1) * HEAD_DIM])
            gate = jnp.where(lane < own, gate, -jnp.inf)
            sel_ref[n] = _top_blocks(gate, lane, MOBA_TOPK)

    def attend(k, v, limit_of):
        for n in range(KV_HEADS):
            sc = _dot_nt(qs_ref[n].astype(BF16), k[:, n * HEAD_DIM:(n + 1) * HEAD_DIM]) * ATTN_SCALE
            sc = jnp.where(lane <= limit_of(n), sc, NEG)
            _softmax_step(sc, v[:, n * HEAD_DIM:(n + 1) * HEAD_DIM], m_ref, l_ref, acc_ref, n)

    @pl.when(s >= N_PAGES)
    def _():
        blk = (s - N_PAGES) // PAGES_PER_BLOCK

        def limit_of(n):
            picked = jnp.sum(jnp.where(lane == blk, sel_ref[n], 0.0), axis=-1, keepdims=True) > 0.0
            return jnp.where(picked, PAGE_SIZE, -1)

        attend(ck_ref[...].astype(BF16), cv_ref[...].astype(BF16), limit_of)

    @pl.when(s == 2 * N_PAGES - 1)
    def _():
        t_in = lax.broadcasted_iota(jnp.int32, (rows, 1), 0) % dec
        attend(nk_ref[...].astype(BF16), nv_ref[...].astype(BF16), lambda n: t_in)
        _unstack_heads(o_ref, acc_ref, l_ref, dec)


def _sample_moba(pt, q, kn, vn, ck, cv):
    nb, dec, qw = q.shape
    rows = GROUP * dec
    page_k = lambda b, s, pt: (pt[b, s % N_PAGES], 0, 0)
    page_v = lambda b, s, pt: (pt[b, jnp.maximum(s - N_PAGES, 0)], 0, 0)
    seq = lambda b, s, pt: (b, 0, 0)
    return pl.pallas_call(
        _sample_moba_kernel,
        grid_spec=pltpu.PrefetchScalarGridSpec(
            num_scalar_prefetch=1,
            grid=(nb, 2 * N_PAGES),
            in_specs=[
                pl.BlockSpec((None, dec, qw), seq),
                pl.BlockSpec((None, dec, KVW), seq),
                pl.BlockSpec((None, dec, KVW), seq),
                pl.BlockSpec((None, PAGE_SIZE, KVW), page_k),
                pl.BlockSpec((None, PAGE_SIZE, KVW), page_v),
            ],
            out_specs=pl.BlockSpec((None, dec, qw), seq),
            scratch_shapes=[
                pltpu.VMEM((KV_HEADS, rows, HEAD_DIM), F32),
                pltpu.VMEM((LANES, KVW), F32),
                pltpu.VMEM((KV_HEADS, rows, LANES), F32),
                pltpu.VMEM((KV_HEADS, rows, 1), F32),
                pltpu.VMEM((KV_HEADS, rows, 1), F32),
                pltpu.VMEM((KV_HEADS, rows, HEAD_DIM), F32),
                pltpu.VMEM((PAGE_SIZE, KVW), F32),
                pltpu.VMEM((PAGE_SIZE, KVW), F32),
            ],
        ),
        out_shape=jax.ShapeDtypeStruct((nb, dec, qw), F32),
        compiler_params=_cparams(("parallel", "arbitrary")),
        name="moba_sample",
    )(pt, q, kn, vn, ck, cv)


def _sample_dsa_kernel(k_keep, pt_ref, qi_ref, wi_ref, q_ref, kin_ref, kn_ref, vn_ref, ci_ref, ck_ref, cv_ref, o_ref,
                       qs_ref, idx_ref, bias_ref, m_ref, l_ref, acc_ref, nki_ref, nk_ref, nv_ref):
    s = pl.program_id(1)
    dec = q_ref.shape[0]
    rows = GROUP * dec

    def index_scores(kpage):
        sc = _dot_nt(qi_ref[...].astype(BF16), kpage.astype(BF16))
        sc = jnp.maximum(sc, 0.0) * wi_ref[...]
        return jnp.sum(sc.reshape(dec, IDX_HEADS, PAGE_SIZE), axis=1)

    @pl.when(s == 0)
    def _():
        _stack_heads(q_ref, qs_ref, dec)
        idx_ref[...] = jnp.full(idx_ref.shape, -jnp.inf, F32)
        m_ref[...] = jnp.full(m_ref.shape, NEG, F32)
        l_ref[...] = jnp.zeros(l_ref.shape, F32)
        acc_ref[...] = jnp.zeros(acc_ref.shape, F32)
        nki_ref[...] = jnp.zeros(nki_ref.shape, F32)
        nk_ref[...] = jnp.zeros(nk_ref.shape, F32)
        nv_ref[...] = jnp.zeros(nv_ref.shape, F32)
        nki_ref[0:dec, :] = kin_ref[...]
        nk_ref[0:dec, :] = kn_ref[...]
        nv_ref[0:dec, :] = vn_ref[...]

    @pl.when(s < N_PAGES)
    def _():
        idx_ref[s, 0:dec, :] = index_scores(ci_ref[...])

    @pl.when(s == N_PAGES)
    def _():
        t_in = lax.broadcasted_iota(jnp.int32, (dec, PAGE_SIZE), 0)
        col = lax.broadcasted_iota(jnp.int32, (dec, PAGE_SIZE), 1)
        idx_ref[N_PAGES, 0:dec, :] = jnp.where(col <= t_in, index_scores(nki_ref[...]), -jnp.inf)
        keys = _sortable_key(idx_ref[...])

        def count_ge(cand):
            c = jnp.sum((keys >= cand).astype(jnp.int32), axis=0)
            return jnp.sum(c, axis=-1, keepdims=True)[None]

        thr = _kth_largest_key(count_ge, (1, 8, 1), k_keep)
        ok = jnp.logical_and(keys >= thr, idx_ref[...] > -jnp.inf)
        bias_ref[...] = jnp.where(ok, 0.0, NEG)

    def attend(k, v, page):
        b4 = bias_ref[page, 0:dec, :]
        bias = jnp.concatenate([b4] * GROUP, axis=0)
        for n in range(KV_HEADS):
            sc = _dot_nt(qs_ref[n].astype(BF16), k[:, n * HEAD_DIM:(n + 1) * HEAD_DIM]) * ATTN_SCALE + bias
            _softmax_step(sc, v[:, n * HEAD_DIM:(n + 1) * HEAD_DIM], m_ref, l_ref, acc_ref, n)

    @pl.when(s >= N_PAGES)
    def _():
        attend(ck_ref[...].astype(BF16), cv_ref[...].astype(BF16), s - N_PAGES)

    @pl.when(s == 2 * N_PAGES - 1)
    def _():
        attend(nk_ref[...].astype(BF16), nv_ref[...].astype(BF16), N_PAGES)
        _unstack_heads(o_ref, acc_ref, l_ref, dec)


def _sample_dsa(pt, qi, wi, q, kin, kn, vn, ci, ck, cv):
    nb, dec, qw = q.shape
    rows = GROUP * dec
    k_keep = max(1, min(DSA_TOPK_MAX, (PAST_LEN + dec) // 4))
    page_i = lambda b, s, pt: (pt[b, jnp.minimum(s, N_PAGES - 1)], 0, 0)
    page_kv = lambda b, s, pt: (pt[b, jnp.maximum(s - N_PAGES, 0)], 0, 0)
    seq = lambda b, s, pt: (b, 0, 0)
    return pl.pallas_call(
        functools.partial(_sample_dsa_kernel, k_keep),
        grid_spec=pltpu.PrefetchScalarGridSpec(
            num_scalar_prefetch=1,
            grid=(nb, 2 * N_PAGES),
            in_specs=[
                pl.BlockSpec((None, dec * IDX_HEADS, IDX_DIM), seq),
                pl.BlockSpec((None, dec * IDX_HEADS, 1), seq),
                pl.BlockSpec((None, dec, qw), seq),
                pl.BlockSpec((None, dec, IDX_DIM), seq),
                pl.BlockSpec((None, dec, KVW), seq),
                pl.BlockSpec((None, dec, KVW), seq),
                pl.BlockSpec((None, PAGE_SIZE, IDX_DIM), page_i),
                pl.BlockSpec((None, PAGE_SIZE, KVW), page_kv),
                pl.BlockSpec((None, PAGE_SIZE, KVW), page_kv),
            ],
            out_specs=pl.BlockSpec((None, dec, qw), seq),
            scratch_shapes=[
                pltpu.VMEM((KV_HEADS, rows, HEAD_DIM), F32),
                pltpu.VMEM((N_PAGES + 1, 8, PAGE_SIZE), F32),
                pltpu.VMEM((N_PAGES + 1, 8, PAGE_SIZE), F32),
                pltpu.VMEM((KV_HEADS, rows, 1), F32),
                pltpu.VMEM((KV_HEADS, rows, 1), F32),
                pltpu.VMEM((KV_HEADS, rows, HEAD_DIM), F32),
                pltpu.VMEM((PAGE_SIZE, IDX_DIM), F32),
                pltpu.VMEM((PAGE_SIZE, KVW), F32),
                pltpu.VMEM((PAGE_SIZE, KVW), F32),
            ],
        ),
        out_shape=jax.ShapeDtypeStruct((nb, dec, qw), F32),
        compiler_params=_cparams(("parallel", "arbitrary")),
        name="dsa_sample",
    )(pt, qi, wi, q, kin, kn, vn, ci, ck, cv)


def _outproj_kernel(mix_ref, w_ref, x_ref, g_ref, h_ref, h2_ref, acc_ref):
    k = pl.program_id(1)

    @pl.when(k == 0)
    def _():
        acc_ref[...] = jnp.zeros(acc_ref.shape, F32)

    acc_ref[...] += jnp.dot(mix_ref[...], w_ref[...], preferred_element_type=F32)

    @pl.when(k == pl.num_programs(1) - 1)
    def _():
        h = x_ref[...] + acc_ref[...]
        h_ref[...] = h
        y = h * lax.rsqrt(jnp.mean(h * h, axis=-1, keepdims=True) + NORM_EPS)
        h2_ref[...] = (y * g_ref[...]).astype(h2_ref.dtype)


def _outproj(mix, w, x, g, tm, tk):
    n, d = x.shape
    kdim = mix.shape[1]
    return pl.pallas_call(
        _outproj_kernel,
        grid=(n // tm, kdim // tk),
        in_specs=[
            pl.BlockSpec((tm, tk), lambda i, k: (i, k)),
            pl.BlockSpec((tk, d), lambda i, k: (k, 0)),
            pl.BlockSpec((tm, d), lambda i, k: (i, 0)),
            pl.BlockSpec((1, d), lambda i, k: (0, 0)),
        ],
        out_specs=[pl.BlockSpec((tm, d), lambda i, k: (i, 0)), pl.BlockSpec((tm, d), lambda i, k: (i, 0))],
        out_shape=[jax.ShapeDtypeStruct((n, d), F32), jax.ShapeDtypeStruct((n, d), BF16)],
        scratch_shapes=[pltpu.VMEM((tm, d), F32)],
        compiler_params=_cparams(("parallel", "arbitrary")),
        name="outproj_norm",
    )(mix, w, x, g.reshape(1, d))


def _top_rows(x, k):
    r = x.shape[0]
    ridx = lax.broadcasted_iota(jnp.int32, x.shape, 0)
    outs = []
    for _ in range(k):
        mx = jnp.max(x, axis=0, keepdims=True)
        idx = jnp.min(jnp.where(x == mx, ridx, r), axis=0, keepdims=True)
        outs.append(mx)
        x = jnp.where(ridx == idx, -1.0, x)
    return jnp.concatenate(outs, axis=0)


def _peer_select_kernel(h2_ref, wq_ref, sk_ref, e_ref, thr_ref, iz_ref, acc_ref):
    k = pl.program_id(1)

    @pl.when(k == 0)
    def _():
        acc_ref[...] = jnp.zeros(acc_ref.shape, F32)

    acc_ref[...] += jnp.dot(h2_ref[...], wq_ref[...], preferred_element_type=F32)

    @pl.when(k == pl.num_programs(1) - 1)
    def _():
        half = PEER_DK // 2
        for h in range(PEER_HEADS):
            tops = []
            for p in range(2):
                c0 = (h * 2 + p) * half
                q = acc_ref[:, c0:c0 + half].astype(BF16)
                st = _dot_nt(sk_ref[h, p], q)
                e = jnp.exp(st - jnp.max(st, axis=0, keepdims=True))
                e_ref[h, p] = e
                tops.append(_top_rows(e, PEER_TOPK))
            cand = jnp.concatenate([tops[0][a:a + 1, :] * tops[1] for a in range(PEER_TOPK)], axis=0)
            top = _top_rows(cand, PEER_TOPK)
            thr_ref[h:h + 1, :] = top[PEER_TOPK - 1:PEER_TOPK, :]
            iz_ref[h:h + 1, :] = 1.0 / jnp.sum(top, axis=0, keepdims=True)


def _peer_select(h2, wq, sk, tm, tk):
    n, d = h2.shape
    qw = wq.shape[1]
    return pl.pallas_call(
        _peer_select_kernel,
        grid=(n // tm, d // tk),
        in_specs=[
            pl.BlockSpec((tm, tk), lambda i, k: (i, k)),
            pl.BlockSpec((tk, qw), lambda i, k: (k, 0)),
            pl.BlockSpec(sk.shape, lambda i, k: (0, 0, 0, 0)),
        ],
        out_specs=[
            pl.BlockSpec((PEER_HEADS, 2, PEER_N_KEYS, tm), lambda i, k: (0, 0, 0, i)),
            pl.BlockSpec((PEER_HEADS, tm), lambda i, k: (0, i)),
            pl.BlockSpec((PEER_HEADS, tm), lambda i, k: (0, i)),
        ],
        out_shape=[
            jax.ShapeDtypeStruct((PEER_HEADS, 2, PEER_N_KEYS, n), F32),
            jax.ShapeDtypeStruct((PEER_HEADS, n), F32),
            jax.ShapeDtypeStruct((PEER_HEADS, n), F32),
        ],
        scratch_shapes=[pltpu.VMEM((tm, qw), F32)],
        compiler_params=_cparams(("parallel", "arbitrary")),
        name="peer_select",
    )(h2, wq, sk)


def _peer_dense_kernel(h2_ref, u_ref, v_ref, e_ref, thr_ref, iz_ref, o_ref):
    e = pl.program_id(1)
    te = u_ref.shape[0]
    tm = h2_ref.shape[0]

    @pl.when(e == 0)
    def _():
        o_ref[...] = jnp.zeros(o_ref.shape, F32)

    at = _dot_nt(u_ref[...], h2_ref[...])
    act = 0.5 * at * (1.0 + lax.erf(at * (2.0 ** -0.5)))
    parts = []
    for c in range(te // PEER_N_KEYS):
        i = e * (te // PEER_N_KEYS) + c
        g = jnp.zeros((PEER_N_KEYS, tm), F32)
        for h in range(PEER_HEADS):
            p = e_ref[h, 0, pl.ds(i, 1), :] * e_ref[h, 1]
            g = g + jnp.where(p >= thr_ref[h:h + 1, :], p, 0.0) * iz_ref[h:h + 1, :]
        parts.append(g)
    gates = jnp.concatenate(parts, axis=0) if len(parts) > 1 else parts[0]
    ga = (gates * act).astype(BF16)
    o_ref[...] += lax.dot_general(ga, v_ref[...], (((0,), (0,)), ((), ())), preferred_element_type=F32)


def _peer_dense(h2, u, v, e, thr, iz, tm, te):
    n, d = h2.shape
    ne = u.shape[0]
    return pl.pallas_call(
        _peer_dense_kernel,
        grid=(n // tm, ne // te),
        in_specs=[
            pl.BlockSpec((tm, d), lambda i, k: (i, 0)),
            pl.BlockSpec((te, d), lambda i, k: (k, 0)),
            pl.BlockSpec((te, d), lambda i, k: (k, 0)),
            pl.BlockSpec((PEER_HEADS, 2, PEER_N_KEYS, tm), lambda i, k: (0, 0, 0, i)),
            pl.BlockSpec((PEER_HEADS, tm), lambda i, k: (0, i)),
            pl.BlockSpec((PEER_HEADS, tm), lambda i, k: (0, i)),
        ],
        out_specs=pl.BlockSpec((tm, d), lambda i, k: (i, 0)),
        out_shape=jax.ShapeDtypeStruct((n, d), F32),
        compiler_params=_cparams(("parallel", "arbitrary")),
        name="peer_dense",
    )(h2, u, v, e, thr, iz)


def _final_kernel(h_ref, f_ref, g_ref, o_ref):
    x = h_ref[...] + f_ref[...]
    y = x * lax.rsqrt(jnp.mean(x * x, axis=-1, keepdims=True) + NORM_EPS)
    o_ref[...] = y * g_ref[...]


def _final(h, f, g, tm):
    n, d = h.shape
    return pl.pallas_call(
        _final_kernel,
        grid=(n // tm,),
        in_specs=[pl.BlockSpec((tm, d), lambda i: (i, 0)), pl.BlockSpec((tm, d), lambda i: (i, 0)),
                  pl.BlockSpec((1, d), lambda i: (0, 0))],
        out_specs=pl.BlockSpec((tm, d), lambda i: (i, 0)),
        out_shape=jax.ShapeDtypeStruct((n, d), F32),
        compiler_params=_cparams(("parallel",)),
        name="final_norm",
    )(h, f, g.reshape(1, d))


def _permute_w_in(w_in):
    sizes = (MOBA_HEADS * HEAD_DIM, KVW, KVW, DSA_HEADS * HEAD_DIM, KVW, KVW, IDX_HEADS * IDX_DIM, IDX_DIM, IDX_HEADS)
    q_m, k_m, v_m, q_d, k_d, v_d, q_i, k_i, w_i = jnp.split(w_in, tuple(int(c) for c in np.cumsum(sizes)[:-1]), axis=-1)
    w = jnp.concatenate([q_i, q_m, q_d, k_m, v_m, k_d, v_d, k_i, w_i], axis=-1)
    return jnp.pad(w, ((0, 0), (0, PROJ_PAD - w.shape[1]))).astype(BF16)


def _trunk(x, pos, mix_fn, norm_mix, w_in_p, w_out_b, norm_ffn, wq_b, sk_b, u_b, v_b, norm_final, tm):
    cos, sin = _rope_tables(pos)
    u = _rmsnorm(x, norm_mix, BF16, tm)
    proj = _project(u, w_in_p, cos, sin, tm)
    mix = mix_fn(proj)
    h, h2 = _outproj(mix, w_out_b, x, norm_ffn, 256, 512)
    e, thr, iz = _peer_select(h2, wq_b, sk_b, tm, 512)
    f = _peer_dense(h2, u_b, v_b, e, thr, iz, tm, 256)
    y = _final(h, f, norm_final, 256)
    return y, proj


def kernel(x_prompt, x_sample, cache_moba_k, cache_moba_v, cache_dsa_k, cache_dsa_v, cache_idx_k, page_table,
           norm_mix, w_in, w_out, norm_ffn, peer_w_q, peer_subkeys, peer_u, peer_v, norm_final):
    depth = w_in.shape[0]
    assert depth == 1 and x_prompt.shape[0] == 1
    assert PAST_LEN % MOBA_BLOCK == 0 and page_table.shape[1] == N_PAGES
    t = x_prompt.shape[1]
    nb, dec, d = x_sample.shape
    n_tok = t + nb * dec
    tm = 512
    assert n_tok % tm == 0 and t % tm == 0

    x = jnp.concatenate([x_prompt.reshape(t, d), x_sample.reshape(nb * dec, d)], axis=0)
    pos = jnp.concatenate([jnp.arange(t, dtype=jnp.int32),
                           PAST_LEN + jnp.tile(jnp.arange(dec, dtype=jnp.int32), nb)])

    n_pool = cache_moba_k.shape[1]
    c_mk = cache_moba_k[0].reshape(n_pool, PAGE_SIZE, KVW)
    c_mv = cache_moba_v[0].reshape(n_pool, PAGE_SIZE, KVW)
    c_dk = cache_dsa_k[0].reshape(n_pool, PAGE_SIZE, KVW)
    c_dv = cache_dsa_v[0].reshape(n_pool, PAGE_SIZE, KVW)
    c_ik = cache_idx_k[0]

    def mix_fn(proj):
        kmean = _kmean(proj, t)
        kmean = jnp.pad(kmean, ((0, LANES - kmean.shape[0]), (0, 0)))
        o_m = _prompt_attn(True, proj, kmean, t)
        bias = _dsa_select(proj, t)
        o_d = _prompt_attn(False, proj, bias, t)
        ps = proj[t:].reshape(nb, dec, PROJ_PAD)
        seg = lambda c, w: ps[:, :, c:c + w]
        s_m = _sample_moba(page_table, seg(C_QM, MOBA_HEADS * HEAD_DIM), seg(C_KM, KVW), seg(C_VM, KVW), c_mk, c_mv)
        qi = seg(C_QI, IDX_HEADS * IDX_DIM).reshape(nb, dec * IDX_HEADS, IDX_DIM)
        wi = seg(C_WI, IDX_HEADS).reshape(nb, dec * IDX_HEADS, 1)
        s_d = _sample_dsa(page_table, qi, wi, seg(C_QD, DSA_HEADS * HEAD_DIM), seg(C_KI, IDX_DIM),
                          seg(C_KD, KVW), seg(C_VD, KVW), c_ik, c_dk, c_dv)
        mix_p = jnp.concatenate([o_m, o_d], axis=-1)
        mix_s = jnp.concatenate([s_m, s_d], axis=-1).reshape(nb * dec, d).astype(BF16)
        return jnp.concatenate([mix_p, mix_s], axis=0)

    y, proj = _trunk(x, pos, mix_fn, norm_mix[0], _permute_w_in(w_in[0]), w_out[0].astype(BF16), norm_ffn[0],
                     peer_w_q[0].astype(BF16), peer_subkeys[0].astype(BF16), peer_u[0].astype(BF16),
                     peer_v[0].astype(BF16), norm_final, tm)

    y_prompt = y[:t].reshape(1, t, d)
    y_sample = y[t:].reshape(nb, dec, d)
    pp = proj[:t]
    ps = proj[t:]
    kv = lambda a, c: a[:, c:c + KVW]
    outs_p = [kv(pp, c).reshape(1, 1, t, KV_HEADS, HEAD_DIM) for c in (C_KM, C_VM, C_KD, C_VD)]
    outs_p.append(pp[:, C_KI:C_KI + IDX_DIM].reshape(1, 1, t, IDX_DIM))
    outs_s = [kv(ps, c).reshape(1, nb, dec, KV_HEADS, HEAD_DIM) for c in (C_KM, C_VM, C_KD, C_VD)]
    outs_s.append(ps[:, C_KI:C_KI + IDX_DIM].reshape(1, nb, dec, IDX_DIM))
    return (y_prompt, y_sample, *outs_p, *outs_s)
```

```python
import functools

import numpy as np
import jax
import jax.numpy as jnp
from jax import lax
from jax.experimental import pallas as pl
from jax.experimental.pallas import tpu as pltpu

F32 = jnp.float32
BF16 = jnp.bfloat16

D_MODEL = 4096
HEAD_DIM = 128
MOBA_HEADS = 16
DSA_HEADS = 16
KV_HEADS = 4
GROUP = 4
MOBA_BLOCK = 256
MOBA_TOPK = 3
IDX_HEADS = 32
IDX_DIM = 128
DSA_TOPK_MAX = 256
ROPE_THETA = 10000.0
ATTN_SCALE = HEAD_DIM ** -0.5
IDX_W_SCALE = (IDX_HEADS ** -0.5) * (IDX_DIM ** -0.5)
PEER_HEADS = 8
PEER_N_KEYS = 128
PEER_DK = 256
PEER_TOPK = 16
NORM_EPS = 1e-6
PAST_LEN = 2048
PAGE_SIZE = 128

LANES = 128
SUBLANES = 8
NEG = -1e30
INT_MIN = -2 ** 31
VMEM_LIMIT = 56 * 1024 * 1024

C_QI = 0
C_QM = C_QI + IDX_HEADS * IDX_DIM
C_QD = C_QM + MOBA_HEADS * HEAD_DIM
C_KM = C_QD + DSA_HEADS * HEAD_DIM
C_VM = C_KM + KV_HEADS * HEAD_DIM
C_KD = C_VM + KV_HEADS * HEAD_DIM
C_VD = C_KD + KV_HEADS * HEAD_DIM
C_KI = C_VD + KV_HEADS * HEAD_DIM
C_WI = C_KI + IDX_DIM
PROJ_TN = 512
PROJ_PAD = 10752
KVW = KV_HEADS * HEAD_DIM


def _cparams(sem):
    return pltpu.CompilerParams(dimension_semantics=sem, vmem_limit_bytes=VMEM_LIMIT)


def _dot_nt(a, b):
    return lax.dot_general(a, b, (((1,), (1,)), ((), ())), preferred_element_type=F32)


def _dot(a, b):
    return jnp.dot(a, b, preferred_element_type=F32)


def _rmsnorm_kernel(x_ref, g_ref, o_ref):
    x = x_ref[...]
    y = x * lax.rsqrt(jnp.mean(x * x, axis=-1, keepdims=True) + NORM_EPS)
    o_ref[...] = (y * g_ref[...]).astype(o_ref.dtype)


def _rmsnorm(x, g, out_dtype, tm):
    n, d = x.shape
    return pl.pallas_call(
        _rmsnorm_kernel,
        grid=(n // tm,),
        in_specs=[pl.BlockSpec((tm, d), lambda i: (i, 0)), pl.BlockSpec((1, d), lambda i: (0, 0))],
        out_specs=pl.BlockSpec((tm, d), lambda i: (i, 0)),
        out_shape=jax.ShapeDtypeStruct((n, d), out_dtype),
        compiler_params=_cparams(("parallel",)),
        name="rmsnorm",
    )(x, g.reshape(1, d))


def _rope_group(a, cos, sin):
    return a * cos + pltpu.roll(a, HEAD_DIM // 2, axis=1) * sin


def _proj_kernel(u_ref, w_ref, cos_ref, sin_ref, o_ref):
    j = pl.program_id(1)
    acc = _dot(u_ref[...], w_ref[...])
    plain = jnp.logical_or(j == C_VM // PROJ_TN, j == C_VD // PROJ_TN)
    last = j == C_KI // PROJ_TN
    groups = PROJ_TN // HEAD_DIM

    @pl.when(plain)
    def _():
        o_ref[...] = acc

    @pl.when(jnp.logical_not(jnp.logical_or(plain, last)))
    def _():
        cos = cos_ref[...]
        sin = sin_ref[...]
        for c in range(groups):
            o_ref[:, c * HEAD_DIM:(c + 1) * HEAD_DIM] = _rope_group(acc[:, c * HEAD_DIM:(c + 1) * HEAD_DIM], cos, sin)

    @pl.when(last)
    def _():
        o_ref[:, 0:HEAD_DIM] = _rope_group(acc[:, 0:HEAD_DIM], cos_ref[...], sin_ref[...])
        o_ref[:, HEAD_DIM:2 * HEAD_DIM] = acc[:, HEAD_DIM:2 * HEAD_DIM] * IDX_W_SCALE
        o_ref[:, 2 * HEAD_DIM:] = acc[:, 2 * HEAD_DIM:]


def _project(u, w, cos, sin, tm):
    n, d = u.shape
    return pl.pallas_call(
        _proj_kernel,
        grid=(n // tm, PROJ_PAD // PROJ_TN),
        in_specs=[
            pl.BlockSpec((tm, d), lambda i, j: (i, 0)),
            pl.BlockSpec((d, PROJ_TN), lambda i, j: (0, j)),
            pl.BlockSpec((tm, HEAD_DIM), lambda i, j: (i, 0)),
            pl.BlockSpec((tm, HEAD_DIM), lambda i, j: (i, 0)),
        ],
        out_specs=pl.BlockSpec((tm, PROJ_TN), lambda i, j: (i, j)),
        out_shape=jax.ShapeDtypeStruct((n, PROJ_PAD), F32),
        compiler_params=_cparams(("parallel", "arbitrary")),
        name="proj_rope",
    )(u, w, cos, sin)


def _rope_tables(pos):
    half = HEAD_DIM // 2
    inv = 1.0 / (ROPE_THETA ** (jnp.arange(half, dtype=F32) * (2.0 / HEAD_DIM)))
    ang = pos.astype(F32)[:, None] * inv[None, :]
    c, s = jnp.cos(ang), jnp.sin(ang)
    return jnp.concatenate([c, c], axis=-1), jnp.concatenate([-s, s], axis=-1)


def _kmean_kernel(k_ref, o_ref):
    nb = o_ref.shape[0]
    k = k_ref[...].reshape(nb, MOBA_BLOCK, KVW)
    o_ref[...] = jnp.sum(k, axis=1) * (1.0 / MOBA_BLOCK)


def _kmean(proj, t):
    nb = t // MOBA_BLOCK
    return pl.pallas_call(
        _kmean_kernel,
        grid=(1,),
        in_specs=[pl.BlockSpec((t, KVW), lambda i: (0, C_KM // KVW))],
        out_specs=pl.BlockSpec((nb, KVW), lambda i: (0, 0)),
        out_shape=jax.ShapeDtypeStruct((nb, KVW), F32),
        compiler_params=_cparams(("arbitrary",)),
        name="moba_kmean",
    )(proj)


def _top_block_ids(gate, lane, n_sel):
    ids = []
    for _ in range(n_sel):
        mx = jnp.max(gate, axis=-1, keepdims=True)
        idx = jnp.min(jnp.where(gate == mx, lane, LANES), axis=-1, keepdims=True)
        ids.append(jnp.where(mx > -jnp.inf, idx, -1))
        gate = jnp.where(lane == idx, -jnp.inf, gate)
    return ids


def _sortable_key(x):
    bits = pltpu.bitcast(x + 0.0, jnp.int32)
    return bits ^ ((bits >> 31) & 0x7FFFFFFF)


def _kth_largest_key(count_ge, shape, k):
    t = jnp.where(count_ge(jnp.zeros(shape, jnp.int32)) >= k, 0, INT_MIN).astype(jnp.int32)

    def body(it, t):
        cand = t + (jnp.int32(1) << (30 - it))
        return jnp.where(count_ge(cand) >= k, cand, t)

    return lax.fori_loop(0, 31, body, t)


PTQ = 256
PTK = 512


def _prompt_attn_kernel(moba, q_ref, k_ref, v_ref, aux_ref, o_ref, qs_ref, m_ref, l_ref, acc_ref, sel_ref):
    i = pl.program_id(0)
    j = pl.program_id(1)
    rows = GROUP * PTQ
    last_j = (i * PTQ + PTQ - 1) // PTK
    chunks = PTK // LANES

    @pl.when(j == 0)
    def _():
        for n in range(KV_HEADS):
            for g in range(GROUP):
                h = n * GROUP + g
                qs_ref[n, g * PTQ:(g + 1) * PTQ, :] = q_ref[:, h * HEAD_DIM:(h + 1) * HEAD_DIM].astype(BF16)
        m_ref[...] = jnp.full(m_ref.shape, NEG, F32)
        l_ref[...] = jnp.zeros(l_ref.shape, F32)
        acc_ref[...] = jnp.zeros(acc_ref.shape, F32)
        if moba:
            lane = lax.broadcasted_iota(jnp.int32, (rows, LANES), 1)
            for n in range(KV_HEADS):
                km = aux_ref[:, n * HEAD_DIM:(n + 1) * HEAD_DIM].astype(BF16)
                gate = jnp.where(lane < i, _dot_nt(qs_ref[n], km), -jnp.inf)
                bits = jnp.zeros((rows, 1), jnp.int32)
                for idx in _top_block_ids(gate, lane, MOBA_TOPK):
                    bits = bits | jnp.where(idx >= 0, jnp.int32(1) << jnp.maximum(idx, 0), 0)
                sel_ref[n] = jnp.broadcast_to(bits, (rows, LANES))

    @pl.when(j <= last_j)
    def _():
        k = k_ref[...].astype(BF16)
        v = v_ref[...].astype(BF16)
        ones = jnp.ones((PTK, HEAD_DIM), BF16)
        if moba:
            lane = lax.broadcasted_iota(jnp.int32, (rows, LANES), 1)
            t_in = lax.broadcasted_iota(jnp.int32, (rows, LANES), 0) & (PTQ - 1)
        else:
            bias = jnp.concatenate([aux_ref[...]] * GROUP, axis=0)
        for n in range(KV_HEADS):
            s = _dot_nt(qs_ref[n], k[:, n * HEAD_DIM:(n + 1) * HEAD_DIM]) * ATTN_SCALE
            if moba:
                bits = sel_ref[n]
                limits = []
                for b in range(PTK // MOBA_BLOCK):
                    kb = j * (PTK // MOBA_BLOCK) + b
                    picked = (bits >> jnp.minimum(kb, 31)) & 1
                    limits.append(jnp.where(kb == i, t_in, jnp.where(kb < i, picked * MOBA_BLOCK, 0) - 1))
                parts = []
                for c in range(chunks):
                    col = lane + (c * LANES) % MOBA_BLOCK
                    parts.append(jnp.where(col <= limits[(c * LANES) // MOBA_BLOCK],
                                           s[:, c * LANES:(c + 1) * LANES], NEG))
            else:
                s = s + bias
                parts = [s[:, c * LANES:(c + 1) * LANES] for c in range(chunks)]
            mx = parts[0]
            for c in range(1, chunks):
                mx = jnp.maximum(mx, parts[c])
            m_prev = m_ref[n]
            m_new = jnp.maximum(m_prev, jnp.max(mx, axis=-1, keepdims=True))
            alpha = jnp.exp(m_prev - m_new)
            p = jnp.concatenate([jnp.exp(x - m_new) for x in parts], axis=1).astype(BF16)
            l_ref[n] = alpha * l_ref[n] + _dot(p, ones)
            acc_ref[n] = alpha * acc_ref[n] + _dot(p, v[:, n * HEAD_DIM:(n + 1) * HEAD_DIM])
            m_ref[n] = m_new

    @pl.when(j == last_j)
    def _():
        for n in range(KV_HEADS):
            for g in range(GROUP):
                h = n * GROUP + g
                r = slice(g * PTQ, (g + 1) * PTQ)
                o_ref[:, h * HEAD_DIM:(h + 1) * HEAD_DIM] = (acc_ref[n, r, :] / l_ref[n, r, :]).astype(o_ref.dtype)


def _prompt_attn(moba, proj, aux, t):
    nq, nk = t // PTQ, t // PTK
    assert t // MOBA_BLOCK <= 32
    qw = MOBA_HEADS * HEAD_DIM
    c_q, c_k, c_v = (C_QM, C_KM, C_VM) if moba else (C_QD, C_KD, C_VD)
    jj = lambda i, j: jnp.minimum(j, (i * PTQ + PTQ - 1) // PTK)
    if moba:
        aux_spec = pl.BlockSpec((LANES, KVW), lambda i, j: (0, 0))
    else:
        aux_spec = pl.BlockSpec((PTQ, PTK), lambda i, j: (i, jj(i, j)))
    rows = GROUP * PTQ
    return pl.pallas_call(
        functools.partial(_prompt_attn_kernel, moba),
        grid=(nq, nk),
        in_specs=[
            pl.BlockSpec((PTQ, qw), lambda i, j: (i, c_q // qw)),
            pl.BlockSpec((PTK, KVW), lambda i, j: (jj(i, j), c_k // KVW)),
            pl.BlockSpec((PTK, KVW), lambda i, j: (jj(i, j), c_v // KVW)),
            aux_spec,
        ],
        out_specs=pl.BlockSpec((PTQ, qw), lambda i, j: (i, 0)),
        out_shape=jax.ShapeDtypeStruct((t, qw), BF16),
        scratch_shapes=[
            pltpu.VMEM((KV_HEADS, rows, HEAD_DIM), BF16),
            pltpu.VMEM((KV_HEADS, rows, LANES), F32),
            pltpu.VMEM((KV_HEADS, rows, LANES), F32),
            pltpu.VMEM((KV_HEADS, rows, HEAD_DIM), F32),
            pltpu.VMEM((KV_HEADS, rows, LANES), jnp.int32),
        ],
        compiler_params=_cparams(("parallel", "arbitrary")),
        name="moba_prompt" if moba else "dsa_prompt",
    )(proj, proj, proj, aux)


DTK = 512


def _dsa_select_kernel(k_keep, qi_ref, ki_ref, wi_ref, bias_ref, qs_ref, key_ref):
    i = pl.program_id(0)
    j = pl.program_id(1)
    nj = pl.num_programs(1)
    length = key_ref.shape[1]

    @pl.when(j == 0)
    def _():
        qs_ref[...] = qi_ref[...].astype(BF16)
        key_ref[...] = jnp.full(key_ref.shape, INT_MIN, jnp.int32)

    @pl.when(j * DTK <= i * PTQ + PTQ - 1)
    def _():
        kb = ki_ref[...].astype(BF16)
        w = wi_ref[...]
        acc = jnp.zeros((PTQ, DTK), F32)
        for h in range(IDX_HEADS):
            s = _dot_nt(qs_ref[:, h * IDX_DIM:(h + 1) * IDX_DIM], kb)
            acc = acc + w[:, h:h + 1] * jnp.maximum(s, 0.0)
        row = i * PTQ + lax.broadcasted_iota(jnp.int32, (PTQ, DTK), 0)
        col = j * DTK + lax.broadcasted_iota(jnp.int32, (PTQ, DTK), 1)
        x = jnp.where(col <= row, acc, -jnp.inf)
        key_ref[:, pl.ds(pl.multiple_of(j * DTK, DTK), DTK)] = _sortable_key(x)

    @pl.when(j == nj - 1)
    def _():
        n_tiles = (i * PTQ + PTQ - 1) // DTK + 1

        def count_ge(cand):
            def tile(c, acc):
                keys = key_ref[:, pl.ds(pl.multiple_of(c * DTK, DTK), DTK)]
                return acc + jnp.sum((keys >= cand).astype(jnp.int32), axis=-1, keepdims=True)

            return lax.fori_loop(0, n_tiles, tile, jnp.zeros((PTQ, 1), jnp.int32))

        thr = _kth_largest_key(count_ge, (PTQ, 1), k_keep)
        row = i * PTQ + lax.broadcasted_iota(jnp.int32, (PTQ, length), 0)
        col = lax.broadcasted_iota(jnp.int32, (PTQ, length), 1)
        ok = jnp.logical_and(key_ref[...] >= thr, col <= row)
        bias_ref[...] = jnp.where(ok, 0.0, NEG)


def _dsa_select(proj, t):
    nq, nj = t // PTQ, t // DTK
    k_keep = max(1, min(DSA_TOPK_MAX, t // 4))
    qw = IDX_HEADS * IDX_DIM
    last_j = lambda i: (i * PTQ + PTQ - 1) // DTK
    return pl.pallas_call(
        functools.partial(_dsa_select_kernel, k_keep),
        grid=(nq, nj),
        in_specs=[
            pl.BlockSpec((PTQ, qw), lambda i, j: (i, C_QI // qw)),
            pl.BlockSpec((DTK, IDX_DIM), lambda i, j: (jnp.minimum(j, last_j(i)), C_KI // IDX_DIM)),
            pl.BlockSpec((PTQ, LANES), lambda i, j: (i, C_WI // LANES)),
        ],
        out_specs=pl.BlockSpec((PTQ, t), lambda i, j: (i, 0)),
        out_shape=jax.ShapeDtypeStruct((t, t), F32),
        scratch_shapes=[pltpu.VMEM((PTQ, qw), BF16), pltpu.VMEM((PTQ, t), jnp.int32)],
        compiler_params=_cparams(("parallel", "arbitrary")),
        name="dsa_select",
    )(proj, proj, proj)


N_PAGES = PAST_LEN // PAGE_SIZE
PAGES_PER_BLOCK = MOBA_BLOCK // PAGE_SIZE
PAGE_ROWS = PAGE_SIZE * KV_HEADS


def _head_match(rows, cols, dec):
    r = lax.broadcasted_iota(jnp.int32, (rows, cols), 0)
    c = lax.broadcasted_iota(jnp.int32, (rows, cols), 1)
    shift = (GROUP * dec).bit_length() - 1
    return (r >> shift) == (c & (KV_HEADS - 1))


def _stage_new(dst_ref, src_ref):
    dst_ref[...] = jnp.zeros(dst_ref.shape, F32)
    dst_ref[0:src_ref.shape[0], :] = src_ref[...]


def _softmax_over_pages(s_ref, s_new, v_refs, nv_ref, o_ref):
    m = jnp.max(s_new, axis=-1, keepdims=True)
    for p in range(N_PAGES):
        m = jnp.maximum(m, jnp.max(s_ref[p], axis=-1, keepdims=True))
    pn = jnp.exp(s_new - m)
    l = jnp.sum(pn, axis=-1, keepdims=True)
    acc = _dot(pn.astype(BF16), nv_ref[...].astype(BF16))
    for p in range(N_PAGES):
        pr = jnp.exp(s_ref[p] - m)
        l = l + jnp.sum(pr, axis=-1, keepdims=True)
        acc = acc + _dot(pr.astype(BF16), v_refs[p][...].astype(BF16))
    o_ref[...] = acc / l


def _sample_moba_kernel(pt_ref, q_ref, kn_ref, vn_ref, *rest):
    k_refs, v_refs = rest[:N_PAGES], rest[N_PAGES:2 * N_PAGES]
    o_ref, s_ref, nk_ref, nv_ref = rest[2 * N_PAGES:]
    rows = q_ref.shape[0]
    dec = rows // MOBA_HEADS
    n_past_blocks = PAST_LEN // MOBA_BLOCK
    q = q_ref[...].astype(BF16)
    match = _head_match(rows, PAGE_ROWS, dec)
    lane = lax.broadcasted_iota(jnp.int32, (rows, LANES), 1)

    for p in range(N_PAGES):
        s_ref[p] = _dot_nt(q, k_refs[p][...].astype(BF16))

    gate = jnp.full((rows, LANES), -jnp.inf, F32)
    for b in range(n_past_blocks):
        tot = s_ref[PAGES_PER_BLOCK * b]
        for c in range(1, PAGES_PER_BLOCK):
            tot = tot + s_ref[PAGES_PER_BLOCK * b + c]
        g = jnp.sum(jnp.where(match, tot, 0.0), axis=-1, keepdims=True) * (1.0 / MOBA_BLOCK)
        gate = jnp.where(lane == b, g, gate)
    ids = _top_block_ids(gate, lane, MOBA_TOPK)

    for b in range(n_past_blocks):
        picked = ids[0] == b
        for idx in ids[1:]:
            picked = jnp.logical_or(picked, idx == b)
        row_bias = jnp.where(picked, 0.0, NEG)
        for c in range(PAGES_PER_BLOCK):
            p = PAGES_PER_BLOCK * b + c
            s_ref[p] = jnp.where(match, s_ref[p] * ATTN_SCALE + row_bias, NEG)

    _stage_new(nk_ref, kn_ref)
    _stage_new(nv_ref, vn_ref)
    rn = lax.broadcasted_iota(jnp.int32, (rows, LANES), 0)
    ok = jnp.logical_and(_head_match(rows, LANES, dec), (lane >> 2) <= (rn & (dec - 1)))
    s_new = jnp.where(ok, _dot_nt(q, nk_ref[...].astype(BF16)) * ATTN_SCALE, NEG)
    _softmax_over_pages(s_ref, s_new, v_refs, nv_ref, o_ref)


def _page_specs(shape, n):
    return [pl.BlockSpec((None,) + shape, functools.partial(lambda b, pt, p: (pt[b, p], 0, 0), p=p)) for p in range(n)]


def _sample_moba(pt, q, kn, vn, ck, cv):
    nb, rows, _ = q.shape
    dec = rows // MOBA_HEADS
    assert dec & (dec - 1) == 0 and dec * KV_HEADS <= LANES and KV_HEADS == 4
    seq = lambda b, pt: (b, 0, 0)
    return pl.pallas_call(
        _sample_moba_kernel,
        grid_spec=pltpu.PrefetchScalarGridSpec(
            num_scalar_prefetch=1,
            grid=(nb,),
            in_specs=[pl.BlockSpec((None, rows, HEAD_DIM), seq),
                      pl.BlockSpec((None, dec * KV_HEADS, HEAD_DIM), seq),
                      pl.BlockSpec((None, dec * KV_HEADS, HEAD_DIM), seq)]
                     + _page_specs((PAGE_ROWS, HEAD_DIM), N_PAGES) + _page_specs((PAGE_ROWS, HEAD_DIM), N_PAGES),
            out_specs=pl.BlockSpec((None, rows, HEAD_DIM), seq),
            scratch_shapes=[
                pltpu.VMEM((N_PAGES, rows, PAGE_ROWS), F32),
                pltpu.VMEM((LANES, HEAD_DIM), F32),
                pltpu.VMEM((LANES, HEAD_DIM), F32),
            ],
        ),
        out_shape=jax.ShapeDtypeStruct((nb, rows, HEAD_DIM), F32),
        compiler_params=_cparams(("parallel",)),
        name="moba_sample",
    )(pt, q, kn, vn, *([ck] * N_PAGES), *([cv] * N_PAGES))


def _sample_dsa_kernel(k_keep, pt_ref, qi_ref, wi_ref, q_ref, kin_ref, kn_ref, vn_ref, *rest):
    i_refs, k_refs, v_refs = rest[:N_PAGES], rest[N_PAGES:2 * N_PAGES], rest[2 * N_PAGES:3 * N_PAGES]
    o_ref, idx_ref, s_ref, nki_ref, nk_ref, nv_ref = rest[3 * N_PAGES:]
    rows = q_ref.shape[0]
    dec = rows // DSA_HEADS
    qi = qi_ref[...].astype(BF16)
    w = wi_ref[...]

    def index_scores(kpage):
        sc = jnp.maximum(_dot_nt(qi, kpage.astype(BF16)), 0.0) * w
        return jnp.sum(sc.reshape(dec, IDX_HEADS, PAGE_SIZE), axis=1)

    idx_ref[...] = jnp.full(idx_ref.shape, -jnp.inf, F32)
    for p in range(N_PAGES):
        idx_ref[p, 0:dec, :] = index_scores(i_refs[p][...])
    _stage_new(nki_ref, kin_ref)
    t_in = lax.broadcasted_iota(jnp.int32, (dec, PAGE_SIZE), 0)
    col = lax.broadcasted_iota(jnp.int32, (dec, PAGE_SIZE), 1)
    idx_ref[N_PAGES, 0:dec, :] = jnp.where(col <= t_in, index_scores(nki_ref[...]), -jnp.inf)

    keys = _sortable_key(idx_ref[...])

    def count_ge(cand):
        c = jnp.sum((keys >= cand).astype(jnp.int32), axis=0)
        return jnp.sum(c, axis=-1, keepdims=True)[None]

    thr = _kth_largest_key(count_ge, (1, SUBLANES, 1), k_keep)
    keep = jnp.where(jnp.logical_and(keys >= thr, idx_ref[...] > -jnp.inf), 1.0, 0.0)

    q = q_ref[...].astype(BF16)
    match = _head_match(rows, PAGE_ROWS, dec)
    er = lax.broadcasted_iota(jnp.int32, (PAGE_SIZE, PAGE_ROWS), 0)
    ec = lax.broadcasted_iota(jnp.int32, (PAGE_SIZE, PAGE_ROWS), 1)
    expand = jnp.where((ec >> 2) == er, 1.0, 0.0).astype(BF16)

    def visible(page, width):
        per_row = jnp.concatenate([keep[page, 0:dec, :]] * (rows // dec), axis=0).astype(BF16)
        return _dot(per_row, expand[:, 0:width]) > 0.5

    for p in range(N_PAGES):
        sc = _dot_nt(q, k_refs[p][...].astype(BF16)) * ATTN_SCALE
        s_ref[p] = jnp.where(jnp.logical_and(match, visible(p, PAGE_ROWS)), sc, NEG)
    _stage_new(nk_ref, kn_ref)
    _stage_new(nv_ref, vn_ref)
    ok = jnp.logical_and(_head_match(rows, LANES, dec), visible(N_PAGES, LANES))
    s_new = jnp.where(ok, _dot_nt(q, nk_ref[...].astype(BF16)) * ATTN_SCALE, NEG)
    _softmax_over_pages(s_ref, s_new, v_refs, nv_ref, o_ref)


def _sample_dsa(pt, qi, wi, q, kin, kn, vn, ci, ck, cv):
    nb, rows, _ = q.shape
    dec = rows // DSA_HEADS
    assert dec & (dec - 1) == 0 and dec <= SUBLANES and KV_HEADS == 4
    k_keep = max(1, min(DSA_TOPK_MAX, (PAST_LEN + dec) // 4))
    seq = lambda b, pt: (b, 0, 0)
    return pl.pallas_call(
        functools.partial(_sample_dsa_kernel, k_keep),
        grid_spec=pltpu.PrefetchScalarGridSpec(
            num_scalar_prefetch=1,
            grid=(nb,),
            in_specs=[pl.BlockSpec((None, dec * IDX_HEADS, IDX_DIM), seq),
                      pl.BlockSpec((None, dec * IDX_HEADS, 1), seq),
                      pl.BlockSpec((None, rows, HEAD_DIM), seq),
                      pl.BlockSpec((None, dec, IDX_DIM), seq),
                      pl.BlockSpec((None, dec * KV_HEADS, HEAD_DIM), seq),
                      pl.BlockSpec((None, dec * KV_HEADS, HEAD_DIM), seq)]
                     + _page_specs((PAGE_SIZE, IDX_DIM), N_PAGES)
                     + _page_specs((PAGE_ROWS, HEAD_DIM), N_PAGES) + _page_specs((PAGE_ROWS, HEAD_DIM), N_PAGES),
            out_specs=pl.BlockSpec((None, rows, HEAD_DIM), seq),
            scratch_shapes=[
                pltpu.VMEM((N_PAGES + 1, SUBLANES, PAGE_SIZE), F32),
                pltpu.VMEM((N_PAGES, rows, PAGE_ROWS), F32),
                pltpu.VMEM((PAGE_SIZE, IDX_DIM), F32),
                pltpu.VMEM((LANES, HEAD_DIM), F32),
                pltpu.VMEM((LANES, HEAD_DIM), F32),
            ],
        ),
        out_shape=jax.ShapeDtypeStruct((nb, rows, HEAD_DIM), F32),
        compiler_params=_cparams(("parallel",)),
        name="dsa_sample",
    )(pt, qi, wi, q, kin, kn, vn, *([ci] * N_PAGES), *([ck] * N_PAGES), *([cv] * N_PAGES))


def _outproj_kernel(mix_ref, w_ref, x_ref, g_ref, h_ref, h2_ref, acc_ref):
    k = pl.program_id(1)

    @pl.when(k == 0)
    def _():
        acc_ref[...] = jnp.zeros(acc_ref.shape, F32)

    acc_ref[...] += _dot(mix_ref[...], w_ref[...])

    @pl.when(k == pl.num_programs(1) - 1)
    def _():
        h = x_ref[...] + acc_ref[...]
        h_ref[...] = h
        y = h * lax.rsqrt(jnp.mean(h * h, axis=-1, keepdims=True) + NORM_EPS)
        h2_ref[...] = (y * g_ref[...]).astype(h2_ref.dtype)


def _outproj(mix, w, x, g, tm, tk):
    n, d = x.shape
    kdim = mix.shape[1]
    return pl.pallas_call(
        _outproj_kernel,
        grid=(n // tm, kdim // tk),
        in_specs=[
            pl.BlockSpec((tm, tk), lambda i, k: (i, k)),
            pl.BlockSpec((tk, d), lambda i, k: (k, 0)),
            pl.BlockSpec((tm, d), lambda i, k: (i, 0)),
            pl.BlockSpec((1, d), lambda i, k: (0, 0)),
        ],
        out_specs=[pl.BlockSpec((tm, d), lambda i, k: (i, 0)), pl.BlockSpec((tm, d), lambda i, k: (i, 0))],
        out_shape=[jax.ShapeDtypeStruct((n, d), F32), jax.ShapeDtypeStruct((n, d), BF16)],
        scratch_shapes=[pltpu.VMEM((tm, d), F32)],
        compiler_params=_cparams(("parallel", "arbitrary")),
        name="outproj_norm",
    )(mix, w, x, g.reshape(1, d))


def _top_rows(x, k):
    r = x.shape[0]
    ridx = lax.broadcasted_iota(jnp.int32, x.shape, 0)
    outs = []
    for _ in range(k):
        mx = jnp.max(x, axis=0, keepdims=True)
        idx = jnp.min(jnp.where(x == mx, ridx, r), axis=0, keepdims=True)
        outs.append(mx)
        x = jnp.where(ridx == idx, -1.0, x)
    return jnp.concatenate(outs, axis=0)


def _peer_select_kernel(h2_ref, wq_ref, sk_ref, e_ref, thr_ref, iz_ref, acc_ref):
    k = pl.program_id(1)

    @pl.when(k == 0)
    def _():
        acc_ref[...] = jnp.zeros(acc_ref.shape, F32)

    acc_ref[...] += _dot(h2_ref[...], wq_ref[...])

    @pl.when(k == pl.num_programs(1) - 1)
    def _():
        half = PEER_DK // 2
        hk = PEER_TOPK // 2
        for h in range(PEER_HEADS):
            tops = []
            for p in range(2):
                c0 = (h * 2 + p) * half
                q = acc_ref[:, c0:c0 + half].astype(BF16)
                st = _dot_nt(sk_ref[h, p], q)
                e = jnp.exp(st - jnp.max(st, axis=0, keepdims=True))
                e_ref[h, p] = e
                tops.append(_top_rows(e, PEER_TOPK))
            a, b = tops
            cand = [a[x:x + 1, :] * b[0:hk, :] for x in range(hk)]
            cand += [a[0:1, :] * b[hk:, :], a[hk:, :] * b[0:1, :]]
            top = _top_rows(jnp.concatenate(cand, axis=0), PEER_TOPK)
            thr_ref[h:h + 1, :] = top[PEER_TOPK - 1:PEER_TOPK, :]
            iz_ref[h:h + 1, :] = 1.0 / jnp.sum(top, axis=0, keepdims=True)


def _peer_select(h2, wq, sk, tm, tk):
    n, d = h2.shape
    qw = wq.shape[1]
    return pl.pallas_call(
        _peer_select_kernel,
        grid=(n // tm, d // tk),
        in_specs=[
            pl.BlockSpec((tm, tk), lambda i, k: (i, k)),
            pl.BlockSpec((tk, qw), lambda i, k: (k, 0)),
            pl.BlockSpec(sk.shape, lambda i, k: (0, 0, 0, 0)),
        ],
        out_specs=[
            pl.BlockSpec((PEER_HEADS, 2, PEER_N_KEYS, tm), lambda i, k: (0, 0, 0, i)),
            pl.BlockSpec((PEER_HEADS, tm), lambda i, k: (0, i)),
            pl.BlockSpec((PEER_HEADS, tm), lambda i, k: (0, i)),
        ],
        out_shape=[
            jax.ShapeDtypeStruct((PEER_HEADS, 2, PEER_N_KEYS, n), F32),
            jax.ShapeDtypeStruct((PEER_HEADS, n), F32),
            jax.ShapeDtypeStruct((PEER_HEADS, n), F32),
        ],
        scratch_shapes=[pltpu.VMEM((tm, qw), F32)],
        compiler_params=_cparams(("parallel", "arbitrary")),
        name="peer_select",
    )(h2, wq, sk)


PEER_SUB = 256


def _peer_dense_kernel(h2_ref, u_ref, v_ref, e_ref, thr_ref, iz_ref, o_ref):
    e = pl.program_id(1)
    te = u_ref.shape[0]
    tm = h2_ref.shape[0]

    @pl.when(e == 0)
    def _():
        o_ref[...] = jnp.zeros(o_ref.shape, F32)

    h2 = h2_ref[...]
    total = None
    for sub in range(te // PEER_SUB):
        r0 = sub * PEER_SUB
        at = _dot_nt(u_ref[r0:r0 + PEER_SUB, :], h2)
        act = 0.5 * at * (1.0 + lax.erf(at * (2.0 ** -0.5)))
        parts = []
        for c in range(PEER_SUB // PEER_N_KEYS):
            i = (e * te + r0) // PEER_N_KEYS + c
            g = jnp.zeros((PEER_N_KEYS, tm), F32)
            for h in range(PEER_HEADS):
                p = e_ref[h, 0, pl.ds(i, 1), :] * e_ref[h, 1]
                g = g + jnp.where(p >= thr_ref[h:h + 1, :], p, 0.0) * iz_ref[h:h + 1, :]
            parts.append(g)
        ga = (jnp.concatenate(parts, axis=0) * act).astype(BF16)
        d = lax.dot_general(ga, v_ref[r0:r0 + PEER_SUB, :], (((0,), (0,)), ((), ())), preferred_element_type=F32)
        total = d if total is None else total + d
    o_ref[...] += total


def _peer_dense(h2, u, v, e, thr, iz, tm, te):
    n, d = h2.shape
    ne = u.shape[0]
    return pl.pallas_call(
        _peer_dense_kernel,
        grid=(n // tm, ne // te),
        in_specs=[
            pl.BlockSpec((tm, d), lambda i, k: (i, 0)),
            pl.BlockSpec((te, d), lambda i, k: (k, 0)),
            pl.BlockSpec((te, d), lambda i, k: (k, 0)),
            pl.BlockSpec((PEER_HEADS, 2, PEER_N_KEYS, tm), lambda i, k: (0, 0, 0, i)),
            pl.BlockSpec((PEER_HEADS, tm), lambda i, k: (0, i)),
            pl.BlockSpec((PEER_HEADS, tm), lambda i, k: (0, i)),
        ],
        out_specs=pl.BlockSpec((tm, d), lambda i, k: (i, 0)),
        out_shape=jax.ShapeDtypeStruct((n, d), F32),
        compiler_params=_cparams(("parallel", "arbitrary")),
        name="peer_dense",
    )(h2, u, v, e, thr, iz)


def _final_kernel(h_ref, f_ref, g_ref, o_ref):
    x = h_ref[...] + f_ref[...]
    y = x * lax.rsqrt(jnp.mean(x * x, axis=-1, keepdims=True) + NORM_EPS)
    o_ref[...] = y * g_ref[...]


def _final(h, f, g, tm):
    n, d = h.shape
    return pl.pallas_call(
        _final_kernel,
        grid=(n // tm,),
        in_specs=[pl.BlockSpec((tm, d), lambda i: (i, 0)), pl.BlockSpec((tm, d), lambda i: (i, 0)),
                  pl.BlockSpec((1, d), lambda i: (0, 0))],
        out_specs=pl.BlockSpec((tm, d), lambda i: (i, 0)),
        out_shape=jax.ShapeDtypeStruct((n, d), F32),
        compiler_params=_cparams(("parallel",)),
        name="final_norm",
    )(h, f, g.reshape(1, d))


def _permute_w_in(w_in):
    sizes = (MOBA_HEADS * HEAD_DIM, KVW, KVW, DSA_HEADS * HEAD_DIM, KVW, KVW, IDX_HEADS * IDX_DIM, IDX_DIM, IDX_HEADS)
    q_m, k_m, v_m, q_d, k_d, v_d, q_i, k_i, w_i = jnp.split(w_in, tuple(int(c) for c in np.cumsum(sizes)[:-1]), axis=-1)
    w = jnp.concatenate([q_i, q_m, q_d, k_m, v_m, k_d, v_d, k_i, w_i], axis=-1)
    return jnp.pad(w, ((0, 0), (0, PROJ_PAD - w.shape[1]))).astype(BF16)


def _heads_to_rows(q, nb, dec):
    return q.reshape(nb, dec, MOBA_HEADS, HEAD_DIM).transpose(0, 2, 1, 3).reshape(nb, MOBA_HEADS * dec, HEAD_DIM)


def _rows_to_heads(o, nb, dec):
    return o.reshape(nb, MOBA_HEADS, dec, HEAD_DIM).transpose(0, 2, 1, 3).reshape(nb * dec, MOBA_HEADS * HEAD_DIM)


def _sample_mix(proj_s, page_table, c_mk, c_mv, c_dk, c_dv, c_ik, nb, dec):
    ps = proj_s.reshape(nb, dec, PROJ_PAD)
    seg = lambda c, w: ps[:, :, c:c + w]
    kv_rows = lambda c: seg(c, KVW).reshape(nb, dec * KV_HEADS, HEAD_DIM)
    s_m = _sample_moba(page_table, _heads_to_rows(seg(C_QM, MOBA_HEADS * HEAD_DIM), nb, dec),
                       kv_rows(C_KM), kv_rows(C_VM), c_mk, c_mv)
    qi = seg(C_QI, IDX_HEADS * IDX_DIM).reshape(nb, dec * IDX_HEADS, IDX_DIM)
    wi = seg(C_WI, IDX_HEADS).reshape(nb, dec * IDX_HEADS, 1)
    s_d = _sample_dsa(page_table, qi, wi, _heads_to_rows(seg(C_QD, DSA_HEADS * HEAD_DIM), nb, dec),
                      seg(C_KI, IDX_DIM), kv_rows(C_KD), kv_rows(C_VD), c_ik, c_dk, c_dv)
    return jnp.concatenate([_rows_to_heads(s_m, nb, dec), _rows_to_heads(s_d, nb, dec)], axis=-1)


def _prompt_mix(proj, t):
    kmean = _kmean(proj, t)
    kmean = jnp.pad(kmean, ((0, LANES - kmean.shape[0]), (0, 0)))
    o_m = _prompt_attn(True, proj, kmean, t)
    o_d = _prompt_attn(False, proj, _dsa_select(proj, t), t)
    return jnp.concatenate([o_m, o_d], axis=-1)


def kernel(x_prompt, x_sample, cache_moba_k, cache_moba_v, cache_dsa_k, cache_dsa_v, cache_idx_k, page_table,
           norm_mix, w_in, w_out, norm_ffn, peer_w_q, peer_subkeys, peer_u, peer_v, norm_final):
    depth = w_in.shape[0]
    assert depth == 1 and x_prompt.shape[0] == 1
    assert PAST_LEN % MOBA_BLOCK == 0 and page_table.shape[1] == N_PAGES
    t = x_prompt.shape[1]
    nb, dec, d = x_sample.shape
    n_tok = t + nb * dec
    tm = 512
    assert n_tok % tm == 0 and t % tm == 0

    x = jnp.concatenate([x_prompt.reshape(t, d), x_sample.reshape(nb * dec, d)], axis=0)
    pos = jnp.concatenate([jnp.arange(t, dtype=jnp.int32),
                           PAST_LEN + jnp.tile(jnp.arange(dec, dtype=jnp.int32), nb)])
    cos, sin = _rope_tables(pos)

    n_pool = cache_moba_k.shape[1]
    flat = lambda c: c[0].reshape(n_pool, PAGE_ROWS, HEAD_DIM)

    u = _rmsnorm(x, norm_mix[0], BF16, tm)
    proj = _project(u, _permute_w_in(w_in[0]), cos, sin, tm)
    mix_p = _prompt_mix(proj, t)
    mix_s = _sample_mix(proj[t:], page_table, flat(cache_moba_k), flat(cache_moba_v), flat(cache_dsa_k),
                        flat(cache_dsa_v), cache_idx_k[0], nb, dec)
    mix = jnp.concatenate([mix_p, mix_s.astype(BF16)], axis=0)
    h, h2 = _outproj(mix, w_out[0].astype(BF16), x, norm_ffn[0], 256, 512)
    e, thr, iz = _peer_select(h2, peer_w_q[0].astype(BF16), peer_subkeys[0].astype(BF16), tm, 512)
    f = _peer_dense(h2, peer_u[0].astype(BF16), peer_v[0].astype(BF16), e, thr, iz, tm, 512)
    y = _final(h, f, norm_final, 256)

    y_prompt = y[:t].reshape(1, t, d)
    y_sample = y[t:].reshape(nb, dec, d)
    pp = proj[:t]
    ps = proj[t:]
    kv = lambda a, c: a[:, c:c + KVW]
    outs_p = [kv(pp, c).reshape(1, 1, t, KV_HEADS, HEAD_DIM) for c in (C_KM, C_VM, C_KD, C_VD)]
    outs_p.append(pp[:, C_KI:C_KI + IDX_DIM].reshape(1, 1, t, IDX_DIM))
    outs_s = [kv(ps, c).reshape(1, nb, dec, KV_HEADS, HEAD_DIM) for c in (C_KM, C_VM, C_KD, C_VD)]
    outs_s.append(ps[:, C_KI:C_KI + IDX_DIM].reshape(1, nb, dec, IDX_DIM))
    return (y_prompt, y_sample, *outs_p, *outs_s)
```

```python
import functools

import numpy as np
import jax
import jax.numpy as jnp
from jax import lax
from jax.experimental import pallas as pl
from jax.experimental.pallas import tpu as pltpu

F32 = jnp.float32
BF16 = jnp.bfloat16

D_MODEL = 4096
HEAD_DIM = 128
MOBA_HEADS = 16
DSA_HEADS = 16
KV_HEADS = 4
GROUP = 4
MOBA_BLOCK = 256
MOBA_TOPK = 3
IDX_HEADS = 32
IDX_DIM = 128
DSA_TOPK_MAX = 256
ROPE_THETA = 10000.0
ATTN_SCALE = HEAD_DIM ** -0.5
IDX_W_SCALE = (IDX_HEADS ** -0.5) * (IDX_DIM ** -0.5)
PEER_HEADS = 8
PEER_N_KEYS = 128
PEER_DK = 256
PEER_TOPK = 16
NORM_EPS = 1e-6
PAST_LEN = 2048
PAGE_SIZE = 128

LANES = 128
SUBLANES = 8
NEG = -1e30
INT_MIN = -2 ** 31
VMEM_LIMIT = 56 * 1024 * 1024

C_QI = 0
C_QM = C_QI + IDX_HEADS * IDX_DIM
C_QD = C_QM + MOBA_HEADS * HEAD_DIM
C_KM = C_QD + DSA_HEADS * HEAD_DIM
C_VM = C_KM + KV_HEADS * HEAD_DIM
C_KD = C_VM + KV_HEADS * HEAD_DIM
C_VD = C_KD + KV_HEADS * HEAD_DIM
C_KI = C_VD + KV_HEADS * HEAD_DIM
C_WI = C_KI + IDX_DIM
PROJ_TN = 512
PROJ_PAD = 10752
KVW = KV_HEADS * HEAD_DIM


def _cparams(sem):
    return pltpu.CompilerParams(dimension_semantics=sem, vmem_limit_bytes=VMEM_LIMIT)


def _dot_nt(a, b):
    return lax.dot_general(a, b, (((1,), (1,)), ((), ())), preferred_element_type=F32)


def _dot(a, b):
    return jnp.dot(a, b, preferred_element_type=F32)


def _rmsnorm_kernel(x_ref, g_ref, o_ref):
    x = x_ref[...]
    y = x * lax.rsqrt(jnp.mean(x * x, axis=-1, keepdims=True) + NORM_EPS)
    o_ref[...] = (y * g_ref[...]).astype(o_ref.dtype)


def _rmsnorm(x, g, out_dtype, tm):
    n, d = x.shape
    return pl.pallas_call(
        _rmsnorm_kernel,
        grid=(n // tm,),
        in_specs=[pl.BlockSpec((tm, d), lambda i: (i, 0)), pl.BlockSpec((1, d), lambda i: (0, 0))],
        out_specs=pl.BlockSpec((tm, d), lambda i: (i, 0)),
        out_shape=jax.ShapeDtypeStruct((n, d), out_dtype),
        compiler_params=_cparams(("parallel",)),
        name="rmsnorm",
    )(x, g.reshape(1, d))


def _rope_group(a, cos, sin):
    return a * cos + pltpu.roll(a, HEAD_DIM // 2, axis=1) * sin


def _proj_kernel(u_ref, w_ref, cos_ref, sin_ref, o_ref):
    j = pl.program_id(1)
    acc = _dot(u_ref[...], w_ref[...])
    plain = jnp.logical_or(j == C_VM // PROJ_TN, j == C_VD // PROJ_TN)
    last = j == C_KI // PROJ_TN
    groups = PROJ_TN // HEAD_DIM

    @pl.when(plain)
    def _():
        o_ref[...] = acc

    @pl.when(jnp.logical_not(jnp.logical_or(plain, last)))
    def _():
        cos = cos_ref[...]
        sin = sin_ref[...]
        for c in range(groups):
            o_ref[:, c * HEAD_DIM:(c + 1) * HEAD_DIM] = _rope_group(acc[:, c * HEAD_DIM:(c + 1) * HEAD_DIM], cos, sin)

    @pl.when(last)
    def _():
        o_ref[:, 0:HEAD_DIM] = _rope_group(acc[:, 0:HEAD_DIM], cos_ref[...], sin_ref[...])
        o_ref[:, HEAD_DIM:2 * HEAD_DIM] = acc[:, HEAD_DIM:2 * HEAD_DIM] * IDX_W_SCALE
        o_ref[:, 2 * HEAD_DIM:] = acc[:, 2 * HEAD_DIM:]


def _project(u, w, cos, sin, tm):
    n, d = u.shape
    return pl.pallas_call(
        _proj_kernel,
        grid=(n // tm, PROJ_PAD // PROJ_TN),
        in_specs=[
            pl.BlockSpec((tm, d), lambda i, j: (i, 0)),
            pl.BlockSpec((d, PROJ_TN), lambda i, j: (0, j)),
            pl.BlockSpec((tm, HEAD_DIM), lambda i, j: (i, 0)),
            pl.BlockSpec((tm, HEAD_DIM), lambda i, j: (i, 0)),
        ],
        out_specs=pl.BlockSpec((tm, PROJ_TN), lambda i, j: (i, j)),
        out_shape=jax.ShapeDtypeStruct((n, PROJ_PAD), F32),
        compiler_params=_cparams(("parallel", "arbitrary")),
        name="proj_rope",
    )(u, w, cos, sin)


def _rope_tables(pos):
    half = HEAD_DIM // 2
    inv = 1.0 / (ROPE_THETA ** (jnp.arange(half, dtype=F32) * (2.0 / HEAD_DIM)))
    ang = pos.astype(F32)[:, None] * inv[None, :]
    c, s = jnp.cos(ang), jnp.sin(ang)
    return jnp.concatenate([c, c], axis=-1), jnp.concatenate([-s, s], axis=-1)


def _kmean_kernel(k_ref, o_ref):
    nb = o_ref.shape[0]
    k = k_ref[...].reshape(nb, MOBA_BLOCK, KVW)
    o_ref[...] = jnp.sum(k, axis=1) * (1.0 / MOBA_BLOCK)


def _kmean(proj, t):
    nb = t // MOBA_BLOCK
    return pl.pallas_call(
        _kmean_kernel,
        grid=(1,),
        in_specs=[pl.BlockSpec((t, KVW), lambda i: (0, C_KM // KVW))],
        out_specs=pl.BlockSpec((nb, KVW), lambda i: (0, 0)),
        out_shape=jax.ShapeDtypeStruct((nb, KVW), F32),
        compiler_params=_cparams(("arbitrary",)),
        name="moba_kmean",
    )(proj)


def _top_block_ids(gate, lane, n_sel):
    ids = []
    for _ in range(n_sel):
        mx = jnp.max(gate, axis=-1, keepdims=True)
        idx = jnp.min(jnp.where(gate == mx, lane, LANES), axis=-1, keepdims=True)
        ids.append(jnp.where(mx > -jnp.inf, idx, -1))
        gate = jnp.where(lane == idx, -jnp.inf, gate)
    return ids


def _sortable_key(x):
    bits = pltpu.bitcast(x + 0.0, jnp.int32)
    return bits ^ ((bits >> 31) & 0x7FFFFFFF)


def _kth_largest_key(count_ge, shape, k):
    t = jnp.broadcast_to(jnp.where(count_ge(jnp.zeros(shape, jnp.int32)) >= k, 0, INT_MIN).astype(jnp.int32), shape)

    def body(it, t):
        cand = t + (jnp.int32(1) << (30 - it))
        return jnp.where(count_ge(cand) >= k, cand, t)

    return lax.fori_loop(0, 31, body, t)


PTQ = 256
PTK = 512


def _prompt_attn_kernel(moba, q_ref, k_ref, v_ref, aux_ref, o_ref, qs_ref, m_ref, acc_ref, sel_ref):
    i = pl.program_id(0)
    j = pl.program_id(1)
    rows = GROUP * PTQ
    last_j = (i * PTQ + PTQ - 1) // PTK
    chunks = PTK // LANES

    @pl.when(j == 0)
    def _():
        for n in range(KV_HEADS):
            for g in range(GROUP):
                h = n * GROUP + g
                qs_ref[n, g * PTQ:(g + 1) * PTQ, :] = q_ref[:, h * HEAD_DIM:(h + 1) * HEAD_DIM].astype(BF16)
        m_ref[...] = jnp.full(m_ref.shape, NEG, F32)
        acc_ref[...] = jnp.zeros(acc_ref.shape, F32)
        if moba:
            lane = lax.broadcasted_iota(jnp.int32, (rows, LANES), 1)
            for n in range(KV_HEADS):
                km =aux_ref[:, n * HEAD_DIM:(n + 1) * HEAD_DIM].astype(BF16)
                gate = jnp.where(lane < i, _dot_nt(qs_ref[n], km), -jnp.inf)
                bits = jnp.zeros((rows, 1), jnp.int32)
                for idx in _top_block_ids(gate, lane, MOBA_TOPK):
                    bits = bits | jnp.where(idx >= 0, jnp.int32(1) << jnp.maximum(idx, 0), 0)
                sel_ref[n] = jnp.broadcast_to(bits, (rows, LANES))

    @pl.when(j <= last_j)
    def _():
        k = k_ref[...].astype(BF16)
        v = v_ref[...].astype(BF16)
        ones = jnp.ones((PTK, HEAD_DIM), BF16)
        if moba:
            lane = lax.broadcasted_iota(jnp.int32, (rows, LANES), 1)
            t_in = lax.broadcasted_iota(jnp.int32, (rows, LANES), 0) & (PTQ - 1)
        else:
            bias = jnp.concatenate([aux_ref[...]] * GROUP, axis=0)
        for n in range(KV_HEADS):
            s = _dot_nt(qs_ref[n], k[:, n * HEAD_DIM:(n + 1) * HEAD_DIM]) * ATTN_SCALE
            if moba:
                bits = sel_ref[n]
                limits = []
                for b in range(PTK // MOBA_BLOCK):
                    kb = j * (PTK // MOBA_BLOCK) + b
                    picked = (bits >> jnp.minimum(kb, 31)) & 1
                    limits.append(jnp.where(kb == i, t_in, jnp.where(kb < i, picked * MOBA_BLOCK, 0) - 1))
                parts = []
                for c in range(chunks):
                    col = lane + (c * LANES) % MOBA_BLOCK
                    parts.append(jnp.where(col <= limits[(c * LANES) // MOBA_BLOCK],
                                           s[:, c * LANES:(c + 1) * LANES], NEG))
            else:
                s = s + bias
                parts = [s[:, c * LANES:(c + 1) * LANES] for c in range(chunks)]
            mx = parts[0]
            for c in range(1, chunks):
                mx = jnp.maximum(mx, parts[c])
            m_prev = m_ref[n]
            m_new = jnp.maximum(m_prev, jnp.max(mx, axis=-1, keepdims=True))
            alpha = jnp.exp(m_prev - m_new)
            p = jnp.concatenate([jnp.exp(x - m_new) for x in parts], axis=1).astype(BF16)
            v_ones = jnp.concatenate([v[:, n * HEAD_DIM:(n + 1) * HEAD_DIM], ones], axis=1)
            acc_ref[n] = jnp.concatenate([alpha, alpha], axis=1) * acc_ref[n] + _dot(p, v_ones)
            m_ref[n] = m_new

    @pl.when(j == last_j)
    def _():
        for n in range(KV_HEADS):
            for g in range(GROUP):
                h = n * GROUP + g
                r = slice(g * PTQ, (g + 1) * PTQ)
                o_ref[:, h * HEAD_DIM:(h + 1) * HEAD_DIM] = (acc_ref[n, r, 0:HEAD_DIM]
                                                             / acc_ref[n, r, HEAD_DIM:]).astype(o_ref.dtype)


def _prompt_attn(moba, proj, aux, t):
    nq, nk = t // PTQ, t // PTK
    assert t // MOBA_BLOCK <= 32
    qw = MOBA_HEADS * HEAD_DIM
    c_q, c_k, c_v = (C_QM, C_KM, C_VM) if moba else (C_QD, C_KD, C_VD)
    jj = lambda i, j: jnp.minimum(j, (i * PTQ + PTQ - 1) // PTK)
    if moba:
        aux_spec = pl.BlockSpec((LANES, KVW), lambda i, j: (0, 0))
    else:
        aux_spec = pl.BlockSpec((PTQ, PTK), lambda i, j: (i, jj(i, j)))
    rows = GROUP * PTQ
    return pl.pallas_call(
        functools.partial(_prompt_attn_kernel, moba),
        grid=(nq, nk),
        in_specs=[
            pl.BlockSpec((PTQ, qw), lambda i, j: (i, c_q // qw)),
            pl.BlockSpec((PTK, KVW), lambda i, j: (jj(i, j), c_k // KVW)),
            pl.BlockSpec((PTK, KVW), lambda i, j: (jj(i, j), c_v // KVW)),
            aux_spec,
        ],
        out_specs=pl.BlockSpec((PTQ, qw), lambda i, j: (i, 0)),
        out_shape=jax.ShapeDtypeStruct((t, qw), BF16),
        scratch_shapes=[
            pltpu.VMEM((KV_HEADS, rows, HEAD_DIM), BF16),
            pltpu.VMEM((KV_HEADS, rows, LANES), F32),
            pltpu.VMEM((KV_HEADS, rows, HEAD_DIM + LANES), F32),
            pltpu.VMEM((KV_HEADS, rows, LANES), jnp.int32),
        ],
        compiler_params=_cparams(("parallel", "arbitrary")),
        name="moba_prompt" if moba else "dsa_prompt",
    )(proj, proj, proj, aux)


DTK = 512


def _dsa_select_kernel(k_keep, qi_ref, ki_ref, wi_ref, bias_ref, qs_ref, key_ref):
    i = pl.program_id(0)
    j = pl.program_id(1)
    nj = pl.num_programs(1)
    length = key_ref.shape[1]

    @pl.when(j == 0)
    def _():
        qs_ref[...] = qi_ref[...].astype(BF16)
        key_ref[...] = jnp.full(key_ref.shape, INT_MIN, jnp.int32)

    @pl.when(j * DTK <= i * PTQ + PTQ - 1)
    def _():
        kb = ki_ref[...].astype(BF16)
        w = wi_ref[...]
        acc = jnp.zeros((PTQ, DTK), F32)
        for h in range(IDX_HEADS):
            s = _dot_nt(qs_ref[:, h * IDX_DIM:(h + 1) * IDX_DIM], kb)
            acc = acc + w[:, h:h + 1] * jnp.maximum(s, 0.0)
        row = i * PTQ + lax.broadcasted_iota(jnp.int32, (PTQ, DTK), 0)
        col = j * DTK + lax.broadcasted_iota(jnp.int32, (PTQ, DTK), 1)
        x = jnp.where(col <= row, acc, -jnp.inf)
        key_ref[:, pl.ds(pl.multiple_of(j * DTK, DTK), DTK)] = _sortable_key(x)

    @pl.when(j == nj - 1)
    def _():
        n_tiles = (i * PTQ + PTQ - 1) // DTK + 1

        def search(r0, nr):
            def count_ge(cand):
                def tile(c, acc):
                    base = pl.multiple_of(c * DTK, DTK)
                    for q in range(DTK // LANES):
                        acc = acc + jnp.where(key_ref[r0:r0 + nr, pl.ds(base + q * LANES, LANES)] >= cand, 1, 0)
                    return acc

                acc = lax.fori_loop(0, n_tiles, tile, jnp.zeros((nr, LANES), jnp.int32))
                return jnp.sum(acc, axis=-1, keepdims=True)

            return _kth_largest_key(count_ge, (nr, LANES), k_keep)[:, 0:1]

        thr = jnp.concatenate([search(0, PTQ // 2), search(PTQ // 2, PTQ // 2)], axis=0)
        row = i * PTQ + lax.broadcasted_iota(jnp.int32, (PTQ, length), 0)
        col = lax.broadcasted_iota(jnp.int32, (PTQ, length), 1)
        ok = jnp.logical_and(key_ref[...] >= thr, col <= row)
        bias_ref[...] = jnp.where(ok, 0.0, NEG)


def _dsa_select(proj, t):
    nq, nj = t // PTQ, t // DTK
    k_keep = max(1, min(DSA_TOPK_MAX, t // 4))
    qw = IDX_HEADS * IDX_DIM
    last_j = lambda i: (i * PTQ + PTQ - 1) // DTK
    return pl.pallas_call(
        functools.partial(_dsa_select_kernel, k_keep),
        grid=(nq, nj),
        in_specs=[
            pl.BlockSpec((PTQ, qw), lambda i, j: (i, C_QI // qw)),
            pl.BlockSpec((DTK, IDX_DIM), lambda i, j: (jnp.minimum(j, last_j(i)), C_KI // IDX_DIM)),
            pl.BlockSpec((PTQ, LANES), lambda i, j: (i, C_WI // LANES)),
        ],
        out_specs=pl.BlockSpec((PTQ, t), lambda i, j: (i, 0)),
        out_shape=jax.ShapeDtypeStruct((t, t), F32),
        scratch_shapes=[pltpu.VMEM((PTQ, qw), BF16), pltpu.VMEM((PTQ, t), jnp.int32)],
        compiler_params=_cparams(("parallel", "arbitrary")),
        name="dsa_select",
    )(proj, proj, proj)


N_PAGES = PAST_LEN // PAGE_SIZE
PAGES_PER_BLOCK = MOBA_BLOCK // PAGE_SIZE
PAGE_ROWS = PAGE_SIZE * KV_HEADS


def _head_match(rows, cols, dec):
    r = lax.broadcasted_iota(jnp.int32, (rows, cols), 0)
    c = lax.broadcasted_iota(jnp.int32, (rows, cols), 1)
    shift = (GROUP * dec).bit_length() - 1
    return (r >> shift) == (c & (KV_HEADS - 1))


def _stage_new(dst_ref, src_ref):
    dst_ref[...] = jnp.zeros(dst_ref.shape, F32)
    dst_ref[0:src_ref.shape[0], :] = src_ref[...]


def _softmax_over_pages(s_ref, s_new, v_refs, nv_ref, o_ref):
    m = jnp.max(s_new, axis=-1, keepdims=True)
    for p in range(N_PAGES):
        m = jnp.maximum(m, jnp.max(s_ref[p], axis=-1, keepdims=True))
    pn = jnp.exp(s_new - m)
    l = jnp.sum(pn, axis=-1, keepdims=True)
    acc = _dot(pn.astype(BF16), nv_ref[...].astype(BF16))
    for p in range(N_PAGES):
        pr = jnp.exp(s_ref[p] - m)
        l = l + jnp.sum(pr, axis=-1, keepdims=True)
        acc = acc + _dot(pr.astype(BF16), v_refs[p][...].astype(BF16))
    o_ref[...] = acc / l


def _sample_moba_kernel(pt_ref, q_ref, kn_ref, vn_ref, *rest):
    k_refs, v_refs = rest[:N_PAGES], rest[N_PAGES:2 * N_PAGES]
    o_ref, s_ref, nk_ref, nv_ref = rest[2 * N_PAGES:]
    rows = q_ref.shape[0]
    dec = rows // MOBA_HEADS
    n_past_blocks = PAST_LEN // MOBA_BLOCK
    q = q_ref[...].astype(BF16)
    match = _head_match(rows, PAGE_ROWS, dec)
    lane = lax.broadcasted_iota(jnp.int32, (rows, LANES), 1)

    for p in range(N_PAGES):
        s_ref[p] = _dot_nt(q, k_refs[p][...].astype(BF16))

    gate = jnp.full((rows, LANES), -jnp.inf, F32)
    for b in range(n_past_blocks):
        tot = s_ref[PAGES_PER_BLOCK * b]
        for c in range(1, PAGES_PER_BLOCK):
            tot = tot + s_ref[PAGES_PER_BLOCK * b + c]
        g = jnp.sum(jnp.where(match, tot, 0.0), axis=-1, keepdims=True) * (1.0 / MOBA_BLOCK)
        gate = jnp.where(lane == b, g, gate)
    ids = _top_block_ids(gate, lane, MOBA_TOPK)

    for b in range(n_past_blocks):
        picked = ids[0] == b
        for idx in ids[1:]:
            picked = jnp.logical_or(picked, idx == b)
        row_bias = jnp.where(picked, 0.0, NEG)
        for c in range(PAGES_PER_BLOCK):
            p = PAGES_PER_BLOCK * b + c
            s_ref[p] = jnp.where(match, s_ref[p] * ATTN_SCALE + row_bias, NEG)

    _stage_new(nk_ref, kn_ref)
    _stage_new(nv_ref, vn_ref)
    rn = lax.broadcasted_iota(jnp.int32, (rows, LANES), 0)
    ok = jnp.logical_and(_head_match(rows, LANES, dec), (lane >> 2) <= (rn & (dec - 1)))
    s_new = jnp.where(ok, _dot_nt(q, nk_ref[...].astype(BF16)) * ATTN_SCALE, NEG)
    _softmax_over_pages(s_ref, s_new, v_refs, nv_ref, o_ref)


def _page_specs(shape, n):
    return [pl.BlockSpec((None,) + shape, functools.partial(lambda b, pt, p: (pt[b, p], 0, 0), p=p)) for p in range(n)]


def _sample_moba(pt, q, kn, vn, ck, cv):
    nb, rows, _ = q.shape
    dec = rows // MOBA_HEADS
    assert dec & (dec - 1) == 0 and dec * KV_HEADS <= LANES and KV_HEADS == 4
    seq = lambda b, pt: (b, 0, 0)
    return pl.pallas_call(
        _sample_moba_kernel,
        grid_spec=pltpu.PrefetchScalarGridSpec(
            num_scalar_prefetch=1,
            grid=(nb,),
            in_specs=[pl.BlockSpec((None, rows, HEAD_DIM), seq),
                      pl.BlockSpec((None, dec * KV_HEADS, HEAD_DIM), seq),
                      pl.BlockSpec((None, dec * KV_HEADS, HEAD_DIM), seq)]
                     + _page_specs((PAGE_ROWS, HEAD_DIM), N_PAGES) + _page_specs((PAGE_ROWS, HEAD_DIM), N_PAGES),
            out_specs=pl.BlockSpec((None, rows, HEAD_DIM), seq),
            scratch_shapes=[
                pltpu.VMEM((N_PAGES, rows, PAGE_ROWS), F32),
                pltpu.VMEM((LANES, HEAD_DIM), F32),
                pltpu.VMEM((LANES, HEAD_DIM), F32),
            ],
        ),
        out_shape=jax.ShapeDtypeStruct((nb, rows, HEAD_DIM), F32),
        compiler_params=_cparams(("parallel",)),
        name="moba_sample",
    )(pt, q, kn, vn, *([ck] * N_PAGES), *([cv] * N_PAGES))


def _sample_dsa_kernel(k_keep, pt_ref, qi_ref, wi_ref, q_ref, kin_ref, kn_ref, vn_ref, *rest):
    i_refs, k_refs, v_refs = rest[:N_PAGES], rest[N_PAGES:2 * N_PAGES], rest[2 * N_PAGES:3 * N_PAGES]
    o_ref, idx_ref, s_ref, nki_ref, nk_ref, nv_ref = rest[3 * N_PAGES:]
    rows = q_ref.shape[0]
    dec = rows // DSA_HEADS
    qi = qi_ref[...].astype(BF16)
    w = wi_ref[...]

    def index_scores(kpage):
        sc = jnp.maximum(_dot_nt(qi, kpage.astype(BF16)), 0.0) * w
        return jnp.sum(sc.reshape(dec, IDX_HEADS, PAGE_SIZE), axis=1)

    idx_ref[...] = jnp.full(idx_ref.shape, -jnp.inf, F32)
    for p in range(N_PAGES):
        idx_ref[p, 0:dec, :] = index_scores(i_refs[p][...])
    _stage_new(nki_ref, kin_ref)
    t_in = lax.broadcasted_iota(jnp.int32, (dec, PAGE_SIZE), 0)
    col = lax.broadcasted_iota(jnp.int32, (dec, PAGE_SIZE), 1)
    idx_ref[N_PAGES, 0:dec, :] = jnp.where(col <= t_in, index_scores(nki_ref[...]), -jnp.inf)

    keys = _sortable_key(idx_ref[...])

    def count_ge(cand):
        c = jnp.sum((keys >= cand).astype(jnp.int32), axis=0)
        return jnp.sum(c, axis=-1, keepdims=True)[None]

    thr = _kth_largest_key(count_ge, (1, SUBLANES, 1), k_keep)
    keep = jnp.where(jnp.logical_and(keys >= thr, idx_ref[...] > -jnp.inf), 1.0, 0.0)

    q = q_ref[...].astype(BF16)
    match = _head_match(rows, PAGE_ROWS, dec)
    er = lax.broadcasted_iota(jnp.int32, (PAGE_SIZE, PAGE_ROWS), 0)
    ec = lax.broadcasted_iota(jnp.int32, (PAGE_SIZE, PAGE_ROWS), 1)
    expand = jnp.where((ec >> 2) == er, 1.0, 0.0).astype(BF16)

    def visible(page, width):
        per_row = jnp.concatenate([keep[page, 0:dec, :]] * (rows // dec), axis=0).astype(BF16)
        return _dot(per_row, expand[:, 0:width]) > 0.5

    for p in range(N_PAGES):
        sc = _dot_nt(q, k_refs[p][...].astype(BF16)) * ATTN_SCALE
        s_ref[p] = jnp.where(jnp.logical_and(match, visible(p, PAGE_ROWS)), sc, NEG)
    _stage_new(nk_ref, kn_ref)
    _stage_new(nv_ref, vn_ref)
    ok = jnp.logical_and(_head_match(rows, LANES, dec), visible(N_PAGES, LANES))
    s_new = jnp.where(ok, _dot_nt(q, nk_ref[...].astype(BF16)) * ATTN_SCALE, NEG)
    _softmax_over_pages(s_ref, s_new, v_refs, nv_ref, o_ref)


def _sample_dsa(pt, qi, wi, q, kin, kn, vn, ci, ck, cv):
    nb, rows, _ = q.shape
    dec = rows // DSA_HEADS
    assert dec & (dec - 1) == 0 and dec <= SUBLANES and KV_HEADS == 4
    k_keep = max(1, min(DSA_TOPK_MAX, (PAST_LEN + dec) // 4))
    seq = lambda b, pt: (b, 0, 0)
    return pl.pallas_call(
        functools.partial(_sample_dsa_kernel, k_keep),
        grid_spec=pltpu.PrefetchScalarGridSpec(
            num_scalar_prefetch=1,
            grid=(nb,),
            in_specs=[pl.BlockSpec((None, dec * IDX_HEADS, IDX_DIM), seq),
                      pl.BlockSpec((None, dec * IDX_HEADS, 1), seq),
                      pl.BlockSpec((None, rows, HEAD_DIM), seq),
                      pl.BlockSpec((None, dec, IDX_DIM), seq),
                      pl.BlockSpec((None, dec * KV_HEADS, HEAD_DIM), seq),
                      pl.BlockSpec((None, dec * KV_HEADS, HEAD_DIM), seq)]
                     + _page_specs((PAGE_SIZE, IDX_DIM), N_PAGES)
                     + _page_specs((PAGE_ROWS, HEAD_DIM), N_PAGES) + _page_specs((PAGE_ROWS, HEAD_DIM), N_PAGES),
            out_specs=pl.BlockSpec((None, rows, HEAD_DIM), seq),
            scratch_shapes=[
                pltpu.VMEM((N_PAGES + 1, SUBLANES, PAGE_SIZE), F32),
                pltpu.VMEM((N_PAGES, rows, PAGE_ROWS), F32),
                pltpu.VMEM((PAGE_SIZE, IDX_DIM), F32),
                pltpu.VMEM((LANES, HEAD_DIM), F32),
                pltpu.VMEM((LANES, HEAD_DIM), F32),
            ],
        ),
        out_shape=jax.ShapeDtypeStruct((nb, rows, HEAD_DIM), F32),
        compiler_params=_cparams(("parallel",)),
        name="dsa_sample",
    )(pt, qi, wi, q, kin, kn, vn, *([ci] * N_PAGES), *([ck] * N_PAGES), *([cv] * N_PAGES))


def _outproj_kernel(mix_ref, w_ref, x_ref, g_ref, h_ref, h2_ref, acc_ref):
    k = pl.program_id(1)

    @pl.when(k == 0)
    def _():
        acc_ref[...] = jnp.zeros(acc_ref.shape, F32)

    acc_ref[...] += _dot(mix_ref[...], w_ref[...])

    @pl.when(k == pl.num_programs(1) - 1)
    def _():
        h = x_ref[...] + acc_ref[...]
        h_ref[...] = h
        y = h * lax.rsqrt(jnp.mean(h * h, axis=-1, keepdims=True) + NORM_EPS)
        h2_ref[...] = (y * g_ref[...]).astype(h2_ref.dtype)


def _outproj(mix, w, x, g, tm, tk):
    n, d = x.shape
    kdim = mix.shape[1]
    return pl.pallas_call(
        _outproj_kernel,
        grid=(n // tm, kdim // tk),
        in_specs=[
            pl.BlockSpec((tm, tk), lambda i, k: (i, k)),
            pl.BlockSpec((tk, d), lambda i, k: (k, 0)),
            pl.BlockSpec((tm, d), lambda i, k: (i, 0)),
            pl.BlockSpec((1, d), lambda i, k: (0, 0)),
        ],
        out_specs=[pl.BlockSpec((tm, d), lambda i, k: (i, 0)), pl.BlockSpec((tm, d), lambda i, k: (i, 0))],
        out_shape=[jax.ShapeDtypeStruct((n, d), F32), jax.ShapeDtypeStruct((n, d), BF16)],
        scratch_shapes=[pltpu.VMEM((tm, d), F32)],
        compiler_params=_cparams(("parallel", "arbitrary")),
        name="outproj_norm",
    )(mix, w, x, g.reshape(1, d))


def _top_rows(x, k):
    r = x.shape[0]
    ridx = lax.broadcasted_iota(jnp.int32, x.shape, 0)
    outs = []
    for _ in range(k):
        mx = jnp.max(x, axis=0, keepdims=True)
        idx = jnp.min(jnp.where(x == mx, ridx, r), axis=0, keepdims=True)
        outs.append(mx)
        x = jnp.where(ridx == idx, -1.0, x)
    return jnp.concatenate(outs, axis=0)


def _peer_select_kernel(h2_ref, wq_ref, sk_ref, e_ref, thr_ref, iz_ref, acc_ref):
    k = pl.program_id(1)

    @pl.when(k == 0)
    def _():
        acc_ref[...] = jnp.zeros(acc_ref.shape, F32)

    acc_ref[...] += _dot(h2_ref[...], wq_ref[...])

    @pl.when(k == pl.num_programs(1) - 1)
    def _():
        half = PEER_DK // 2
        hk = PEER_TOPK // 2
        for h in range(PEER_HEADS):
            tops = []
            for p in range(2):
                c0 = (h * 2 + p) * half
                q = acc_ref[:, c0:c0 + half].astype(BF16)
                st = _dot_nt(sk_ref[h, p], q)
                e = jnp.exp(st - jnp.max(st, axis=0, keepdims=True))
                e_ref[h, p] = e
                tops.append(_top_rows(e, PEER_TOPK))
            a, b = tops
            cand = [a[x:x + 1, :] * b[0:hk, :] for x in range(hk)]
            cand += [a[0:1, :] * b[hk:, :], a[hk:, :] * b[0:1, :]]
            top = _top_rows(jnp.concatenate(cand, axis=0), PEER_TOPK)
            thr_ref[h:h + 1, :] = top[PEER_TOPK - 1:PEER_TOPK, :]
            iz_ref[h:h + 1, :] = 1.0 / jnp.sum(top, axis=0, keepdims=True)


def _peer_select(h2, wq, sk, tm, tk):
    n, d = h2.shape
    qw = wq.shape[1]
    return pl.pallas_call(
        _peer_select_kernel,
        grid=(n // tm, d // tk),
        in_specs=[
            pl.BlockSpec((tm, tk), lambda i, k: (i, k)),
            pl.BlockSpec((tk, qw), lambda i, k: (k, 0)),
            pl.BlockSpec(sk.shape, lambda i, k: (0, 0, 0, 0)),
        ],
        out_specs=[
            pl.BlockSpec((PEER_HEADS, 2, PEER_N_KEYS, tm), lambda i, k: (0, 0, 0, i)),
            pl.BlockSpec((PEER_HEADS, tm), lambda i, k: (0, i)),
            pl.BlockSpec((PEER_HEADS, tm), lambda i, k: (0, i)),
        ],
        out_shape=[
            jax.ShapeDtypeStruct((PEER_HEADS, 2, PEER_N_KEYS, n), F32),
            jax.ShapeDtypeStruct((PEER_HEADS, n), F32),
            jax.ShapeDtypeStruct((PEER_HEADS, n), F32),
        ],
        scratch_shapes=[pltpu.VMEM((tm, qw), F32)],
        compiler_params=_cparams(("parallel", "arbitrary")),
        name="peer_select",
    )(h2, wq, sk)


def _peer_dense_kernel(h2_ref, u_ref, v_ref, e_ref, thr_ref, iz_ref, o_ref, ga_ref):
    e = pl.program_id(1)
    n_tiles = pl.num_programs(1) - 1
    te = u_ref.shape[0]
    tm = h2_ref.shape[0]

    @pl.when(e == 0)
    def _():
        o_ref[...] = jnp.zeros(o_ref.shape, F32)
        ga_ref[...] = jnp.zeros(ga_ref.shape, BF16)

    prev = lax.dot_general(ga_ref[...], v_ref[...], (((0,), (0,)), ((), ())), preferred_element_type=F32)

    tile = jnp.minimum(e, n_tiles - 1)
    at = _dot_nt(u_ref[...], h2_ref[...])
    act = 0.5 * at * (1.0 + lax.erf(at * (2.0 ** -0.5)))
    parts = []
    for c in range(te // PEER_N_KEYS):
        i = tile * (te // PEER_N_KEYS) + c
        g = jnp.zeros((PEER_N_KEYS, tm), F32)
        for h in range(PEER_HEADS):
            p = e_ref[h, 0, pl.ds(i, 1), :] * e_ref[h, 1]
            g = g + jnp.where(p >= thr_ref[h:h + 1, :], p, 0.0) * iz_ref[h:h + 1, :]
        parts.append(g)
    ga = (jnp.concatenate(parts, axis=0) * act).astype(BF16)

    o_ref[...] += prev
    ga_ref[...] = ga


def _peer_dense(h2, u, v, e, thr, iz, tm, te):
    n, d = h2.shape
    n_tiles = u.shape[0] // te
    return pl.pallas_call(
        _peer_dense_kernel,
        grid=(n // tm, n_tiles + 1),
        in_specs=[
            pl.BlockSpec((tm, d), lambda i, k: (i, 0)),
            pl.BlockSpec((te, d), lambda i, k: (jnp.minimum(k, n_tiles - 1), 0)),
            pl.BlockSpec((te, d), lambda i, k: (jnp.maximum(k - 1, 0), 0)),
            pl.BlockSpec((PEER_HEADS, 2, PEER_N_KEYS, tm), lambda i, k: (0, 0, 0, i)),
            pl.BlockSpec((PEER_HEADS, tm), lambda i, k: (0, i)),
            pl.BlockSpec((PEER_HEADS, tm), lambda i, k: (0, i)),
        ],
        out_specs=pl.BlockSpec((tm, d), lambda i, k: (i, 0)),
        out_shape=jax.ShapeDtypeStruct((n, d), F32),
        scratch_shapes=[pltpu.VMEM((te, tm), BF16)],
        compiler_params=_cparams(("parallel", "arbitrary")),
        name="peer_dense",
    )(h2, u, v, e, thr, iz)


def _final_kernel(h_ref, f_ref, g_ref, o_ref):
    x = h_ref[...] + f_ref[...]
    y = x * lax.rsqrt(jnp.mean(x * x, axis=-1, keepdims=True) + NORM_EPS)
    o_ref[...] = y * g_ref[...]


def _final(h, f, g, tm):
    n, d = h.shape
    return pl.pallas_call(
        _final_kernel,
        grid=(n // tm,),
        in_specs=[pl.BlockSpec((tm, d), lambda i: (i, 0)), pl.BlockSpec((tm, d), lambda i: (i, 0)),
                  pl.BlockSpec((1, d), lambda i: (0, 0))],
        out_specs=pl.BlockSpec((tm, d), lambda i: (i, 0)),
        out_shape=jax.ShapeDtypeStruct((n, d), F32),
        compiler_params=_cparams(("parallel",)),
        name="final_norm",
    )(h, f, g.reshape(1, d))


def _permute_w_in(w_in):
    sizes = (MOBA_HEADS * HEAD_DIM, KVW, KVW, DSA_HEADS * HEAD_DIM, KVW, KVW, IDX_HEADS * IDX_DIM, IDX_DIM, IDX_HEADS)
    q_m, k_m, v_m, q_d, k_d, v_d, q_i, k_i, w_i = jnp.split(w_in, tuple(int(c) for c in np.cumsum(sizes)[:-1]), axis=-1)
    w = jnp.concatenate([q_i, q_m, q_d, k_m, v_m, k_d, v_d, k_i, w_i], axis=-1)
    return jnp.pad(w, ((0, 0), (0, PROJ_PAD - w.shape[1]))).astype(BF16)


def _heads_to_rows(q, nb, dec):
    return q.reshape(nb, dec, MOBA_HEADS, HEAD_DIM).transpose(0, 2, 1, 3).reshape(nb, MOBA_HEADS * dec, HEAD_DIM)


def _rows_to_heads(o, nb, dec):
    return o.reshape(nb, MOBA_HEADS, dec, HEAD_DIM).transpose(0, 2, 1, 3).reshape(nb * dec, MOBA_HEADS * HEAD_DIM)


def _sample_mix(proj_s, page_table, c_mk, c_mv, c_dk, c_dv, c_ik, nb, dec):
    ps = proj_s.reshape(nb, dec, PROJ_PAD)
    seg = lambda c, w: ps[:, :, c:c + w]
    kv_rows = lambda c: seg(c, KVW).reshape(nb, dec * KV_HEADS, HEAD_DIM)
    s_m = _sample_moba(page_table, _heads_to_rows(seg(C_QM, MOBA_HEADS * HEAD_DIM), nb, dec),
                       kv_rows(C_KM), kv_rows(C_VM), c_mk, c_mv)
    qi = seg(C_QI, IDX_HEADS * IDX_DIM).reshape(nb, dec * IDX_HEADS, IDX_DIM)
    wi = seg(C_WI, IDX_HEADS).reshape(nb, dec * IDX_HEADS, 1)
    s_d = _sample_dsa(page_table, qi, wi, _heads_to_rows(seg(C_QD, DSA_HEADS * HEAD_DIM), nb, dec),
                      seg(C_KI, IDX_DIM), kv_rows(C_KD), kv_rows(C_VD), c_ik, c_dk, c_dv)
    return jnp.concatenate([_rows_to_heads(s_m, nb, dec), _rows_to_heads(s_d, nb, dec)], axis=-1)


def _prompt_mix(proj, t):
    kmean = _kmean(proj, t)
    kmean = jnp.pad(kmean, ((0, LANES - kmean.shape[0]), (0, 0)))
    o_m = _prompt_attn(True, proj, kmean, t)
    o_d = _prompt_attn(False, proj, _dsa_select(proj, t), t)
    return jnp.concatenate([o_m, o_d], axis=-1)


def kernel(x_prompt, x_sample, cache_moba_k, cache_moba_v, cache_dsa_k, cache_dsa_v, cache_idx_k, page_table,
           norm_mix, w_in, w_out, norm_ffn, peer_w_q, peer_subkeys, peer_u, peer_v, norm_final):
    depth = w_in.shape[0]
    assert depth == 1 and x_prompt.shape[0] == 1
    assert PAST_LEN % MOBA_BLOCK == 0 and page_table.shape[1] == N_PAGES
    t = x_prompt.shape[1]
    nb, dec, d = x_sample.shape
    n_tok = t + nb * dec
    tm = 512
    assert n_tok % tm == 0 and t % tm == 0

    x = jnp.concatenate([x_prompt.reshape(t, d), x_sample.reshape(nb * dec, d)], axis=0)
    pos = jnp.concatenate([jnp.arange(t, dtype=jnp.int32),
                           PAST_LEN + jnp.tile(jnp.arange(dec, dtype=jnp.int32), nb)])
    cos, sin = _rope_tables(pos)

    n_pool = cache_moba_k.shape[1]
    flat = lambda c: c[0].reshape(n_pool, PAGE_ROWS, HEAD_DIM)

    u = _rmsnorm(x, norm_mix[0], BF16, tm)
    tm_proj = n_tok // 8 if (n_tok // 8) % 16 == 0 and n_tok % 8 == 0 else tm
    proj = _project(u, _permute_w_in(w_in[0]), cos, sin, tm_proj)
    mix_p = _prompt_mix(proj, t)
    mix_s = _sample_mix(proj[t:], page_table, flat(cache_moba_k), flat(cache_moba_v), flat(cache_dsa_k),
                        flat(cache_dsa_v), cache_idx_k[0], nb, dec)
    mix = jnp.concatenate([mix_p, mix_s.astype(BF16)], axis=0)
    h, h2 = _outproj(mix, w_out[0].astype(BF16), x, norm_ffn[0], 256, 1024)
    e, thr, iz = _peer_select(h2, peer_w_q[0].astype(BF16), peer_subkeys[0].astype(BF16), tm, 512)
    f = _peer_dense(h2, peer_u[0].astype(BF16), peer_v[0].astype(BF16), e, thr, iz, tm, 512)
    y = _final(h, f, norm_final, 256)

    y_prompt = y[:t].reshape(1, t, d)
    y_sample = y[t:].reshape(nb, dec, d)
    pp = proj[:t]
    ps = proj[t:]
    kv = lambda a, c: a[:, c:c + KVW]
    outs_p = [kv(pp, c).reshape(1, 1, t, KV_HEADS, HEAD_DIM) for c in (C_KM, C_VM, C_KD, C_VD)]
    outs_p.append(pp[:, C_KI:C_KI + IDX_DIM].reshape(1, 1, t, IDX_DIM))
    outs_s = [kv(ps, c).reshape(1, nb, dec, KV_HEADS, HEAD_DIM) for c in (C_KM, C_VM, C_KD, C_VD)]
    outs_s.append(ps[:, C_KI:C_KI + IDX_DIM].reshape(1, nb, dec, IDX_DIM))
    return (y_prompt, y_sample, *outs_p, *outs_s)
```

```python
import functools

import numpy as np
import jax
import jax.numpy as jnp
from jax import lax
from jax.experimental import pallas as pl
from jax.experimental.pallas import tpu as pltpu

F32 = jnp.float32
BF16 = jnp.bfloat16

D_MODEL = 4096
HEAD_DIM = 128
MOBA_HEADS = 16
DSA_HEADS = 16
KV_HEADS = 4
GROUP = 4
MOBA_BLOCK = 256
MOBA_TOPK = 3
IDX_HEADS = 32
IDX_DIM = 128
DSA_TOPK_MAX = 256
ROPE_THETA = 10000.0
ATTN_SCALE = HEAD_DIM ** -0.5
IDX_W_SCALE = (IDX_HEADS ** -0.5) * (IDX_DIM ** -0.5)
PEER_HEADS = 8
PEER_N_KEYS = 128
PEER_DK = 256
PEER_TOPK = 16
NORM_EPS = 1e-6
PAST_LEN = 2048
PAGE_SIZE = 128

LANES = 128
SUBLANES = 8
NEG = -1e30
INT_MIN = -2 ** 31
VMEM_LIMIT = 56 * 1024 * 1024

C_QI = 0
C_QM = C_QI + IDX_HEADS * IDX_DIM
C_QD = C_QM + MOBA_HEADS * HEAD_DIM
C_KM = C_QD + DSA_HEADS * HEAD_DIM
C_VM = C_KM + KV_HEADS * HEAD_DIM
C_KD = C_VM + KV_HEADS * HEAD_DIM
C_VD = C_KD + KV_HEADS * HEAD_DIM
C_KI = C_VD + KV_HEADS * HEAD_DIM
C_WI = C_KI + IDX_DIM
PROJ_TN = 512
PROJ_PAD = 10752
KVW = KV_HEADS * HEAD_DIM


def _cparams(sem):
    return pltpu.CompilerParams(dimension_semantics=sem, vmem_limit_bytes=VMEM_LIMIT)


def _dot_nt(a, b):
    return lax.dot_general(a, b, (((1,), (1,)), ((), ())), preferred_element_type=F32)


def _dot(a, b):
    return jnp.dot(a, b, preferred_element_type=F32)


def _two_source_specs(tm, d, tiles_a):
    return [pl.BlockSpec((tm, d), lambda i, *_: (jnp.minimum(i, tiles_a - 1), 0)),
            pl.BlockSpec((tm, d), lambda i, *_: (jnp.maximum(i - tiles_a, 0), 0))]


def _rmsnorm_kernel(tiles_a, xa_ref, xb_ref, g_ref, o_ref):
    x = jnp.where(pl.program_id(0) < tiles_a, xa_ref[...], xb_ref[...])
    y = x * lax.rsqrt(jnp.mean(x * x, axis=-1, keepdims=True) + NORM_EPS)
    o_ref[...] = (y * g_ref[...]).astype(o_ref.dtype)


def _rmsnorm(xa, xb, g, out_dtype, tm):
    (na, d), nb = xa.shape, xb.shape[0]
    assert na % tm == 0 and nb % tm == 0
    return pl.pallas_call(
        functools.partial(_rmsnorm_kernel, na // tm),
        grid=((na + nb) // tm,),
        in_specs=_two_source_specs(tm, d, na // tm) + [pl.BlockSpec((1, d), lambda i: (0, 0))],
        out_specs=pl.BlockSpec((tm, d), lambda i: (i, 0)),
        out_shape=jax.ShapeDtypeStruct((na + nb, d), out_dtype),
        compiler_params=_cparams(("parallel",)),
        name="rmsnorm",
    )(xa, xb, g.reshape(1, d))


def _rope_group(a, cos, sin):
    return a * cos + pltpu.roll(a, HEAD_DIM // 2, axis=1) * sin


def _proj_kernel(u_ref, w_ref, cos_ref, sin_ref, o_ref):
    j = pl.program_id(1)
    acc = _dot(u_ref[...], w_ref[...])
    plain = jnp.logical_or(j == C_VM // PROJ_TN, j == C_VD // PROJ_TN)
    last = j == C_KI // PROJ_TN
    groups = PROJ_TN // HEAD_DIM

    @pl.when(plain)
    def _():
        o_ref[...] = acc

    @pl.when(jnp.logical_not(jnp.logical_or(plain, last)))
    def _():
        cos = cos_ref[...]
        sin = sin_ref[...]
        for c in range(groups):
            o_ref[:, c * HEAD_DIM:(c + 1) * HEAD_DIM] = _rope_group(acc[:, c * HEAD_DIM:(c + 1) * HEAD_DIM], cos, sin)

    @pl.when(last)
    def _():
        o_ref[:, 0:HEAD_DIM] = _rope_group(acc[:, 0:HEAD_DIM], cos_ref[...], sin_ref[...])
        o_ref[:, HEAD_DIM:2 * HEAD_DIM] = acc[:, HEAD_DIM:2 * HEAD_DIM] * IDX_W_SCALE
        o_ref[:, 2 * HEAD_DIM:] = acc[:, 2 * HEAD_DIM:]


def _w_in_tile_order():
    model = dict(q_m=0, k_m=MOBA_HEADS * HEAD_DIM)
    model["v_m"] = model["k_m"] + KVW
    model["q_d"] = model["v_m"] + KVW
    model["k_d"] = model["q_d"] + DSA_HEADS * HEAD_DIM
    model["v_d"] = model["k_d"] + KVW
    model["q_i"] = model["v_d"] + KVW
    model["k_i"] = model["q_i"] + IDX_HEADS * IDX_DIM
    ours = (("q_i", C_QI, C_QM), ("q_m", C_QM, C_QD), ("q_d", C_QD, C_KM), ("k_m", C_KM, C_VM), ("v_m", C_VM, C_KD),
            ("k_d", C_KD, C_VD), ("v_d", C_VD, C_KI), ("k_i", C_KI, PROJ_PAD))
    order = []
    for name, lo, hi in ours:
        assert lo % PROJ_TN == 0 and model[name] % PROJ_TN == 0
        order += [model[name] // PROJ_TN + x for x in range((hi - lo) // PROJ_TN)]
    return np.asarray(order, np.int32)


def _proj_kernel_mapped(src_ref, u_ref, w_ref, cos_ref, sin_ref, o_ref):
    _proj_kernel(u_ref, w_ref, cos_ref, sin_ref, o_ref)


def _project(u, w, cos, sin, tm):
    n, d = u.shape
    return pl.pallas_call(
        _proj_kernel_mapped,
        grid_spec=pltpu.PrefetchScalarGridSpec(
            num_scalar_prefetch=1,
            grid=(n // tm, PROJ_PAD // PROJ_TN),
            in_specs=[
                pl.BlockSpec((tm, d), lambda i, j, src: (i, 0)),
                pl.BlockSpec((d, PROJ_TN), lambda i, j, src: (0, src[j])),
                pl.BlockSpec((tm, HEAD_DIM), lambda i, j, src: (i, 0)),
                pl.BlockSpec((tm, HEAD_DIM), lambda i, j, src: (i, 0)),
            ],
            out_specs=pl.BlockSpec((tm, PROJ_TN), lambda i, j, src: (i, j)),
        ),
        out_shape=jax.ShapeDtypeStruct((n, PROJ_PAD), F32),
        compiler_params=_cparams(("parallel", "arbitrary")),
        name="proj_rope",
    )(jnp.asarray(_w_in_tile_order()), u, w, cos, sin)


def _rope_tables(pos):
    half = HEAD_DIM // 2
    inv = 1.0 / (ROPE_THETA ** (jnp.arange(half, dtype=F32) * (2.0 / HEAD_DIM)))
    ang = pos.astype(F32)[:, None] * inv[None, :]
    c, s = jnp.cos(ang), jnp.sin(ang)
    return jnp.concatenate([c, c], axis=-1), jnp.concatenate([-s, s], axis=-1)


def _kmean_kernel(k_ref, o_ref):
    nb = o_ref.shape[0]
    k = k_ref[...].reshape(nb, MOBA_BLOCK, KVW)
    o_ref[...] = jnp.sum(k, axis=1) * (1.0 / MOBA_BLOCK)


def _kmean(proj, t):
    nb = t // MOBA_BLOCK
    return pl.pallas_call(
        _kmean_kernel,
        grid=(1,),
        in_specs=[pl.BlockSpec((t, KVW), lambda i: (0, C_KM // KVW))],
        out_specs=pl.BlockSpec((nb, KVW), lambda i: (0, 0)),
        out_shape=jax.ShapeDtypeStruct((nb, KVW), F32),
        compiler_params=_cparams(("arbitrary",)),
        name="moba_kmean",
    )(proj)


def _top_block_ids(gate, lane, n_sel):
    ids = []
    for _ in range(n_sel):
        mx = jnp.max(gate, axis=-1, keepdims=True)
        idx = jnp.min(jnp.where(gate == mx, lane, LANES), axis=-1, keepdims=True)
        ids.append(jnp.where(mx > -jnp.inf, idx, -1))
        gate = jnp.where(lane == idx, -jnp.inf, gate)
    return ids


def _sortable_key(x):
    bits = pltpu.bitcast(x + 0.0, jnp.int32)
    return bits ^ ((bits >> 31) & 0x7FFFFFFF)


def _kth_largest_key(count_ge, shape, k):
    t = jnp.broadcast_to(jnp.where(count_ge(jnp.zeros(shape, jnp.int32)) >= k, 0, INT_MIN).astype(jnp.int32), shape)

    def body(it, t):
        cand = t + (jnp.int32(1) << (30 - it))
        return jnp.where(count_ge(cand) >= k, cand, t)

    return lax.fori_loop(0, 31, body, t)


PTQ = 256
PTK = 512


def _prompt_attn_kernel(moba, q_ref, k_ref, v_ref, aux_ref, o_ref, qs_ref, m_ref, acc_ref, sel_ref):
    i = pl.program_id(0)
    j = pl.program_id(1)
    rows = GROUP * PTQ
    last_j = (i * PTQ + PTQ - 1) // PTK
    chunks = PTK // LANES

    @pl.when(j == 0)
    def _():
        for n in range(KV_HEADS):
            for g in range(GROUP):
                h = n * GROUP + g
                qs_ref[n, g * PTQ:(g + 1) * PTQ, :] = q_ref[:, h * HEAD_DIM:(h + 1) * HEAD_DIM].astype(BF16)
        m_ref[...] = jnp.full(m_ref.shape, NEG, F32)
        acc_ref[...] = jnp.zeros(acc_ref.shape, F32)
        if moba:
            lane = lax.broadcasted_iota(jnp.int32, (rows, LANES), 1)
            for n in range(KV_HEADS):
                km =aux_ref[:, n * HEAD_DIM:(n + 1) * HEAD_DIM].astype(BF16)
                gate = jnp.where(lane < i, _dot_nt(qs_ref[n], km), -jnp.inf)
                bits = jnp.zeros((rows, 1), jnp.int32)
                for idx in _top_block_ids(gate, lane, MOBA_TOPK):
                    bits = bits | jnp.where(idx >= 0, jnp.int32(1) << jnp.maximum(idx, 0), 0)
                sel_ref[n] = jnp.broadcast_to(bits, (rows, LANES))

    @pl.when(j <= last_j)
    def _():
        k = k_ref[...].astype(BF16)
        v = v_ref[...].astype(BF16)
        ones = jnp.ones((PTK, HEAD_DIM), BF16)
        if moba:
            lane = lax.broadcasted_iota(jnp.int32, (rows, LANES), 1)
            t_in = lax.broadcasted_iota(jnp.int32, (rows, LANES), 0) & (PTQ - 1)
        else:
            bias = jnp.concatenate([aux_ref[...].T] * GROUP, axis=0)
        for n in range(KV_HEADS):
            s = _dot_nt(qs_ref[n], k[:, n * HEAD_DIM:(n + 1) * HEAD_DIM]) * ATTN_SCALE
            if moba:
                bits = sel_ref[n]
                limits = []
                for b in range(PTK // MOBA_BLOCK):
                    kb = j * (PTK // MOBA_BLOCK) + b
                    picked = (bits >> jnp.minimum(kb, 31)) & 1
                    limits.append(jnp.where(kb == i, t_in, jnp.where(kb < i, picked * MOBA_BLOCK, 0) - 1))
                parts = []
                for c in range(chunks):
                    col = lane + (c * LANES) % MOBA_BLOCK
                    parts.append(jnp.where(col <= limits[(c * LANES) // MOBA_BLOCK],
                                           s[:, c * LANES:(c + 1) * LANES], NEG))
            else:
                s = s + bias
                parts = [s[:, c * LANES:(c + 1) * LANES] for c in range(chunks)]
            mx = parts[0]
            for c in range(1, chunks):
                mx = jnp.maximum(mx, parts[c])
            m_prev = m_ref[n]
            m_new = jnp.maximum(m_prev, jnp.max(mx, axis=-1, keepdims=True))
            alpha = jnp.exp(m_prev - m_new)
            p = jnp.concatenate([jnp.exp(x - m_new) for x in parts], axis=1).astype(BF16)
            v_ones = jnp.concatenate([v[:, n * HEAD_DIM:(n + 1) * HEAD_DIM], ones], axis=1)
            acc_ref[n] = jnp.concatenate([alpha, alpha], axis=1) * acc_ref[n] + _dot(p, v_ones)
            m_ref[n] = m_new

    @pl.when(j == last_j)
    def _():
        for n in range(KV_HEADS):
            for g in range(GROUP):
                h = n * GROUP + g
                r = slice(g * PTQ, (g + 1) * PTQ)
                o_ref[:, h * HEAD_DIM:(h + 1) * HEAD_DIM] = (acc_ref[n, r, 0:HEAD_DIM]
                                                             / acc_ref[n, r, HEAD_DIM:]).astype(o_ref.dtype)


def _prompt_attn(moba, proj, aux, t):
    nq, nk = t // PTQ, t // PTK
    assert t // MOBA_BLOCK <= 32
    qw = MOBA_HEADS * HEAD_DIM
    c_q, c_k, c_v = (C_QM, C_KM, C_VM) if moba else (C_QD, C_KD, C_VD)
    jj = lambda i, j: jnp.minimum(j, (i * PTQ + PTQ - 1) // PTK)
    if moba:
        aux_spec = pl.BlockSpec((LANES, KVW), lambda i, j: (0, 0))
    else:
        aux_spec = pl.BlockSpec((PTK, PTQ), lambda i, j: (jj(i, j), i))
    rows = GROUP * PTQ
    return pl.pallas_call(
        functools.partial(_prompt_attn_kernel, moba),
        grid=(nq, nk),
        in_specs=[
            pl.BlockSpec((PTQ, qw), lambda i, j: (i, c_q // qw)),
            pl.BlockSpec((PTK, KVW), lambda i, j: (jj(i, j), c_k // KVW)),
            pl.BlockSpec((PTK, KVW), lambda i, j: (jj(i, j), c_v // KVW)),
            aux_spec,
        ],
        out_specs=pl.BlockSpec((PTQ, qw), lambda i, j: (i, 0)),
        out_shape=jax.ShapeDtypeStruct((t, qw), BF16),
        scratch_shapes=[
            pltpu.VMEM((KV_HEADS, rows, HEAD_DIM), BF16),
            pltpu.VMEM((KV_HEADS, rows, LANES), F32),
            pltpu.VMEM((KV_HEADS, rows, HEAD_DIM + LANES), F32),
            pltpu.VMEM((KV_HEADS, rows, LANES), jnp.int32),
        ],
        compiler_params=_cparams(("parallel", "arbitrary")),
        name="moba_prompt" if moba else "dsa_prompt",
    )(proj, proj, proj, aux)


DTK = 512


COUNT_ROWS = 32


def _dsa_select_kernel(k_keep, qi_ref, ki_ref, wt_ref, bias_ref, qs_ref, key_ref):
    i = pl.program_id(0)
    j = pl.program_id(1)
    nj = pl.num_programs(1)
    length = key_ref.shape[0]

    @pl.when(j == 0)
    def _():
        qs_ref[...] = qi_ref[...].astype(BF16)
        key_ref[...] = jnp.full(key_ref.shape, INT_MIN, jnp.int32)

    @pl.when(j * DTK <= i * PTQ + PTQ - 1)
    def _():
        kb = ki_ref[...].astype(BF16)
        acc = jnp.zeros((DTK, PTQ), F32)
        for h in range(IDX_HEADS):
            s = _dot_nt(kb, qs_ref[:, h * IDX_DIM:(h + 1) * IDX_DIM])
            acc = acc + wt_ref[h:h + 1, :] * jnp.maximum(s, 0.0)
        key_pos = j * DTK + lax.broadcasted_iota(jnp.int32, (DTK, PTQ), 0)
        q_pos = i * PTQ + lax.broadcasted_iota(jnp.int32, (DTK, PTQ), 1)
        x = jnp.where(key_pos <= q_pos, acc, -jnp.inf)
        key_ref[pl.ds(pl.multiple_of(j * DTK, DTK), DTK), :] = _sortable_key(x)

    @pl.when(j == nj - 1)
    def _():
        n_tiles = (i * PTQ + PTQ - 1) // DTK + 1

        def count_ge(cand):
            def tile(c, acc):
                keys = key_ref[pl.ds(pl.multiple_of(c * DTK, DTK), DTK), :]
                hit = jnp.where(keys.reshape(DTK // COUNT_ROWS, COUNT_ROWS, PTQ) >= cand, 1, 0)
                return acc + jnp.sum(hit, axis=0)

            acc = lax.fori_loop(0, n_tiles, tile, jnp.zeros((COUNT_ROWS, PTQ), jnp.int32))
            return jnp.sum(acc, axis=0, keepdims=True)

        thr = _kth_largest_key(count_ge, (1, PTQ), k_keep)
        key_pos = lax.broadcasted_iota(jnp.int32, (length, PTQ), 0)
        q_pos = i * PTQ + lax.broadcasted_iota(jnp.int32, (length, PTQ), 1)
        ok = jnp.logical_and(key_ref[...] >= thr, key_pos <= q_pos)
        bias_ref[...] = jnp.where(ok, 0.0, NEG)


def _dsa_select(proj, t):
    nq, nj = t // PTQ, t // DTK
    k_keep = max(1, min(DSA_TOPK_MAX, t // 4))
    qw = IDX_HEADS * IDX_DIM
    last_j = lambda i: (i * PTQ + PTQ - 1) // DTK
    w_t = proj[:t, C_WI:C_WI + IDX_HEADS].T
    return pl.pallas_call(
        functools.partial(_dsa_select_kernel, k_keep),
        grid=(nq, nj),
        in_specs=[
            pl.BlockSpec((PTQ, qw), lambda i, j: (i, C_QI // qw)),
            pl.BlockSpec((DTK, IDX_DIM), lambda i, j: (jnp.minimum(j, last_j(i)), C_KI // IDX_DIM)),
            pl.BlockSpec((IDX_HEADS, PTQ), lambda i, j: (0, i)),
        ],
        out_specs=pl.BlockSpec((t, PTQ), lambda i, j: (0, i)),
        out_shape=jax.ShapeDtypeStruct((t, t), F32),
        scratch_shapes=[pltpu.VMEM((PTQ, qw), BF16), pltpu.VMEM((t, PTQ), jnp.int32)],
        compiler_params=_cparams(("parallel", "arbitrary")),
        name="dsa_select",
    )(proj, proj, w_t)


N_PAGES = PAST_LEN // PAGE_SIZE
PAGES_PER_BLOCK = MOBA_BLOCK // PAGE_SIZE
PAGE_ROWS = PAGE_SIZE * KV_HEADS


def _head_match(rows, cols, dec):
    r = lax.broadcasted_iota(jnp.int32, (rows, cols), 0)
    c = lax.broadcasted_iota(jnp.int32, (rows, cols), 1)
    shift = (GROUP * dec).bit_length() - 1
    return (r >> shift) == (c & (KV_HEADS - 1))


def _stage_new(dst_ref, src_ref):
    dst_ref[...] = jnp.zeros(dst_ref.shape, F32)
    dst_ref[0:src_ref.shape[0], :] = src_ref[...]


def _softmax_over_pages(s_ref, s_new, v_refs, nv_ref, o_ref):
    m = jnp.max(s_new, axis=-1, keepdims=True)
    for p in range(N_PAGES):
        m = jnp.maximum(m, jnp.max(s_ref[p], axis=-1, keepdims=True))
    pn = jnp.exp(s_new - m)
    l = jnp.sum(pn, axis=-1, keepdims=True)
    acc = _dot(pn.astype(BF16), nv_ref[...].astype(BF16))
    for p in range(N_PAGES):
        pr = jnp.exp(s_ref[p] - m)
        l = l + jnp.sum(pr, axis=-1, keepdims=True)
        acc = acc + _dot(pr.astype(BF16), v_refs[p][...].astype(BF16))
    o_ref[...] = acc / l


def _sample_moba_kernel(pt_ref, q_ref, kn_ref, vn_ref, *rest):
    k_refs, v_refs = rest[:N_PAGES], rest[N_PAGES:2 * N_PAGES]
    o_ref, s_ref, nk_ref, nv_ref = rest[2 * N_PAGES:]
    rows = q_ref.shape[0]
    dec = rows // MOBA_HEADS
    n_past_blocks = PAST_LEN // MOBA_BLOCK
    q = q_ref[...].astype(BF16)
    match = _head_match(rows, PAGE_ROWS, dec)
    lane = lax.broadcasted_iota(jnp.int32, (rows, LANES), 1)

    for p in range(N_PAGES):
        s_ref[p] = _dot_nt(q, k_refs[p][...].astype(BF16))

    gate = jnp.full((rows, LANES), -jnp.inf, F32)
    for b in range(n_past_blocks):
        tot = s_ref[PAGES_PER_BLOCK * b]
        for c in range(1, PAGES_PER_BLOCK):
            tot = tot + s_ref[PAGES_PER_BLOCK * b + c]
        g = jnp.sum(jnp.where(match, tot, 0.0), axis=-1, keepdims=True) * (1.0 / MOBA_BLOCK)
        gate = jnp.where(lane == b, g, gate)
    ids = _top_block_ids(gate, lane, MOBA_TOPK)

    for b in range(n_past_blocks):
        picked = ids[0] == b
        for idx in ids[1:]:
            picked = jnp.logical_or(picked, idx == b)
        row_bias = jnp.where(picked, 0.0, NEG)
        for c in range(PAGES_PER_BLOCK):
            p = PAGES_PER_BLOCK * b + c
            s_ref[p] = jnp.where(match, s_ref[p] * ATTN_SCALE + row_bias, NEG)

    _stage_new(nk_ref, kn_ref)
    _stage_new(nv_ref, vn_ref)
    rn = lax.broadcasted_iota(jnp.int32, (rows, LANES), 0)
    ok = jnp.logical_and(_head_match(rows, LANES, dec), (lane >> 2) <= (rn & (dec - 1)))
    s_new = jnp.where(ok, _dot_nt(q, nk_ref[...].astype(BF16)) * ATTN_SCALE, NEG)
    _softmax_over_pages(s_ref, s_new, v_refs, nv_ref, o_ref)


def _page_specs(shape, n):
    return [pl.BlockSpec((None,) + shape, functools.partial(lambda b, pt, p: (pt[b, p], 0, 0), p=p)) for p in range(n)]


def _sample_moba(pt, q, kn, vn, ck, cv):
    nb, rows, _ = q.shape
    dec = rows // MOBA_HEADS
    assert dec & (dec - 1) == 0 and dec * KV_HEADS <= LANES and KV_HEADS == 4
    seq = lambda b, pt: (b, 0, 0)
    return pl.pallas_call(
        _sample_moba_kernel,
        grid_spec=pltpu.PrefetchScalarGridSpec(
            num_scalar_prefetch=1,
            grid=(nb,),
            in_specs=[pl.BlockSpec((None, rows, HEAD_DIM), seq),
                      pl.BlockSpec((None, dec * KV_HEADS, HEAD_DIM), seq),
                      pl.BlockSpec((None, dec * KV_HEADS, HEAD_DIM), seq)]
                     + _page_specs((PAGE_ROWS, HEAD_DIM), N_PAGES) + _page_specs((PAGE_ROWS, HEAD_DIM), N_PAGES),
            out_specs=pl.BlockSpec((None, rows, HEAD_DIM), seq),
            scratch_shapes=[
                pltpu.VMEM((N_PAGES, rows, PAGE_ROWS), F32),
                pltpu.VMEM((LANES, HEAD_DIM), F32),
                pltpu.VMEM((LANES, HEAD_DIM), F32),
            ],
        ),
        out_shape=jax.ShapeDtypeStruct((nb, rows, HEAD_DIM), F32),
        compiler_params=_cparams(("parallel",)),
        name="moba_sample",
    )(pt, q, kn, vn, *([ck] * N_PAGES), *([cv] * N_PAGES))


def _sample_dsa_kernel(k_keep, pt_ref, qi_ref, wi_ref, q_ref, kin_ref, kn_ref, vn_ref, *rest):
    i_refs, k_refs, v_refs = rest[:N_PAGES], rest[N_PAGES:2 * N_PAGES], rest[2 * N_PAGES:3 * N_PAGES]
    o_ref, idx_ref, s_ref, nki_ref, nk_ref, nv_ref = rest[3 * N_PAGES:]
    rows = q_ref.shape[0]
    dec = rows // DSA_HEADS
    qi = qi_ref[...].astype(BF16)
    w = wi_ref[...]

    def index_scores(kpage):
        sc = jnp.maximum(_dot_nt(qi, kpage.astype(BF16)), 0.0) * w
        return jnp.sum(sc.reshape(dec, IDX_HEADS, PAGE_SIZE), axis=1)

    idx_ref[...] = jnp.full(idx_ref.shape, -jnp.inf, F32)
    for p in range(N_PAGES):
        idx_ref[p, 0:dec, :] = index_scores(i_refs[p][...])
    _stage_new(nki_ref, kin_ref)
    t_in = lax.broadcasted_iota(jnp.int32, (dec, PAGE_SIZE), 0)
    col = lax.broadcasted_iota(jnp.int32, (dec, PAGE_SIZE), 1)
    idx_ref[N_PAGES, 0:dec, :] = jnp.where(col <= t_in, index_scores(nki_ref[...]), -jnp.inf)

    keys = _sortable_key(idx_ref[...])

    def count_ge(cand):
        c = jnp.sum((keys >= cand).astype(jnp.int32), axis=0)
        return jnp.sum(c, axis=-1, keepdims=True)[None]

    thr = _kth_largest_key(count_ge, (1, SUBLANES, 1), k_keep)
    keep = jnp.where(jnp.logical_and(keys >= thr, idx_ref[...] > -jnp.inf), 1.0, 0.0)

    q = q_ref[...].astype(BF16)
    match = _head_match(rows, PAGE_ROWS, dec)
    er = lax.broadcasted_iota(jnp.int32, (PAGE_SIZE, PAGE_ROWS), 0)
    ec = lax.broadcasted_iota(jnp.int32, (PAGE_SIZE, PAGE_ROWS), 1)
    expand = jnp.where((ec >> 2) == er, 1.0, 0.0).astype(BF16)

    def visible(page, width):
        per_row = jnp.concatenate([keep[page, 0:dec, :]] * (rows // dec), axis=0).astype(BF16)
        return _dot(per_row, expand[:, 0:width]) > 0.5

    for p in range(N_PAGES):
        sc = _dot_nt(q, k_refs[p][...].astype(BF16)) * ATTN_SCALE
        s_ref[p] = jnp.where(jnp.logical_and(match, visible(p, PAGE_ROWS)), sc, NEG)
    _stage_new(nk_ref, kn_ref)
    _stage_new(nv_ref, vn_ref)
    ok = jnp.logical_and(_head_match(rows, LANES, dec), visible(N_PAGES, LANES))
    s_new = jnp.where(ok, _dot_nt(q, nk_ref[...].astype(BF16)) * ATTN_SCALE, NEG)
    _softmax_over_pages(s_ref, s_new, v_refs, nv_ref, o_ref)


def _sample_dsa(pt, qi, wi, q, kin, kn, vn, ci, ck, cv):
    nb, rows, _ = q.shape
    dec = rows // DSA_HEADS
    assert dec & (dec - 1) == 0 and dec <= SUBLANES and KV_HEADS == 4
    k_keep = max(1, min(DSA_TOPK_MAX, (PAST_LEN + dec) // 4))
    seq = lambda b, pt: (b, 0, 0)
    return pl.pallas_call(
        functools.partial(_sample_dsa_kernel, k_keep),
        grid_spec=pltpu.PrefetchScalarGridSpec(
            num_scalar_prefetch=1,
            grid=(nb,),
            in_specs=[pl.BlockSpec((None, dec * IDX_HEADS, IDX_DIM), seq),
                      pl.BlockSpec((None, dec * IDX_HEADS, 1), seq),
                      pl.BlockSpec((None, rows, HEAD_DIM), seq),
                      pl.BlockSpec((None, dec, IDX_DIM), seq),
                      pl.BlockSpec((None, dec * KV_HEADS, HEAD_DIM), seq),
                      pl.BlockSpec((None, dec * KV_HEADS, HEAD_DIM), seq)]
                     + _page_specs((PAGE_SIZE, IDX_DIM), N_PAGES)
                     + _page_specs((PAGE_ROWS, HEAD_DIM), N_PAGES) + _page_specs((PAGE_ROWS, HEAD_DIM), N_PAGES),
            out_specs=pl.BlockSpec((None, rows, HEAD_DIM), seq),
            scratch_shapes=[
                pltpu.VMEM((N_PAGES + 1, SUBLANES, PAGE_SIZE), F32),
                pltpu.VMEM((N_PAGES, rows, PAGE_ROWS), F32),
                pltpu.VMEM((PAGE_SIZE, IDX_DIM), F32),
                pltpu.VMEM((LANES, HEAD_DIM), F32),
                pltpu.VMEM((LANES, HEAD_DIM), F32),
            ],
        ),
        out_shape=jax.ShapeDtypeStruct((nb, rows, HEAD_DIM), F32),
        compiler_params=_cparams(("parallel",)),
        name="dsa_sample",
    )(pt, qi, wi, q, kin, kn, vn, *([ci] * N_PAGES), *([ck] * N_PAGES), *([cv] * N_PAGES))


def _outproj_kernel(tiles_a, om_ref, od_ref, ms_ref, w_ref, xa_ref, xb_ref, g_ref, h_ref, h2_ref, acc_ref):
    i = pl.program_id(0)
    k = pl.program_id(1)
    nk = pl.num_programs(1)
    first = i < tiles_a

    @pl.when(k == 0)
    def _():
        acc_ref[...] = jnp.zeros(acc_ref.shape, F32)

    mix = jnp.where(first, jnp.where(k < nk // 2, om_ref[...], od_ref[...]), ms_ref[...])
    acc_ref[...] += _dot(mix, w_ref[...])

    @pl.when(k == nk - 1)
    def _():
        h = jnp.where(first, xa_ref[...], xb_ref[...]) + acc_ref[...]
        h_ref[...] = h
        y = h * lax.rsqrt(jnp.mean(h * h, axis=-1, keepdims=True) + NORM_EPS)
        h2_ref[...] = (y * g_ref[...]).astype(h2_ref.dtype)


def _outproj(o_m, o_d, mix_b, w, xa, xb, g, tm, tk):
    (na, d), nb = xa.shape, xb.shape[0]
    n = na + nb
    nk = d // tk
    half = nk // 2
    tiles_a = na // tm
    assert na % tm == 0 and nb % tm == 0 and nk % 2 == 0 and o_m.shape[1] == half * tk
    row_a = lambda i: jnp.minimum(i, tiles_a - 1)
    return pl.pallas_call(
        functools.partial(_outproj_kernel, tiles_a),
        grid=(n // tm, nk),
        in_specs=[
            pl.BlockSpec((tm, tk), lambda i, k: (row_a(i), jnp.minimum(k, half - 1))),
            pl.BlockSpec((tm, tk), lambda i, k: (row_a(i), jnp.maximum(k - half, 0))),
            pl.BlockSpec((tm, tk), lambda i, k: (jnp.maximum(i - tiles_a, 0), k)),
            pl.BlockSpec((tk, d), lambda i, k: (k, 0)),
        ] + _two_source_specs(tm, d, tiles_a) + [pl.BlockSpec((1, d), lambda i, k: (0, 0))],
        out_specs=[pl.BlockSpec((tm, d), lambda i, k: (i, 0)), pl.BlockSpec((tm, d), lambda i, k: (i, 0))],
        out_shape=[jax.ShapeDtypeStruct((n, d), F32), jax.ShapeDtypeStruct((n, d), BF16)],
        scratch_shapes=[pltpu.VMEM((tm, d), F32)],
        compiler_params=_cparams(("parallel", "arbitrary")),
        name="outproj_norm",
    )(o_m, o_d, mix_b, w, xa, xb, g.reshape(1, d))


def _top_rows(x, k):
    r = x.shape[0]
    ridx = lax.broadcasted_iota(jnp.int32, x.shape, 0)
    outs = []
    for _ in range(k):
        mx = jnp.max(x, axis=0, keepdims=True)
        idx = jnp.min(jnp.where(x == mx, ridx, r), axis=0, keepdims=True)
        outs.append(mx)
        x = jnp.where(ridx == idx, -1.0, x)
    return jnp.concatenate(outs, axis=0)


def _peer_select_kernel(h2_ref, wq_ref, sk_ref, e_ref, thr_ref, iz_ref, acc_ref):
    k = pl.program_id(1)

    @pl.when(k == 0)
    def _():
        acc_ref[...] = jnp.zeros(acc_ref.shape, F32)

    acc_ref[...] += _dot(h2_ref[...], wq_ref[...])

    @pl.when(k == pl.num_programs(1) - 1)
    def _():
        half = PEER_DK // 2
        hk = PEER_TOPK // 2
        for h in range(PEER_HEADS):
            tops = []
            for p in range(2):
                c0 = (h * 2 + p) * half
                q = acc_ref[:, c0:c0 + half].astype(BF16)
                st = _dot_nt(sk_ref[h, p], q)
                e = jnp.exp(st - jnp.max(st, axis=0, keepdims=True))
                e_ref[h, p] = e
                tops.append(_top_rows(e, PEER_TOPK))
            a, b = tops
            cand = [a[x:x + 1, :] * b[0:hk, :] for x in range(hk)]
            cand += [a[0:1, :] * b[hk:, :], a[hk:, :] * b[0:1, :]]
            top = _top_rows(jnp.concatenate(cand, axis=0), PEER_TOPK)
            thr_ref[h:h + 1, :] = top[PEER_TOPK - 1:PEER_TOPK, :]
            iz_ref[h:h + 1, :] = 1.0 / jnp.sum(top, axis=0, keepdims=True)


def _peer_select(h2, wq, sk, tm, tk):
    n, d = h2.shape
    qw = wq.shape[1]
    return pl.pallas_call(
        _peer_select_kernel,
        grid=(n // tm, d // tk),
        in_specs=[
            pl.BlockSpec((tm, tk), lambda i, k: (i, k)),
            pl.BlockSpec((tk, qw), lambda i, k: (k, 0)),
            pl.BlockSpec(sk.shape, lambda i, k: (0, 0, 0, 0)),
        ],
        out_specs=[
            pl.BlockSpec((PEER_HEADS, 2, PEER_N_KEYS, tm), lambda i, k: (0, 0, 0, i)),
            pl.BlockSpec((PEER_HEADS, tm), lambda i, k: (0, i)),
            pl.BlockSpec((PEER_HEADS, tm), lambda i, k: (0, i)),
        ],
        out_shape=[
            jax.ShapeDtypeStruct((PEER_HEADS, 2, PEER_N_KEYS, n), F32),
            jax.ShapeDtypeStruct((PEER_HEADS, n), F32),
            jax.ShapeDtypeStruct((PEER_HEADS, n), F32),
        ],
        scratch_shapes=[pltpu.VMEM((tm, qw), F32)],
        compiler_params=_cparams(("parallel", "arbitrary")),
        name="peer_select",
    )(h2, wq, sk)


PEER_SUB = 256


def _peer_dense_kernel(h2_ref, u_ref, v_ref, e_ref, thr_ref, iz_ref, o_ref):
    e = pl.program_id(1)
    te = u_ref.shape[0]
    tm = h2_ref.shape[0]

    @pl.when(e == 0)
    def _():
        o_ref[...] = jnp.zeros(o_ref.shape, F32)

    h2 = h2_ref[...]
    total = None
    for sub in range(te // PEER_SUB):
        r0 = sub * PEER_SUB
        at = _dot_nt(u_ref[r0:r0 + PEER_SUB, :], h2)
        act = 0.5 * at * (1.0 + lax.erf(at * (2.0 ** -0.5)))
        parts = []
        for c in range(PEER_SUB // PEER_N_KEYS):
            i = (e * te + r0) // PEER_N_KEYS + c
            g = jnp.zeros((PEER_N_KEYS, tm), F32)
            for h in range(PEER_HEADS):
                p = e_ref[h, 0, pl.ds(i, 1), :] * e_ref[h, 1]
                g = g + jnp.where(p >= thr_ref[h:h + 1, :], p, 0.0) * iz_ref[h:h + 1, :]
            parts.append(g)
        ga = (jnp.concatenate(parts, axis=0) * act).astype(BF16)
        d = lax.dot_general(ga, v_ref[r0:r0 + PEER_SUB, :], (((0,), (0,)), ((), ())), preferred_element_type=F32)
        total = d if total is None else total + d
    o_ref[...] += total


def _peer_dense(h2, u, v, e, thr, iz, tm, te):
    n, d = h2.shape
    ne = u.shape[0]
    return pl.pallas_call(
        _peer_dense_kernel,
        grid=(n // tm, ne // te),
        in_specs=[
            pl.BlockSpec((tm, d), lambda i, k: (i, 0)),
            pl.BlockSpec((te, d), lambda i, k: (k, 0)),
            pl.BlockSpec((te, d), lambda i, k: (k, 0)),
            pl.BlockSpec((PEER_HEADS, 2, PEER_N_KEYS, tm), lambda i, k: (0, 0, 0, i)),
            pl.BlockSpec((PEER_HEADS, tm), lambda i, k: (0, i)),
            pl.BlockSpec((PEER_HEADS, tm), lambda i, k: (0, i)),
        ],
        out_specs=pl.BlockSpec((tm, d), lambda i, k: (i, 0)),
        out_shape=jax.ShapeDtypeStruct((n, d), F32),
        compiler_params=_cparams(("parallel", "arbitrary")),
        name="peer_dense",
    )(h2, u, v, e, thr, iz)


def _final_kernel(h_ref, f_ref, g_ref, o_ref):
    x = h_ref[...] + f_ref[...]
    y = x * lax.rsqrt(jnp.mean(x * x, axis=-1, keepdims=True) + NORM_EPS)
    o_ref[...] = y * g_ref[...]


def _final(h, f, g, row0, rows, tm):
    d = h.shape[1]
    assert row0 % tm == 0 and rows % tm == 0
    src = pl.BlockSpec((tm, d), lambda i: (i + row0 // tm, 0))
    return pl.pallas_call(
        _final_kernel,
        grid=(rows // tm,),
        in_specs=[src, src, pl.BlockSpec((1, d), lambda i: (0, 0))],
        out_specs=pl.BlockSpec((tm, d), lambda i: (i, 0)),
        out_shape=jax.ShapeDtypeStruct((rows, d), F32),
        compiler_params=_cparams(("parallel",)),
        name="final_norm",
    )(h, f, g.reshape(1, d))


def _heads_to_rows(q, nb, dec):
    return q.reshape(nb, dec, MOBA_HEADS, HEAD_DIM).transpose(0, 2, 1, 3).reshape(nb, MOBA_HEADS * dec, HEAD_DIM)


def _rows_to_heads(o, nb, dec):
    return o.reshape(nb, MOBA_HEADS, dec, HEAD_DIM).transpose(0, 2, 1, 3).reshape(nb * dec, MOBA_HEADS * HEAD_DIM)


def _sample_mix(proj_s, page_table, c_mk, c_mv, c_dk, c_dv, c_ik, nb, dec):
    ps = proj_s.reshape(nb, dec, PROJ_PAD)
    seg = lambda c, w: ps[:, :, c:c + w]
    kv_rows = lambda c: seg(c, KVW).reshape(nb, dec * KV_HEADS, HEAD_DIM)
    s_m = _sample_moba(page_table, _heads_to_rows(seg(C_QM, MOBA_HEADS * HEAD_DIM), nb, dec),
                       kv_rows(C_KM), kv_rows(C_VM), c_mk, c_mv)
    qi = seg(C_QI, IDX_HEADS * IDX_DIM).reshape(nb, dec * IDX_HEADS, IDX_DIM)
    wi = seg(C_WI, IDX_HEADS).reshape(nb, dec * IDX_HEADS, 1)
    s_d = _sample_dsa(page_table, qi, wi, _heads_to_rows(seg(C_QD, DSA_HEADS * HEAD_DIM), nb, dec),
                      seg(C_KI, IDX_DIM), kv_rows(C_KD), kv_rows(C_VD), c_ik, c_dk, c_dv)
    return jnp.concatenate([_rows_to_heads(s_m, nb, dec), _rows_to_heads(s_d, nb, dec)], axis=-1)


def _prompt_mix(proj, t):
    kmean = _kmean(proj, t)
    kmean = jnp.pad(kmean, ((0, LANES - kmean.shape[0]), (0, 0)))
    o_m = _prompt_attn(True, proj, kmean, t)
    o_d = _prompt_attn(False, proj, _dsa_select(proj, t), t)
    return o_m, o_d


def kernel(x_prompt, x_sample, cache_moba_k, cache_moba_v, cache_dsa_k, cache_dsa_v, cache_idx_k, page_table,
           norm_mix, w_in, w_out, norm_ffn, peer_w_q, peer_subkeys, peer_u, peer_v, norm_final):
    depth = w_in.shape[0]
    assert depth == 1 and x_prompt.shape[0] == 1
    assert PAST_LEN % MOBA_BLOCK == 0 and page_table.shape[1] == N_PAGES
    t = x_prompt.shape[1]
    nb, dec, d = x_sample.shape
    n_tok = t + nb * dec
    tm = 512
    assert n_tok % tm == 0 and t % tm == 0

    x_p = x_prompt.reshape(t, d)
    x_s = x_sample.reshape(nb * dec, d)
    pos = jnp.concatenate([jnp.arange(t, dtype=jnp.int32),
                           PAST_LEN + jnp.tile(jnp.arange(dec, dtype=jnp.int32), nb)])
    cos, sin = _rope_tables(pos)

    n_pool = cache_moba_k.shape[1]
    flat = lambda c: c[0].reshape(n_pool, PAGE_ROWS, HEAD_DIM)

    u = _rmsnorm(x_p, x_s, norm_mix[0], BF16, tm)
    tm_proj = n_tok // 8 if (n_tok // 8) % 16 == 0 and n_tok % 8 == 0 else tm
    proj = _project(u, w_in[0].astype(BF16), cos, sin, tm_proj)
    o_m, o_d = _prompt_mix(proj, t)
    mix_s = _sample_mix(proj[t:], page_table, flat(cache_moba_k), flat(cache_moba_v), flat(cache_dsa_k),
                        flat(cache_dsa_v), cache_idx_k[0], nb, dec)
    h, h2 = _outproj(o_m, o_d, mix_s.astype(BF16), w_out[0].astype(BF16), x_p, x_s, norm_ffn[0], 256, 512)
    e, thr, iz = _peer_select(h2, peer_w_q[0].astype(BF16), peer_subkeys[0].astype(BF16), tm, 512)
    f = _peer_dense(h2, peer_u[0].astype(BF16), peer_v[0].astype(BF16), e, thr, iz, tm, 512)

    y_prompt = _final(h, f, norm_final, 0, t, 256).reshape(1, t, d)
    y_sample = _final(h, f, norm_final, t, nb * dec, 256).reshape(nb, dec, d)
    pp = proj[:t]
    ps = proj[t:]
    kv = lambda a, c: a[:, c:c + KVW]
    outs_p = [kv(pp, c).reshape(1, 1, t, KV_HEADS, HEAD_DIM) for c in (C_KM, C_VM, C_KD, C_VD)]
    outs_p.append(pp[:, C_KI:C_KI + IDX_DIM].reshape(1, 1, t, IDX_DIM))
    outs_s = [kv(ps, c).reshape(1, nb, dec, KV_HEADS, HEAD_DIM) for c in (C_KM, C_VM, C_KD, C_VD)]
    outs_s.append(ps[:, C_KI:C_KI + IDX_DIM].reshape(1, nb, dec, IDX_DIM))
    return (y_prompt, y_sample, *outs_p, *outs_s)
```

```python
import functools

import numpy as np
import jax
import jax.numpy as jnp
from jax import lax
from jax.experimental import pallas as pl
from jax.experimental.pallas import tpu as pltpu

F32 = jnp.float32
BF16 = jnp.bfloat16

D_MODEL = 4096
HEAD_DIM = 128
MOBA_HEADS = 16
DSA_HEADS = 16
KV_HEADS = 4
GROUP = 4
MOBA_BLOCK = 256
MOBA_TOPK = 3
IDX_HEADS = 32
IDX_DIM = 128
DSA_TOPK_MAX = 256
ROPE_THETA = 10000.0
ATTN_SCALE = HEAD_DIM ** -0.5
IDX_W_SCALE = (IDX_HEADS ** -0.5) * (IDX_DIM ** -0.5)
PEER_HEADS = 8
PEER_N_KEYS = 128
PEER_DK = 256
PEER_TOPK = 16
NORM_EPS = 1e-6
PAST_LEN = 2048
PAGE_SIZE = 128

LANES = 128
SUBLANES = 8
NEG = -1e30
INT_MIN = -2 ** 31
VMEM_LIMIT = 56 * 1024 * 1024

C_QI = 0
C_QM = C_QI + IDX_HEADS * IDX_DIM
C_QD = C_QM + MOBA_HEADS * HEAD_DIM
C_KM = C_QD + DSA_HEADS * HEAD_DIM
C_VM = C_KM + KV_HEADS * HEAD_DIM
C_KD = C_VM + KV_HEADS * HEAD_DIM
C_VD = C_KD + KV_HEADS * HEAD_DIM
C_KI = C_VD + KV_HEADS * HEAD_DIM
C_WI = C_KI + IDX_DIM
PROJ_TN = 512
PROJ_PAD = 10752
KVW = KV_HEADS * HEAD_DIM


def _cparams(sem):
    return pltpu.CompilerParams(dimension_semantics=sem, vmem_limit_bytes=VMEM_LIMIT)


def _dot_nt(a, b):
    return lax.dot_general(a, b, (((1,), (1,)), ((), ())), preferred_element_type=F32)


def _dot(a, b):
    return jnp.dot(a, b, preferred_element_type=F32)


def _two_source_specs(tm, d, tiles_a):
    return [pl.BlockSpec((tm, d), lambda i, *_: (jnp.minimum(i, tiles_a - 1), 0)),
            pl.BlockSpec((tm, d), lambda i, *_: (jnp.maximum(i - tiles_a, 0), 0))]


def _rmsnorm_kernel(tiles_a, xa_ref, xb_ref, g_ref, o_ref):
    x = jnp.where(pl.program_id(0) < tiles_a, xa_ref[...], xb_ref[...])
    y = x * lax.rsqrt(jnp.mean(x * x, axis=-1, keepdims=True) + NORM_EPS)
    o_ref[...] = (y * g_ref[...]).astype(o_ref.dtype)


def _rmsnorm(xa, xb, g, out_dtype, tm):
    (na, d), nb = xa.shape, xb.shape[0]
    assert na % tm == 0 and nb % tm == 0
    return pl.pallas_call(
        functools.partial(_rmsnorm_kernel, na // tm),
        grid=((na + nb) // tm,),
        in_specs=_two_source_specs(tm, d, na // tm) + [pl.BlockSpec((1, d), lambda i: (0, 0))],
        out_specs=pl.BlockSpec((tm, d), lambda i: (i, 0)),
        out_shape=jax.ShapeDtypeStruct((na + nb, d), out_dtype),
        compiler_params=_cparams(("parallel",)),
        name="rmsnorm",
    )(xa, xb, g.reshape(1, d))


def _rope_group(a, cos, sin):
    return a * cos + pltpu.roll(a, HEAD_DIM // 2, axis=1) * sin


def _proj_kernel(u_ref, w_ref, cos_ref, sin_ref, o_ref):
    j = pl.program_id(1)
    acc = _dot(u_ref[...], w_ref[...])
    plain = jnp.logical_or(j == C_VM // PROJ_TN, j == C_VD // PROJ_TN)
    last = j == C_KI // PROJ_TN
    groups = PROJ_TN // HEAD_DIM

    @pl.when(plain)
    def _():
        o_ref[...] = acc

    @pl.when(jnp.logical_not(jnp.logical_or(plain, last)))
    def _():
        cos = cos_ref[...]
        sin = sin_ref[...]
        for c in range(groups):
            o_ref[:, c * HEAD_DIM:(c + 1) * HEAD_DIM] = _rope_group(acc[:, c * HEAD_DIM:(c + 1) * HEAD_DIM], cos, sin)

    @pl.when(last)
    def _():
        o_ref[:, 0:HEAD_DIM] = _rope_group(acc[:, 0:HEAD_DIM], cos_ref[...], sin_ref[...])
        o_ref[:, HEAD_DIM:2 * HEAD_DIM] = acc[:, HEAD_DIM:2 * HEAD_DIM] * IDX_W_SCALE
        o_ref[:, 2 * HEAD_DIM:] = acc[:, 2 * HEAD_DIM:]


def _w_in_tile_order():
    model = dict(q_m=0, k_m=MOBA_HEADS * HEAD_DIM)
    model["v_m"] = model["k_m"] + KVW
    model["q_d"] = model["v_m"] + KVW
    model["k_d"] = model["q_d"] + DSA_HEADS * HEAD_DIM
    model["v_d"] = model["k_d"] + KVW
    model["q_i"] = model["v_d"] + KVW
    model["k_i"] = model["q_i"] + IDX_HEADS * IDX_DIM
    ours = (("q_i", C_QI, C_QM), ("q_m", C_QM, C_QD), ("q_d", C_QD, C_KM), ("k_m", C_KM, C_VM), ("v_m", C_VM, C_KD),
            ("k_d", C_KD, C_VD), ("v_d", C_VD, C_KI), ("k_i", C_KI, PROJ_PAD))
    order = []
    for name, lo, hi in ours:
        assert lo % PROJ_TN == 0 and model[name] % PROJ_TN == 0
        order += [model[name] // PROJ_TN + x for x in range((hi - lo) // PROJ_TN)]
    return np.asarray(order, np.int32)


def _proj_kernel_mapped(src_ref, u_ref, w_ref, cos_ref, sin_ref, o_ref):
    _proj_kernel(u_ref, w_ref, cos_ref, sin_ref, o_ref)


def _project(u, w, cos, sin, tm):
    n, d = u.shape
    return pl.pallas_call(
        _proj_kernel_mapped,
        grid_spec=pltpu.PrefetchScalarGridSpec(
            num_scalar_prefetch=1,
            grid=(n // tm, PROJ_PAD // PROJ_TN),
            in_specs=[
                pl.BlockSpec((tm, d), lambda i, j, src: (i, 0)),
                pl.BlockSpec((d, PROJ_TN), lambda i, j, src: (0, src[j])),
                pl.BlockSpec((tm, HEAD_DIM), lambda i, j, src: (i, 0)),
                pl.BlockSpec((tm, HEAD_DIM), lambda i, j, src: (i, 0)),
            ],
            out_specs=pl.BlockSpec((tm, PROJ_TN), lambda i, j, src: (i, j)),
        ),
        out_shape=jax.ShapeDtypeStruct((n, PROJ_PAD), F32),
        compiler_params=_cparams(("parallel", "arbitrary")),
        name="proj_rope",
    )(jnp.asarray(_w_in_tile_order()), u, w, cos, sin)


def _rope_tables(pos):
    half = HEAD_DIM // 2
    inv = 1.0 / (ROPE_THETA ** (jnp.arange(half, dtype=F32) * (2.0 / HEAD_DIM)))
    ang = pos.astype(F32)[:, None] * inv[None, :]
    c, s = jnp.cos(ang), jnp.sin(ang)
    return jnp.concatenate([c, c], axis=-1), jnp.concatenate([-s, s], axis=-1)


def _kmean_kernel(k_ref, o_ref):
    nb = o_ref.shape[0]
    k = k_ref[...].reshape(nb, MOBA_BLOCK, KVW)
    o_ref[...] = jnp.sum(k, axis=1) * (1.0 / MOBA_BLOCK)


def _kmean(proj, t):
    nb = t // MOBA_BLOCK
    return pl.pallas_call(
        _kmean_kernel,
        grid=(1,),
        in_specs=[pl.BlockSpec((t, KVW), lambda i: (0, C_KM // KVW))],
        out_specs=pl.BlockSpec((nb, KVW), lambda i: (0, 0)),
        out_shape=jax.ShapeDtypeStruct((nb, KVW), F32),
        compiler_params=_cparams(("arbitrary",)),
        name="moba_kmean",
    )(proj)


def _top_block_ids(gate, lane, n_sel):
    ids = []
    for _ in range(n_sel):
        mx = jnp.max(gate, axis=-1, keepdims=True)
        idx = jnp.min(jnp.where(gate == mx, lane, LANES), axis=-1, keepdims=True)
        ids.append(jnp.where(mx > -jnp.inf, idx, -1))
        gate = jnp.where(lane == idx, -jnp.inf, gate)
    return ids


def _sortable_key(x):
    bits = pltpu.bitcast(x + 0.0, jnp.int32)
    return bits ^ ((bits >> 31) & 0x7FFFFFFF)


def _kth_largest_key(count_ge, shape, k, two_bits=False):
    t = jnp.broadcast_to(jnp.where(count_ge(jnp.zeros(shape, jnp.int32)) >= k, 0, INT_MIN).astype(jnp.int32), shape)

    def one_bit(bit, t):
        cand = t + (jnp.int32(1) << bit)
        return jnp.where(count_ge(cand) >= k, cand, t)

    if not two_bits:
        return lax.fori_loop(0, 31, lambda it, t: one_bit(30 - it, t), t)

    def pair(it, t):
        step = jnp.int32(1) << (29 - 2 * it)
        c1, c2, c3 = t + step, t + 2 * step, t + 3 * step
        n1, n2, n3 = count_ge(c1), count_ge(c2), count_ge(c3)
        return jnp.where(n3 >= k, c3, jnp.where(n2 >= k, c2, jnp.where(n1 >= k, c1, t)))

    return one_bit(0, lax.fori_loop(0, 15, pair, t))


PTQ = 256
PTK = 512


def _prompt_attn_kernel(moba, q_ref, k_ref, v_ref, aux_ref, o_ref, qs_ref, m_ref, acc_ref, sel_ref):
    i = pl.program_id(0)
    j = pl.program_id(1)
    rows = GROUP * PTQ
    last_j = (i * PTQ + PTQ - 1) // PTK
    chunks = PTK // LANES

    @pl.when(j == 0)
    def _():
        for n in range(KV_HEADS):
            for g in range(GROUP):
                h = n * GROUP + g
                qs_ref[n, g * PTQ:(g + 1) * PTQ, :] = q_ref[:, h * HEAD_DIM:(h + 1) * HEAD_DIM].astype(BF16)
        m_ref[...] = jnp.full(m_ref.shape, NEG, F32)
        acc_ref[...] = jnp.zeros(acc_ref.shape, F32)
        if moba:
            lane = lax.broadcasted_iota(jnp.int32, (rows, LANES), 1)
            for n in range(KV_HEADS):
                km =aux_ref[:, n * HEAD_DIM:(n + 1) * HEAD_DIM].astype(BF16)
                gate = jnp.where(lane < i, _dot_nt(qs_ref[n], km), -jnp.inf)
                bits = jnp.zeros((rows, 1), jnp.int32)
                for idx in _top_block_ids(gate, lane, MOBA_TOPK):
                    bits = bits | jnp.where(idx >= 0, jnp.int32(1) << jnp.maximum(idx, 0), 0)
                sel_ref[n] = jnp.broadcast_to(bits, (rows, LANES))

    def step(diagonal):
        k = k_ref[...].astype(BF16)
        v = v_ref[...].astype(BF16)
        ones = jnp.ones((PTK, HEAD_DIM), BF16)
        if moba and diagonal:
            lane = lax.broadcasted_iota(jnp.int32, (rows, LANES), 1)
            t_in = lax.broadcasted_iota(jnp.int32, (rows, LANES), 0) & (PTQ - 1)
        if not moba:
            bias = jnp.concatenate([aux_ref[...].T] * GROUP, axis=0)
        for n in range(KV_HEADS):
            s = _dot_nt(qs_ref[n], k[:, n * HEAD_DIM:(n + 1) * HEAD_DIM]) * ATTN_SCALE
            if moba and diagonal:
                bits = sel_ref[n]
                limits = []
                for b in range(PTK // MOBA_BLOCK):
                    kb = j * (PTK // MOBA_BLOCK) + b
                    picked = (bits >> jnp.minimum(kb, 31)) & 1
                    limits.append(jnp.where(kb == i, t_in, jnp.where(kb < i, picked * MOBA_BLOCK, 0) - 1))
                parts = []
                for c in range(chunks):
                    col = lane + (c * LANES) % MOBA_BLOCK
                    parts.append(jnp.where(col <= limits[(c * LANES) // MOBA_BLOCK],
                                           s[:, c * LANES:(c + 1) * LANES], NEG))
            elif moba:
                bits = sel_ref[n]
                row_bias = [jnp.where(((bits >> (j * (PTK // MOBA_BLOCK) + b)) & 1) == 1, 0.0, NEG)
                            for b in range(PTK // MOBA_BLOCK)]
                parts = [s[:, c * LANES:(c + 1) * LANES] + row_bias[(c * LANES) // MOBA_BLOCK] for c in range(chunks)]
            else:
                s = s + bias
                parts = [s[:, c * LANES:(c + 1) * LANES] for c in range(chunks)]
            mx = parts[0]
            for c in range(1, chunks):
                mx = jnp.maximum(mx, parts[c])
            m_prev = m_ref[n]
            m_new = jnp.maximum(m_prev, jnp.max(mx, axis=-1, keepdims=True))
            alpha = jnp.exp(m_prev - m_new)
            p = jnp.concatenate([jnp.exp(x - m_new) for x in parts], axis=1).astype(BF16)
            v_ones = jnp.concatenate([v[:, n * HEAD_DIM:(n + 1) * HEAD_DIM], ones], axis=1)
            acc_ref[n] = jnp.concatenate([alpha, alpha], axis=1) * acc_ref[n] + _dot(p, v_ones)
            m_ref[n] = m_new

    if moba:
        pl.when(j < last_j)(functools.partial(step, False))
        pl.when(j == last_j)(functools.partial(step, True))
    else:
        pl.when(j <= last_j)(functools.partial(step, False))

    @pl.when(j == last_j)
    def _():
        for n in range(KV_HEADS):
            for g in range(GROUP):
                h = n * GROUP + g
                r = slice(g * PTQ, (g + 1) * PTQ)
                o_ref[:, h * HEAD_DIM:(h + 1) * HEAD_DIM] = (acc_ref[n, r, 0:HEAD_DIM]
                                                             / acc_ref[n, r, HEAD_DIM:]).astype(o_ref.dtype)


def _prompt_attn(moba, proj, aux, t):
    nq, nk = t // PTQ, t // PTK
    assert t // MOBA_BLOCK <= 32
    qw = MOBA_HEADS * HEAD_DIM
    c_q, c_k, c_v = (C_QM, C_KM, C_VM) if moba else (C_QD, C_KD, C_VD)
    jj = lambda i, j: jnp.minimum(j, (i * PTQ + PTQ - 1) // PTK)
    if moba:
        aux_spec = pl.BlockSpec((LANES, KVW), lambda i, j: (0, 0))
    else:
        aux_spec = pl.BlockSpec((PTK, PTQ), lambda i, j: (jj(i, j), i))
    rows = GROUP * PTQ
    return pl.pallas_call(
        functools.partial(_prompt_attn_kernel, moba),
        grid=(nq, nk),
        in_specs=[
            pl.BlockSpec((PTQ, qw), lambda i, j: (i, c_q // qw)),
            pl.BlockSpec((PTK, KVW), lambda i, j: (jj(i, j), c_k // KVW)),
            pl.BlockSpec((PTK, KVW), lambda i, j: (jj(i, j), c_v // KVW)),
            aux_spec,
        ],
        out_specs=pl.BlockSpec((PTQ, qw), lambda i, j: (i, 0)),
        out_shape=jax.ShapeDtypeStruct((t, qw), BF16),
        scratch_shapes=[
            pltpu.VMEM((KV_HEADS, rows, HEAD_DIM), BF16),
            pltpu.VMEM((KV_HEADS, rows, LANES), F32),
            pltpu.VMEM((KV_HEADS, rows, HEAD_DIM + LANES), F32),
            pltpu.VMEM((KV_HEADS, rows, LANES), jnp.int32),
        ],
        compiler_params=_cparams(("parallel", "arbitrary")),
        name="moba_prompt" if moba else "dsa_prompt",
    )(proj, proj, proj, aux)


DTK = 512


COUNT_ROWS = 32


def _dsa_select_kernel(k_keep, qi_ref, ki_ref, wt_ref, bias_ref, qs_ref, key_ref):
    i = pl.program_id(0)
    j = pl.program_id(1)
    nj = pl.num_programs(1)
    length = key_ref.shape[0]

    @pl.when(j == 0)
    def _():
        qs_ref[...] = qi_ref[...].astype(BF16)
        key_ref[...] = jnp.full(key_ref.shape, INT_MIN, jnp.int32)

    @pl.when(j * DTK <= i * PTQ + PTQ - 1)
    def _():
        kb = ki_ref[...].astype(BF16)
        acc = jnp.zeros((DTK, PTQ), F32)
        for h in range(IDX_HEADS):
            s = _dot_nt(kb, qs_ref[:, h * IDX_DIM:(h + 1) * IDX_DIM])
            acc = acc + wt_ref[h:h + 1, :] * jnp.maximum(s, 0.0)
        key_pos = j * DTK + lax.broadcasted_iota(jnp.int32, (DTK, PTQ), 0)
        q_pos = i * PTQ + lax.broadcasted_iota(jnp.int32, (DTK, PTQ), 1)
        x = jnp.where(key_pos <= q_pos, acc, -jnp.inf)
        key_ref[pl.ds(pl.multiple_of(j * DTK, DTK), DTK), :] = _sortable_key(x)

    @pl.when(j == nj - 1)
    def _():
        n_tiles = (i * PTQ + PTQ - 1) // DTK + 1

        def count_ge(cand):
            def tile(c, acc):
                keys = key_ref[pl.ds(pl.multiple_of(c * DTK, DTK), DTK), :]
                hit = jnp.where(keys.reshape(DTK // COUNT_ROWS, COUNT_ROWS, PTQ) >= cand, 1, 0)
                return acc + jnp.sum(hit, axis=0)

            acc = lax.fori_loop(0, n_tiles, tile, jnp.zeros((COUNT_ROWS, PTQ), jnp.int32))
            return jnp.sum(acc, axis=0, keepdims=True)

        thr = _kth_largest_key(count_ge, (1, PTQ), k_keep)
        key_pos = lax.broadcasted_iota(jnp.int32, (length, PTQ), 0)
        q_pos = i * PTQ + lax.broadcasted_iota(jnp.int32, (length, PTQ), 1)
        ok = jnp.logical_and(key_ref[...] >= thr, key_pos <= q_pos)
        bias_ref[...] = jnp.where(ok, 0.0, NEG)


def _dsa_select(proj, t):
    nq, nj = t // PTQ, t // DTK
    k_keep = max(1, min(DSA_TOPK_MAX, t // 4))
    qw = IDX_HEADS * IDX_DIM
    last_j = lambda i: (i * PTQ + PTQ - 1) // DTK
    w_t = proj[:t, C_WI:C_WI + IDX_HEADS].T
    return pl.pallas_call(
        functools.partial(_dsa_select_kernel, k_keep),
        grid=(nq, nj),
        in_specs=[
            pl.BlockSpec((PTQ, qw), lambda i, j: (i, C_QI // qw)),
            pl.BlockSpec((DTK, IDX_DIM), lambda i, j: (jnp.minimum(j, last_j(i)), C_KI // IDX_DIM)),
            pl.BlockSpec((IDX_HEADS, PTQ), lambda i, j: (0, i)),
        ],
        out_specs=pl.BlockSpec((t, PTQ), lambda i, j: (0, i)),
        out_shape=jax.ShapeDtypeStruct((t, t), F32),
        scratch_shapes=[pltpu.VMEM((PTQ, qw), BF16), pltpu.VMEM((t, PTQ), jnp.int32)],
        compiler_params=_cparams(("parallel", "arbitrary")),
        name="dsa_select",
    )(proj, proj, w_t)


N_PAGES = PAST_LEN // PAGE_SIZE
PAGES_PER_BLOCK = MOBA_BLOCK // PAGE_SIZE
PAGE_ROWS = PAGE_SIZE * KV_HEADS


def _head_match(rows, cols, dec):
    r = lax.broadcasted_iota(jnp.int32, (rows, cols), 0)
    c = lax.broadcasted_iota(jnp.int32, (rows, cols), 1)
    shift = (GROUP * dec).bit_length() - 1
    return (r >> shift) == (c & (KV_HEADS - 1))


def _stage_new(dst_ref, src_ref):
    dst_ref[...] = jnp.zeros(dst_ref.shape, F32)
    dst_ref[0:src_ref.shape[0], :] = src_ref[...]


def _softmax_over_pages(s_ref, s_new, v_refs, nv_ref, o_ref):
    m = jnp.max(s_new, axis=-1, keepdims=True)
    for p in range(N_PAGES):
        m = jnp.maximum(m, jnp.max(s_ref[p], axis=-1, keepdims=True))
    pn = jnp.exp(s_new - m)
    l = jnp.sum(pn, axis=-1, keepdims=True)
    acc = _dot(pn.astype(BF16), nv_ref[...].astype(BF16))
    for p in range(N_PAGES):
        pr = jnp.exp(s_ref[p] - m)
        l = l + jnp.sum(pr, axis=-1, keepdims=True)
        acc = acc + _dot(pr.astype(BF16), v_refs[p][...].astype(BF16))
    o_ref[...] = acc / l


def _sample_moba_kernel(pt_ref, q_ref, kn_ref, vn_ref, *rest):
    k_refs, v_refs = rest[:N_PAGES], rest[N_PAGES:2 * N_PAGES]
    o_ref, s_ref, nk_ref, nv_ref = rest[2 * N_PAGES:]
    rows = q_ref.shape[0]
    dec = rows // MOBA_HEADS
    n_past_blocks = PAST_LEN // MOBA_BLOCK
    q = q_ref[...].astype(BF16)
    match = _head_match(rows, PAGE_ROWS, dec)
    lane = lax.broadcasted_iota(jnp.int32, (rows, LANES), 1)

    for p in range(N_PAGES):
        s_ref[p] = _dot_nt(q, k_refs[p][...].astype(BF16))

    gate = jnp.full((rows, LANES), -jnp.inf, F32)
    for b in range(n_past_blocks):
        tot = s_ref[PAGES_PER_BLOCK * b]
        for c in range(1, PAGES_PER_BLOCK):
            tot = tot + s_ref[PAGES_PER_BLOCK * b + c]
        g = jnp.sum(jnp.where(match, tot, 0.0), axis=-1, keepdims=True) * (1.0 / MOBA_BLOCK)
        gate = jnp.where(lane == b, g, gate)
    ids = _top_block_ids(gate, lane, MOBA_TOPK)

    for b in range(n_past_blocks):
        picked = ids[0] == b
        for idx in ids[1:]:
            picked = jnp.logical_or(picked, idx == b)
        row_bias = jnp.where(picked, 0.0, NEG)
        for c in range(PAGES_PER_BLOCK):
            p = PAGES_PER_BLOCK * b + c
            s_ref[p] = jnp.where(match, s_ref[p] * ATTN_SCALE + row_bias, NEG)

    _stage_new(nk_ref, kn_ref)
    _stage_new(nv_ref, vn_ref)
    rn = lax.broadcasted_iota(jnp.int32, (rows, LANES), 0)
    ok = jnp.logical_and(_head_match(rows, LANES, dec), (lane >> 2) <= (rn & (dec - 1)))
    s_new = jnp.where(ok, _dot_nt(q, nk_ref[...].astype(BF16)) * ATTN_SCALE, NEG)
    _softmax_over_pages(s_ref, s_new, v_refs, nv_ref, o_ref)


def _page_specs(shape, n):
    return [pl.BlockSpec((None,) + shape, functools.partial(lambda b, pt, p: (pt[b, p], 0, 0), p=p)) for p in range(n)]


def _sample_moba(pt, q, kn, vn, ck, cv):
    nb, rows, _ = q.shape
    dec = rows // MOBA_HEADS
    assert dec & (dec - 1) == 0 and dec * KV_HEADS <= LANES and KV_HEADS == 4
    seq = lambda b, pt: (b, 0, 0)
    return pl.pallas_call(
        _sample_moba_kernel,
        grid_spec=pltpu.PrefetchScalarGridSpec(
            num_scalar_prefetch=1,
            grid=(nb,),
            in_specs=[pl.BlockSpec((None, rows, HEAD_DIM), seq),
                      pl.BlockSpec((None, dec * KV_HEADS, HEAD_DIM), seq),
                      pl.BlockSpec((None, dec * KV_HEADS, HEAD_DIM), seq)]
                     + _page_specs((PAGE_ROWS, HEAD_DIM), N_PAGES) + _page_specs((PAGE_ROWS, HEAD_DIM), N_PAGES),
            out_specs=pl.BlockSpec((None, rows, HEAD_DIM), seq),
            scratch_shapes=[
                pltpu.VMEM((N_PAGES, rows, PAGE_ROWS), F32),
                pltpu.VMEM((LANES, HEAD_DIM), F32),
                pltpu.VMEM((LANES, HEAD_DIM), F32),
            ],
        ),
        out_shape=jax.ShapeDtypeStruct((nb, rows, HEAD_DIM), F32),
        compiler_params=_cparams(("parallel",)),
        name="moba_sample",
    )(pt, q, kn, vn, *([ck] * N_PAGES), *([cv] * N_PAGES))


def _sample_dsa_kernel(k_keep, pt_ref, qi_ref, wi_ref, q_ref, kin_ref, kn_ref, vn_ref, *rest):
    i_refs, k_refs, v_refs = rest[:N_PAGES], rest[N_PAGES:2 * N_PAGES], rest[2 * N_PAGES:3 * N_PAGES]
    o_ref, idx_ref, s_ref, nki_ref, nk_ref, nv_ref = rest[3 * N_PAGES:]
    rows = q_ref.shape[0]
    dec = rows // DSA_HEADS
    qi = qi_ref[...].astype(BF16)
    w = wi_ref[...]

    def index_scores(kpage):
        sc = jnp.maximum(_dot_nt(qi, kpage.astype(BF16)), 0.0) * w
        return jnp.sum(sc.reshape(dec, IDX_HEADS, PAGE_SIZE), axis=1)

    idx_ref[...] = jnp.full(idx_ref.shape, -jnp.inf, F32)
    for p in range(N_PAGES):
        idx_ref[p, 0:dec, :] = index_scores(i_refs[p][...])
    _stage_new(nki_ref, kin_ref)
    t_in = lax.broadcasted_iota(jnp.int32, (dec, PAGE_SIZE), 0)
    col = lax.broadcasted_iota(jnp.int32, (dec, PAGE_SIZE), 1)
    idx_ref[N_PAGES, 0:dec, :] = jnp.where(col <= t_in, index_scores(nki_ref[...]), -jnp.inf)

    keys = _sortable_key(idx_ref[...])

    def count_ge(cand):
        c = jnp.sum((keys >= cand).astype(jnp.int32), axis=0)
        return jnp.sum(c, axis=-1, keepdims=True)[None]

    thr = _kth_largest_key(count_ge, (1, SUBLANES, 1), k_keep, two_bits=True)
    keep = jnp.where(jnp.logical_and(keys >= thr, idx_ref[...] > -jnp.inf), 1.0, 0.0)

    q = q_ref[...].astype(BF16)
    match = _head_match(rows, PAGE_ROWS, dec)
    er = lax.broadcasted_iota(jnp.int32, (PAGE_SIZE, PAGE_ROWS), 0)
    ec = lax.broadcasted_iota(jnp.int32, (PAGE_SIZE, PAGE_ROWS), 1)
    expand = jnp.where((ec >> 2) == er, 1.0, 0.0).astype(BF16)

    def visible(page, width):
        per_row = jnp.concatenate([keep[page, 0:dec, :]] * (rows // dec), axis=0).astype(BF16)
        return _dot(per_row, expand[:, 0:width]) > 0.5

    for p in range(N_PAGES):
        sc = _dot_nt(q, k_refs[p][...].astype(BF16)) * ATTN_SCALE
        s_ref[p] = jnp.where(jnp.logical_and(match, visible(p, PAGE_ROWS)), sc, NEG)
    _stage_new(nk_ref, kn_ref)
    _stage_new(nv_ref, vn_ref)
    ok = jnp.logical_and(_head_match(rows, LANES, dec), visible(N_PAGES, LANES))
    s_new = jnp.where(ok, _dot_nt(q, nk_ref[...].astype(BF16)) * ATTN_SCALE, NEG)
    _softmax_over_pages(s_ref, s_new, v_refs, nv_ref, o_ref)


def _sample_dsa(pt, qi, wi, q, kin, kn, vn, ci, ck, cv):
    nb, rows, _ = q.shape
    dec = rows // DSA_HEADS
    assert dec & (dec - 1) == 0 and dec <= SUBLANES and KV_HEADS == 4
    k_keep = max(1, min(DSA_TOPK_MAX, (PAST_LEN + dec) // 4))
    seq = lambda b, pt: (b, 0, 0)
    return pl.pallas_call(
        functools.partial(_sample_dsa_kernel, k_keep),
        grid_spec=pltpu.PrefetchScalarGridSpec(
            num_scalar_prefetch=1,
            grid=(nb,),
            in_specs=[pl.BlockSpec((None, dec * IDX_HEADS, IDX_DIM), seq),
                      pl.BlockSpec((None, dec * IDX_HEADS, 1), seq),
                      pl.BlockSpec((None, rows, HEAD_DIM), seq),
                      pl.BlockSpec((None, dec, IDX_DIM), seq),
                      pl.BlockSpec((None, dec * KV_HEADS, HEAD_DIM), seq),
                      pl.BlockSpec((None, dec * KV_HEADS, HEAD_DIM), seq)]
                     + _page_specs((PAGE_SIZE, IDX_DIM), N_PAGES)
                     + _page_specs((PAGE_ROWS, HEAD_DIM), N_PAGES) + _page_specs((PAGE_ROWS, HEAD_DIM), N_PAGES),
            out_specs=pl.BlockSpec((None, rows, HEAD_DIM), seq),
            scratch_shapes=[
                pltpu.VMEM((N_PAGES + 1, SUBLANES, PAGE_SIZE), F32),
                pltpu.VMEM((N_PAGES, rows, PAGE_ROWS), F32),
                pltpu.VMEM((PAGE_SIZE, IDX_DIM), F32),
                pltpu.VMEM((LANES, HEAD_DIM), F32),
                pltpu.VMEM((LANES, HEAD_DIM), F32),
            ],
        ),
        out_shape=jax.ShapeDtypeStruct((nb, rows, HEAD_DIM), F32),
        compiler_params=_cparams(("parallel",)),
        name="dsa_sample",
    )(pt, qi, wi, q, kin, kn, vn, *([ci] * N_PAGES), *([ck] * N_PAGES), *([cv] * N_PAGES))


def _outproj_kernel(tiles_a, om_ref, od_ref, ms_ref, w_ref, xa_ref, xb_ref, h_ref, acc_ref):
    i = pl.program_id(0)
    k = pl.program_id(2)
    nk = pl.num_programs(2)
    first = i < tiles_a

    @pl.when(k == 0)
    def _():
        acc_ref[...] = jnp.zeros(acc_ref.shape, F32)

    mix = jnp.where(first, jnp.where(k < nk // 2, om_ref[...], od_ref[...]), ms_ref[...])
    acc_ref[...] += _dot(mix, w_ref[...])

    @pl.when(k == nk - 1)
    def _():
        h_ref[...] = jnp.where(first, xa_ref[...], xb_ref[...]) + acc_ref[...]


def _outproj(o_m, o_d, mix_b, w, xa, xb, tm, tn, tk):
    (na, d), nb = xa.shape, xb.shape[0]
    n = na + nb
    nk = d // tk
    half = nk // 2
    tiles_a = na // tm
    assert na % tm == 0 and nb % tm == 0 and nk % 2 == 0 and o_m.shape[1] == half * tk and d % tn == 0
    row_a = lambda i: jnp.minimum(i, tiles_a - 1)
    row_b = lambda i: jnp.maximum(i - tiles_a, 0)
    return pl.pallas_call(
        functools.partial(_outproj_kernel, tiles_a),
        grid=(n // tm, d // tn, nk),
        in_specs=[
            pl.BlockSpec((tm, tk), lambda i, c, k: (row_a(i), jnp.minimum(k, half - 1))),
            pl.BlockSpec((tm, tk), lambda i, c, k: (row_a(i), jnp.maximum(k - half, 0))),
            pl.BlockSpec((tm, tk), lambda i, c, k: (row_b(i), k)),
            pl.BlockSpec((tk, tn), lambda i, c, k: (k, c)),
            pl.BlockSpec((tm, tn), lambda i, c, k: (row_a(i), c)),
            pl.BlockSpec((tm, tn), lambda i, c, k: (row_b(i), c)),
        ],
        out_specs=pl.BlockSpec((tm, tn), lambda i, c, k: (i, c)),
        out_shape=jax.ShapeDtypeStruct((n, d), F32),
        scratch_shapes=[pltpu.VMEM((tm, tn), F32)],
        compiler_params=_cparams(("parallel", "parallel", "arbitrary")),
        name="outproj",
    )(o_m, o_d, mix_b, w, xa, xb)


def _rmsnorm_rows_kernel(x_ref, g_ref, o_ref):
    x = x_ref[...]
    y = x * lax.rsqrt(jnp.mean(x * x, axis=-1, keepdims=True) + NORM_EPS)
    o_ref[...] = (y * g_ref[...]).astype(o_ref.dtype)


def _rmsnorm_rows(x, g, out_dtype, tm):
    n, d = x.shape
    return pl.pallas_call(
        _rmsnorm_rows_kernel,
        grid=(n // tm,),
        in_specs=[pl.BlockSpec((tm, d), lambda i: (i, 0)), pl.BlockSpec((1, d), lambda i: (0, 0))],
        out_specs=pl.BlockSpec((tm, d), lambda i: (i, 0)),
        out_shape=jax.ShapeDtypeStruct((n, d), out_dtype),
        compiler_params=_cparams(("parallel",)),
        name="rmsnorm_ffn",
    )(x, g.reshape(1, d))


def _top_rows(x, k):
    r = x.shape[0]
    ridx = lax.broadcasted_iota(jnp.int32, x.shape, 0)
    outs = []
    for _ in range(k):
        mx = jnp.max(x, axis=0, keepdims=True)
        idx = jnp.min(jnp.where(x == mx, ridx, r), axis=0, keepdims=True)
        outs.append(mx)
        x = jnp.where(ridx == idx, -1.0, x)
    return jnp.concatenate(outs, axis=0)


def _peer_select_kernel(h2_ref, wq_ref, sk_ref, e_ref, thr_ref, iz_ref, acc_ref):
    k = pl.program_id(1)

    @pl.when(k == 0)
    def _():
        acc_ref[...] = jnp.zeros(acc_ref.shape, F32)

    acc_ref[...] += _dot(h2_ref[...], wq_ref[...])

    @pl.when(k == pl.num_programs(1) - 1)
    def _():
        half = PEER_DK // 2
        hk = PEER_TOPK // 2
        for h in range(PEER_HEADS):
            tops = []
            for p in range(2):
                c0 = (h * 2 + p) * half
                q = acc_ref[:, c0:c0 + half].astype(BF16)
                st = _dot_nt(sk_ref[h, p], q)
                e = jnp.exp(st - jnp.max(st, axis=0, keepdims=True))
                e_ref[h, p] = e
                tops.append(_top_rows(e, PEER_TOPK))
            a, b = tops
            cand = [a[x:x + 1, :] * b[0:hk, :] for x in range(hk)]
            cand += [a[0:1, :] * b[hk:, :], a[hk:, :] * b[0:1, :]]
            top = _top_rows(jnp.concatenate(cand, axis=0), PEER_TOPK)
            thr_ref[h:h + 1, :] = top[PEER_TOPK - 1:PEER_TOPK, :]
            iz_ref[h:h + 1, :] = 1.0 / jnp.sum(top, axis=0, keepdims=True)


def _peer_select(h2, wq, sk, tm, tk):
    n, d = h2.shape
    qw = wq.shape[1]
    return pl.pallas_call(
        _peer_select_kernel,
        grid=(n // tm, d // tk),
        in_specs=[
            pl.BlockSpec((tm, tk), lambda i, k: (i, k)),
            pl.BlockSpec((tk, qw), lambda i, k: (k, 0)),
            pl.BlockSpec(sk.shape, lambda i, k: (0, 0, 0, 0)),
        ],
        out_specs=[
            pl.BlockSpec((PEER_HEADS, 2, PEER_N_KEYS, tm), lambda i, k: (0, 0, 0, i)),
            pl.BlockSpec((PEER_HEADS, tm), lambda i, k: (0, i)),
            pl.BlockSpec((PEER_HEADS, tm), lambda i, k: (0, i)),
        ],
        out_shape=[
            jax.ShapeDtypeStruct((PEER_HEADS, 2, PEER_N_KEYS, n), F32),
            jax.ShapeDtypeStruct((PEER_HEADS, n), F32),
            jax.ShapeDtypeStruct((PEER_HEADS, n), F32),
        ],
        scratch_shapes=[pltpu.VMEM((tm, qw), F32)],
        compiler_params=_cparams(("parallel", "arbitrary")),
        name="peer_select",
    )(h2, wq, sk)


PEER_SUB = 256


def _peer_dense_kernel(h2_ref, u_ref, v_ref, e_ref, thr_ref, iz_ref, o_ref):
    e = pl.program_id(1)
    te = u_ref.shape[0]
    tm = h2_ref.shape[0]

    @pl.when(e == 0)
    def _():
        o_ref[...] = jnp.zeros(o_ref.shape, F32)

    h2 = h2_ref[...]
    total = None
    for sub in range(te // PEER_SUB):
        r0 = sub * PEER_SUB
        at = _dot_nt(u_ref[r0:r0 + PEER_SUB, :], h2)
        act = 0.5 * at * (1.0 + lax.erf(at * (2.0 ** -0.5)))
        parts = []
        for c in range(PEER_SUB // PEER_N_KEYS):
            i = (e * te + r0) // PEER_N_KEYS + c
            g = jnp.zeros((PEER_N_KEYS, tm), F32)
            for h in range(PEER_HEADS):
                p = e_ref[h, 0, pl.ds(i, 1), :] * e_ref[h, 1]
                g = g + jnp.where(p >= thr_ref[h:h + 1, :], p, 0.0) * iz_ref[h:h + 1, :]
            parts.append(g)
        ga = (jnp.concatenate(parts, axis=0) * act).astype(BF16)
        d = lax.dot_general(ga, v_ref[r0:r0 + PEER_SUB, :], (((0,), (0,)), ((), ())), preferred_element_type=F32)
        total = d if total is None else total + d
    o_ref[...] += total


def _peer_dense(h2, u, v, e, thr, iz, tm, te):
    n, d = h2.shape
    ne = u.shape[0]
    return pl.pallas_call(
        _peer_dense_kernel,
        grid=(n // tm, ne // te),
        in_specs=[
            pl.BlockSpec((tm, d), lambda i, k: (i, 0)),
            pl.BlockSpec((te, d), lambda i, k: (k, 0)),
            pl.BlockSpec((te, d), lambda i, k: (k, 0)),
            pl.BlockSpec((PEER_HEADS, 2, PEER_N_KEYS, tm), lambda i, k: (0, 0, 0, i)),
            pl.BlockSpec((PEER_HEADS, tm), lambda i, k: (0, i)),
            pl.BlockSpec((PEER_HEADS, tm), lambda i, k: (0, i)),
        ],
        out_specs=pl.BlockSpec((tm, d), lambda i, k: (i, 0)),
        out_shape=jax.ShapeDtypeStruct((n, d), F32),
        compiler_params=_cparams(("parallel", "arbitrary")),
        name="peer_dense",
    )(h2, u, v, e, thr, iz)


def _final_kernel(h_ref, f_ref, g_ref, o_ref):
    x = h_ref[...] + f_ref[...]
    y = x * lax.rsqrt(jnp.mean(x * x, axis=-1, keepdims=True) + NORM_EPS)
    o_ref[...] = y * g_ref[...]


def _final(h, f, g, row0, rows, tm):
    d = h.shape[1]
    assert row0 % tm == 0 and rows % tm == 0
    src = pl.BlockSpec((tm, d), lambda i: (i + row0 // tm, 0))
    return pl.pallas_call(
        _final_kernel,
        grid=(rows // tm,),
        in_specs=[src, src, pl.BlockSpec((1, d), lambda i: (0, 0))],
        out_specs=pl.BlockSpec((tm, d), lambda i: (i, 0)),
        out_shape=jax.ShapeDtypeStruct((rows, d), F32),
        compiler_params=_cparams(("parallel",)),
        name="final_norm",
    )(h, f, g.reshape(1, d))


def _heads_to_rows(q, nb, dec):
    return q.reshape(nb, dec, MOBA_HEADS, HEAD_DIM).transpose(0, 2, 1, 3).reshape(nb, MOBA_HEADS * dec, HEAD_DIM)


def _rows_to_heads(o, nb, dec):
    return o.reshape(nb, MOBA_HEADS, dec, HEAD_DIM).transpose(0, 2, 1, 3).reshape(nb * dec, MOBA_HEADS * HEAD_DIM)


def _sample_mix(proj_s, page_table, c_mk, c_mv, c_dk, c_dv, c_ik, nb, dec):
    ps = proj_s.reshape(nb, dec, PROJ_PAD)
    seg = lambda c, w: ps[:, :, c:c + w]
    kv_rows = lambda c: seg(c, KVW).reshape(nb, dec * KV_HEADS, HEAD_DIM)
    s_m = _sample_moba(page_table, _heads_to_rows(seg(C_QM, MOBA_HEADS * HEAD_DIM), nb, dec),
                       kv_rows(C_KM), kv_rows(C_VM), c_mk, c_mv)
    qi = seg(C_QI, IDX_HEADS * IDX_DIM).reshape(nb, dec * IDX_HEADS, IDX_DIM)
    wi = seg(C_WI, IDX_HEADS).reshape(nb, dec * IDX_HEADS, 1)
    s_d = _sample_dsa(page_table, qi, wi, _heads_to_rows(seg(C_QD, DSA_HEADS * HEAD_DIM), nb, dec),
                      seg(C_KI, IDX_DIM), kv_rows(C_KD), kv_rows(C_VD), c_ik, c_dk, c_dv)
    return jnp.concatenate([_rows_to_heads(s_m, nb, dec), _rows_to_heads(s_d, nb, dec)], axis=-1)


def _prompt_mix(proj, t):
    kmean = _kmean(proj, t)
    kmean = jnp.pad(kmean, ((0, LANES - kmean.shape[0]), (0, 0)))
    o_m = _prompt_attn(True, proj, kmean, t)
    o_d = _prompt_attn(False, proj, _dsa_select(proj, t), t)
    return o_m, o_d


def kernel(x_prompt, x_sample, cache_moba_k, cache_moba_v, cache_dsa_k, cache_dsa_v, cache_idx_k, page_table,
           norm_mix, w_in, w_out, norm_ffn, peer_w_q, peer_subkeys, peer_u, peer_v, norm_final):
    depth = w_in.shape[0]
    assert depth == 1 and x_prompt.shape[0] == 1
    assert PAST_LEN % MOBA_BLOCK == 0 and page_table.shape[1] == N_PAGES
    t = x_prompt.shape[1]
    nb, dec, d = x_sample.shape
    n_tok = t + nb * dec
    tm = 512
    assert n_tok % tm == 0 and t % tm == 0

    x_p = x_prompt.reshape(t, d)
    x_s = x_sample.reshape(nb * dec, d)
    pos = jnp.concatenate([jnp.arange(t, dtype=jnp.int32),
                           PAST_LEN + jnp.tile(jnp.arange(dec, dtype=jnp.int32), nb)])
    cos, sin = _rope_tables(pos)

    n_pool = cache_moba_k.shape[1]
    flat = lambda c: c[0].reshape(n_pool, PAGE_ROWS, HEAD_DIM)

    u = _rmsnorm(x_p, x_s, norm_mix[0], BF16, tm)
    tm_proj = n_tok // 8 if (n_tok // 8) % 16 == 0 and n_tok % 8 == 0 else tm
    proj = _project(u, w_in[0].astype(BF16), cos, sin, tm_proj)
    o_m, o_d = _prompt_mix(proj, t)
    mix_s = _sample_mix(proj[t:], page_table, flat(cache_moba_k), flat(cache_moba_v), flat(cache_dsa_k),
                        flat(cache_dsa_v), cache_idx_k[0], nb, dec)
    h = _outproj(o_m, o_d, mix_s.astype(BF16), w_out[0].astype(BF16), x_p, x_s, tm, d // 2, 512)
    h2 = _rmsnorm_rows(h, norm_ffn[0], BF16, tm)
    e, thr, iz = _peer_select(h2, peer_w_q[0].astype(BF16), peer_subkeys[0].astype(BF16), tm, 512)
    f = _peer_dense(h2, peer_u[0].astype(BF16), peer_v[0].astype(BF16), e, thr, iz, tm, 512)

    y_prompt = _final(h, f, norm_final, 0, t, 256).reshape(1, t, d)
    y_sample = _final(h, f, norm_final, t, nb * dec, 256).reshape(nb, dec, d)
    pp = proj[:t]
    ps = proj[t:]
    kv = lambda a, c: a[:, c:c + KVW]
    outs_p = [kv(pp, c).reshape(1, 1, t, KV_HEADS, HEAD_DIM) for c in (C_KM, C_VM, C_KD, C_VD)]
    outs_p.append(pp[:, C_KI:C_KI + IDX_DIM].reshape(1, 1, t, IDX_DIM))
    outs_s = [kv(ps, c).reshape(1, nb, dec, KV_HEADS, HEAD_DIM) for c in (C_KM, C_VM, C_KD, C_VD)]
    outs_s.append(ps[:, C_KI:C_KI + IDX_DIM].reshape(1, nb, dec, IDX_DIM))
    return (y_prompt, y_sample, *outs_p, *outs_s)
```

```python
import functools

import numpy as np
import jax
import jax.numpy as jnp
from jax import lax
from jax.experimental import pallas as pl
from jax.experimental.pallas import tpu as pltpu

F32 = jnp.float32
BF16 = jnp.bfloat16

D_MODEL = 4096
HEAD_DIM = 128
MOBA_HEADS = 16
DSA_HEADS = 16
KV_HEADS = 4
GROUP = 4
MOBA_BLOCK = 256
MOBA_TOPK = 3
IDX_HEADS = 32
IDX_DIM = 128
DSA_TOPK_MAX = 256
ROPE_THETA = 10000.0
ATTN_SCALE = HEAD_DIM ** -0.5
IDX_W_SCALE = (IDX_HEADS ** -0.5) * (IDX_DIM ** -0.5)
PEER_HEADS = 8
PEER_N_KEYS = 128
PEER_DK = 256
PEER_TOPK = 16
NORM_EPS = 1e-6
PAST_LEN = 2048
PAGE_SIZE = 128

LANES = 128
SUBLANES = 8
NEG = -1e30
INT_MIN = -2 ** 31
VMEM_LIMIT = 56 * 1024 * 1024

C_QI = 0
C_QM = C_QI + IDX_HEADS * IDX_DIM
C_QD = C_QM + MOBA_HEADS * HEAD_DIM
C_KM = C_QD + DSA_HEADS * HEAD_DIM
C_VM = C_KM + KV_HEADS * HEAD_DIM
C_KD = C_VM + KV_HEADS * HEAD_DIM
C_VD = C_KD + KV_HEADS * HEAD_DIM
C_KI = C_VD + KV_HEADS * HEAD_DIM
C_WI = C_KI + IDX_DIM
PROJ_TN = 512
PROJ_PAD = 10752
KVW = KV_HEADS * HEAD_DIM


def _cparams(sem):
    return pltpu.CompilerParams(dimension_semantics=sem, vmem_limit_bytes=VMEM_LIMIT)


def _dot_nt(a, b):
    return lax.dot_general(a, b, (((1,), (1,)), ((), ())), preferred_element_type=F32)


def _dot(a, b):
    return jnp.dot(a, b, preferred_element_type=F32)


def _two_source_specs(tm, d, tiles_a):
    return [pl.BlockSpec((tm, d), lambda i, *_: (jnp.minimum(i, tiles_a - 1), 0)),
            pl.BlockSpec((tm, d), lambda i, *_: (jnp.maximum(i - tiles_a, 0), 0))]


def _rmsnorm_kernel(tiles_a, xa_ref, xb_ref, g_ref, o_ref):
    x = jnp.where(pl.program_id(0) < tiles_a, xa_ref[...], xb_ref[...])
    y = x * lax.rsqrt(jnp.mean(x * x, axis=-1, keepdims=True) + NORM_EPS)
    o_ref[...] = (y * g_ref[...]).astype(o_ref.dtype)


def _rmsnorm(xa, xb, g, out_dtype, tm):
    (na, d), nb = xa.shape, xb.shape[0]
    assert na % tm == 0 and nb % tm == 0
    return pl.pallas_call(
        functools.partial(_rmsnorm_kernel, na // tm),
        grid=((na + nb) // tm,),
        in_specs=_two_source_specs(tm, d, na // tm) + [pl.BlockSpec((1, d), lambda i: (0, 0))],
        out_specs=pl.BlockSpec((tm, d), lambda i: (i, 0)),
        out_shape=jax.ShapeDtypeStruct((na + nb, d), out_dtype),
        compiler_params=_cparams(("parallel",)),
        name="rmsnorm",
    )(xa, xb, g.reshape(1, d))


def _rope_group(a, cos, sin):
    return a * cos + pltpu.roll(a, HEAD_DIM // 2, axis=1) * sin


def _proj_kernel(u_ref, w_ref, cos_ref, sin_ref, o_ref):
    j = pl.program_id(1)
    acc = _dot(u_ref[...], w_ref[...])
    plain = jnp.logical_or(j == C_VM // PROJ_TN, j == C_VD // PROJ_TN)
    last = j == C_KI // PROJ_TN
    groups = PROJ_TN // HEAD_DIM

    @pl.when(plain)
    def _():
        o_ref[...] = acc

    @pl.when(jnp.logical_not(jnp.logical_or(plain, last)))
    def _():
        cos = cos_ref[...]
        sin = sin_ref[...]
        for c in range(groups):
            o_ref[:, c * HEAD_DIM:(c + 1) * HEAD_DIM] = _rope_group(acc[:, c * HEAD_DIM:(c + 1) * HEAD_DIM], cos, sin)

    @pl.when(last)
    def _():
        o_ref[:, 0:HEAD_DIM] = _rope_group(acc[:, 0:HEAD_DIM], cos_ref[...], sin_ref[...])
        o_ref[:, HEAD_DIM:2 * HEAD_DIM] = acc[:, HEAD_DIM:2 * HEAD_DIM] * IDX_W_SCALE
        o_ref[:, 2 * HEAD_DIM:] = acc[:, 2 * HEAD_DIM:]


def _w_in_tile_order():
    model = dict(q_m=0, k_m=MOBA_HEADS * HEAD_DIM)
    model["v_m"] = model["k_m"] + KVW
    model["q_d"] = model["v_m"] + KVW
    model["k_d"] = model["q_d"] + DSA_HEADS * HEAD_DIM
    model["v_d"] = model["k_d"] + KVW
    model["q_i"] = model["v_d"] + KVW
    model["k_i"] = model["q_i"] + IDX_HEADS * IDX_DIM
    ours = (("q_i", C_QI, C_QM), ("q_m", C_QM, C_QD), ("q_d", C_QD, C_KM), ("k_m", C_KM, C_VM), ("v_m", C_VM, C_KD),
            ("k_d", C_KD, C_VD), ("v_d", C_VD, C_KI), ("k_i", C_KI, PROJ_PAD))
    order = []
    for name, lo, hi in ours:
        assert lo % PROJ_TN == 0 and model[name] % PROJ_TN == 0
        order += [model[name] // PROJ_TN + x for x in range((hi - lo) // PROJ_TN)]
    return np.asarray(order, np.int32)


def _proj_kernel_mapped(src_ref, u_ref, w_ref, cos_ref, sin_ref, o_ref):
    _proj_kernel(u_ref, w_ref, cos_ref, sin_ref, o_ref)


def _project(u, w, cos, sin, tm):
    n, d = u.shape
    return pl.pallas_call(
        _proj_kernel_mapped,
        grid_spec=pltpu.PrefetchScalarGridSpec(
            num_scalar_prefetch=1,
            grid=(n // tm, PROJ_PAD // PROJ_TN),
            in_specs=[
                pl.BlockSpec((tm, d), lambda i, j, src: (i, 0)),
                pl.BlockSpec((d, PROJ_TN), lambda i, j, src: (0, src[j])),
                pl.BlockSpec((tm, HEAD_DIM), lambda i, j, src: (i, 0)),
                pl.BlockSpec((tm, HEAD_DIM), lambda i, j, src: (i, 0)),
            ],
            out_specs=pl.BlockSpec((tm, PROJ_TN), lambda i, j, src: (i, j)),
        ),
        out_shape=jax.ShapeDtypeStruct((n, PROJ_PAD), F32),
        compiler_params=_cparams(("parallel", "arbitrary")),
        name="proj_rope",
    )(jnp.asarray(_w_in_tile_order()), u, w, cos, sin)


def _rope_tables(pos):
    half = HEAD_DIM // 2
    inv = 1.0 / (ROPE_THETA ** (jnp.arange(half, dtype=F32) * (2.0 / HEAD_DIM)))
    ang = pos.astype(F32)[:, None] * inv[None, :]
    c, s = jnp.cos(ang), jnp.sin(ang)
    return jnp.concatenate([c, c], axis=-1), jnp.concatenate([-s, s], axis=-1)


def _kmean_kernel(k_ref, o_ref):
    nb = o_ref.shape[0]
    k = k_ref[...].reshape(nb, MOBA_BLOCK, KVW)
    o_ref[...] = jnp.sum(k, axis=1) * (1.0 / MOBA_BLOCK)


def _kmean(proj, t):
    nb = t // MOBA_BLOCK
    return pl.pallas_call(
        _kmean_kernel,
        grid=(1,),
        in_specs=[pl.BlockSpec((t, KVW), lambda i: (0, C_KM // KVW))],
        out_specs=pl.BlockSpec((nb, KVW), lambda i: (0, 0)),
        out_shape=jax.ShapeDtypeStruct((nb, KVW), F32),
        compiler_params=_cparams(("arbitrary",)),
        name="moba_kmean",
    )(proj)


def _top_block_ids(gate, lane, n_sel):
    ids = []
    for _ in range(n_sel):
        mx = jnp.max(gate, axis=-1, keepdims=True)
        idx = jnp.min(jnp.where(gate == mx, lane, LANES), axis=-1, keepdims=True)
        ids.append(jnp.where(mx > -jnp.inf, idx, -1))
        gate = jnp.where(lane == idx, -jnp.inf, gate)
    return ids


def _sortable_key(x):
    bits = pltpu.bitcast(x + 0.0, jnp.int32)
    return bits ^ ((bits >> 31) & 0x7FFFFFFF)


def _kth_largest_key(count_ge, shape, k, two_bits=False):
    t = jnp.broadcast_to(jnp.where(count_ge(jnp.zeros(shape, jnp.int32)) >= k, 0, INT_MIN).astype(jnp.int32), shape)

    def one_bit(bit, t):
        cand = t + (jnp.int32(1) << bit)
        return jnp.where(count_ge(cand) >= k, cand, t)

    if not two_bits:
        return lax.fori_loop(0, 31, lambda it, t: one_bit(30 - it, t), t)

    def pair(it, t):
        step = jnp.int32(1) << (29 - 2 * it)
        c1, c2, c3 = t + step, t + 2 * step, t + 3 * step
        n1, n2, n3 = count_ge(c1), count_ge(c2), count_ge(c3)
        return jnp.where(n3 >= k, c3, jnp.where(n2 >= k, c2, jnp.where(n1 >= k, c1, t)))

    return one_bit(0, lax.fori_loop(0, 15, pair, t))


PTQ = 256
PTK = 512


def _prompt_attn_kernel(moba, q_ref, k_ref, v_ref, aux_ref, o_ref, qs_ref, m_ref, acc_ref, sel_ref):
    i = pl.program_id(0)
    j = pl.program_id(1)
    rows = GROUP * PTQ
    last_j = (i * PTQ + PTQ - 1) // PTK
    chunks = PTK // LANES

    @pl.when(j == 0)
    def _():
        for n in range(KV_HEADS):
            for g in range(GROUP):
                h = n * GROUP + g
                qs_ref[n, g * PTQ:(g + 1) * PTQ, :] = q_ref[:, h * HEAD_DIM:(h + 1) * HEAD_DIM].astype(BF16)
        m_ref[...] = jnp.full(m_ref.shape, NEG, F32)
        acc_ref[...] = jnp.zeros(acc_ref.shape, F32)
        if moba:
            blk = lax.broadcasted_iota(jnp.int32, (LANES, rows), 0)
            for n in range(KV_HEADS):
                km = aux_ref[:, n * HEAD_DIM:(n + 1) * HEAD_DIM].astype(BF16)
                gate = jnp.where(blk < i, _dot_nt(km, qs_ref[n]), -jnp.inf)
                bits = jnp.zeros((1, rows), jnp.int32)
                for _ in range(MOBA_TOPK):
                    mx = jnp.max(gate, axis=0, keepdims=True)
                    idx = jnp.min(jnp.where(gate == mx, blk, LANES), axis=0, keepdims=True)
                    bits = bits | jnp.where(mx > -jnp.inf, jnp.int32(1) << jnp.minimum(idx, 31), 0)
                    gate = jnp.where(blk == idx, -jnp.inf, gate)
                lo = jnp.broadcast_to((bits & 0xFFFF).astype(F32), (LANES, rows)).T
                hi = jnp.broadcast_to((bits >> 16).astype(F32), (LANES, rows)).T
                sel_ref[n] = lo.astype(jnp.int32) | (hi.astype(jnp.int32) << 16)

    def step(diagonal):
        k = k_ref[...].astype(BF16)
        v = v_ref[...].astype(BF16)
        ones = jnp.ones((PTK, HEAD_DIM), BF16)
        if moba and diagonal:
            lane = lax.broadcasted_iota(jnp.int32, (rows, LANES), 1)
            t_in = lax.broadcasted_iota(jnp.int32, (rows, LANES), 0) & (PTQ - 1)
        if not moba:
            bias = jnp.concatenate([aux_ref[...].T] * GROUP, axis=0)
        for n in range(KV_HEADS):
            s = _dot_nt(qs_ref[n], k[:, n * HEAD_DIM:(n + 1) * HEAD_DIM]) * ATTN_SCALE
            if moba and diagonal:
                bits = sel_ref[n]
                limits = []
                for b in range(PTK // MOBA_BLOCK):
                    kb = j * (PTK // MOBA_BLOCK) + b
                    picked = (bits >> jnp.minimum(kb, 31)) & 1
                    limits.append(jnp.where(kb == i, t_in, jnp.where(kb < i, picked * MOBA_BLOCK, 0) - 1))
                parts = []
                for c in range(chunks):
                    col = lane + (c * LANES) % MOBA_BLOCK
                    parts.append(jnp.where(col <= limits[(c * LANES) // MOBA_BLOCK],
                                           s[:, c * LANES:(c + 1) * LANES], NEG))
            elif moba:
                bits = sel_ref[n]
                row_bias = [jnp.where(((bits >> (j * (PTK // MOBA_BLOCK) + b)) & 1) == 1, 0.0, NEG)
                            for b in range(PTK // MOBA_BLOCK)]
                parts = [s[:, c * LANES:(c + 1) * LANES] + row_bias[(c * LANES) // MOBA_BLOCK] for c in range(chunks)]
            else:
                s = s + bias
                parts = [s[:, c * LANES:(c + 1) * LANES] for c in range(chunks)]
            mx = parts[0]
            for c in range(1, chunks):
                mx = jnp.maximum(mx, parts[c])
            m_prev = m_ref[n]
            m_new = jnp.maximum(m_prev, jnp.max(mx, axis=-1, keepdims=True))
            alpha = jnp.exp(m_prev - m_new)
            p = jnp.concatenate([jnp.exp(x - m_new) for x in parts], axis=1).astype(BF16)
            v_ones = jnp.concatenate([v[:, n * HEAD_DIM:(n + 1) * HEAD_DIM], ones], axis=1)
            acc_ref[n] = jnp.concatenate([alpha, alpha], axis=1) * acc_ref[n] + _dot(p, v_ones)
            m_ref[n] = m_new

    if moba:
        pl.when(j < last_j)(functools.partial(step, False))
        pl.when(j == last_j)(functools.partial(step, True))
    else:
        pl.when(j <= last_j)(functools.partial(step, False))

    @pl.when(j == last_j)
    def _():
        for n in range(KV_HEADS):
            for g in range(GROUP):
                h = n * GROUP + g
                r = slice(g * PTQ, (g + 1) * PTQ)
                o_ref[:, h * HEAD_DIM:(h + 1) * HEAD_DIM] = (acc_ref[n, r, 0:HEAD_DIM]
                                                             / acc_ref[n, r, HEAD_DIM:]).astype(o_ref.dtype)


def _prompt_attn(moba, proj, aux, t):
    nq, nk = t // PTQ, t // PTK
    assert t // MOBA_BLOCK <= 32
    qw = MOBA_HEADS * HEAD_DIM
    c_q, c_k, c_v = (C_QM, C_KM, C_VM) if moba else (C_QD, C_KD, C_VD)
    jj = lambda i, j: jnp.minimum(j, (i * PTQ + PTQ - 1) // PTK)
    if moba:
        aux_spec = pl.BlockSpec((LANES, KVW), lambda i, j: (0, 0))
    else:
        aux_spec = pl.BlockSpec((PTK, PTQ), lambda i, j: (jj(i, j), i))
    rows = GROUP * PTQ
    return pl.pallas_call(
        functools.partial(_prompt_attn_kernel, moba),
        grid=(nq, nk),
        in_specs=[
            pl.BlockSpec((PTQ, qw), lambda i, j: (i, c_q // qw)),
            pl.BlockSpec((PTK, KVW), lambda i, j: (jj(i, j), c_k // KVW)),
            pl.BlockSpec((PTK, KVW), lambda i, j: (jj(i, j), c_v // KVW)),
            aux_spec,
        ],
        out_specs=pl.BlockSpec((PTQ, qw), lambda i, j: (i, 0)),
        out_shape=jax.ShapeDtypeStruct((t, qw), BF16),
        scratch_shapes=[
            pltpu.VMEM((KV_HEADS, rows, HEAD_DIM), BF16),
            pltpu.VMEM((KV_HEADS, rows, LANES), F32),
            pltpu.VMEM((KV_HEADS, rows, HEAD_DIM + LANES), F32),
            pltpu.VMEM((KV_HEADS, rows, LANES), jnp.int32),
        ],
        compiler_params=_cparams(("parallel", "arbitrary")),
        name="moba_prompt" if moba else "dsa_prompt",
    )(proj, proj, proj, aux)


DTK = 512


COUNT_ROWS = 32


def _dsa_select_kernel(k_keep, qi_ref, ki_ref, wt_ref, bias_ref, qs_ref, key_ref):
    i = pl.program_id(0)
    j = pl.program_id(1)
    nj = pl.num_programs(1)
    length = key_ref.shape[0]

    @pl.when(j == 0)
    def _():
        qs_ref[...] = qi_ref[...].astype(BF16)
        key_ref[...] = jnp.full(key_ref.shape, INT_MIN, jnp.int32)

    @pl.when(j * DTK <= i * PTQ + PTQ - 1)
    def _():
        kb = ki_ref[...].astype(BF16)
        acc = jnp.zeros((DTK, PTQ), F32)
        for h in range(IDX_HEADS):
            s = _dot_nt(kb, qs_ref[:, h * IDX_DIM:(h + 1) * IDX_DIM])
            acc = acc + wt_ref[h:h + 1, :] * jnp.maximum(s, 0.0)
        key_pos = j * DTK + lax.broadcasted_iota(jnp.int32, (DTK, PTQ), 0)
        q_pos = i * PTQ + lax.broadcasted_iota(jnp.int32, (DTK, PTQ), 1)
        x = jnp.where(key_pos <= q_pos, acc, -jnp.inf)
        key_ref[pl.ds(pl.multiple_of(j * DTK, DTK), DTK), :] = _sortable_key(x)

    @pl.when(j == nj - 1)
    def _():
        n_tiles = (i * PTQ + PTQ - 1) // DTK + 1

        def count_ge(cand):
            def tile(c, acc):
                keys = key_ref[pl.ds(pl.multiple_of(c * DTK, DTK), DTK), :]
                hit = jnp.where(keys.reshape(DTK // COUNT_ROWS, COUNT_ROWS, PTQ) >= cand, 1, 0)
                return acc + jnp.sum(hit, axis=0)

            acc = lax.fori_loop(0, n_tiles, tile, jnp.zeros((COUNT_ROWS, PTQ), jnp.int32))
            return jnp.sum(acc, axis=0, keepdims=True)

        thr = _kth_largest_key(count_ge, (1, PTQ), k_keep)
        key_pos = lax.broadcasted_iota(jnp.int32, (length, PTQ), 0)
        q_pos = i * PTQ + lax.broadcasted_iota(jnp.int32, (length, PTQ), 1)
        ok = jnp.logical_and(key_ref[...] >= thr, key_pos <= q_pos)
        bias_ref[...] = jnp.where(ok, 0.0, NEG)


def _dsa_select(proj, t):
    nq, nj = t // PTQ, t // DTK
    k_keep = max(1, min(DSA_TOPK_MAX, t // 4))
    qw = IDX_HEADS * IDX_DIM
    last_j = lambda i: (i * PTQ + PTQ - 1) // DTK
    w_t = proj[:t, C_WI:C_WI + IDX_HEADS].T
    return pl.pallas_call(
        functools.partial(_dsa_select_kernel, k_keep),
        grid=(nq, nj),
        in_specs=[
            pl.BlockSpec((PTQ, qw), lambda i, j: (i, C_QI // qw)),
            pl.BlockSpec((DTK, IDX_DIM), lambda i, j: (jnp.minimum(j, last_j(i)), C_KI // IDX_DIM)),
            pl.BlockSpec((IDX_HEADS, PTQ), lambda i, j: (0, i)),
        ],
        out_specs=pl.BlockSpec((t, PTQ), lambda i, j: (0, i)),
        out_shape=jax.ShapeDtypeStruct((t, t), F32),
        scratch_shapes=[pltpu.VMEM((PTQ, qw), BF16), pltpu.VMEM((t, PTQ), jnp.int32)],
        compiler_params=_cparams(("parallel", "arbitrary")),
        name="dsa_select",
    )(proj, proj, w_t)


N_PAGES = PAST_LEN // PAGE_SIZE
PAGES_PER_BLOCK = MOBA_BLOCK // PAGE_SIZE
PAGE_ROWS = PAGE_SIZE * KV_HEADS


def _head_match(rows, cols, dec):
    r = lax.broadcasted_iota(jnp.int32, (rows, cols), 0)
    c = lax.broadcasted_iota(jnp.int32, (rows, cols), 1)
    shift = (GROUP * dec).bit_length() - 1
    return (r >> shift) == (c & (KV_HEADS - 1))


def _stage_new(dst_ref, src_ref):
    dst_ref[...] = jnp.zeros(dst_ref.shape, F32)
    dst_ref[0:src_ref.shape[0], :] = src_ref[...]


def _softmax_over_pages(s_ref, s_new, v_refs, nv_ref, o_ref):
    m = jnp.max(s_new, axis=-1, keepdims=True)
    for p in range(N_PAGES):
        m = jnp.maximum(m, jnp.max(s_ref[p], axis=-1, keepdims=True))
    pn = jnp.exp(s_new - m)
    l = jnp.sum(pn, axis=-1, keepdims=True)
    acc = _dot(pn.astype(BF16), nv_ref[...].astype(BF16))
    for p in range(N_PAGES):
        pr = jnp.exp(s_ref[p] - m)
        l = l + jnp.sum(pr, axis=-1, keepdims=True)
        acc = acc + _dot(pr.astype(BF16), v_refs[p][...].astype(BF16))
    o_ref[...] = acc / l


def _sample_moba_kernel(pt_ref, q_ref, kn_ref, vn_ref, *rest):
    k_refs, v_refs = rest[:N_PAGES], rest[N_PAGES:2 * N_PAGES]
    o_ref, s_ref, nk_ref, nv_ref = rest[2 * N_PAGES:]
    rows = q_ref.shape[0]
    dec = rows // MOBA_HEADS
    n_past_blocks = PAST_LEN // MOBA_BLOCK
    q = q_ref[...].astype(BF16)
    match = _head_match(rows, PAGE_ROWS, dec)
    lane = lax.broadcasted_iota(jnp.int32, (rows, LANES), 1)

    for p in range(N_PAGES):
        s_ref[p] = _dot_nt(q, k_refs[p][...].astype(BF16))

    gate = jnp.full((rows, LANES), -jnp.inf, F32)
    for b in range(n_past_blocks):
        tot = s_ref[PAGES_PER_BLOCK * b]
        for c in range(1, PAGES_PER_BLOCK):
            tot = tot + s_ref[PAGES_PER_BLOCK * b + c]
        g = jnp.sum(jnp.where(match, tot, 0.0), axis=-1, keepdims=True) * (1.0 / MOBA_BLOCK)
        gate = jnp.where(lane == b, g, gate)
    ids = _top_block_ids(gate, lane, MOBA_TOPK)

    for b in range(n_past_blocks):
        picked = ids[0] == b
        for idx in ids[1:]:
            picked = jnp.logical_or(picked, idx == b)
        row_bias = jnp.where(picked, 0.0, NEG)
        for c in range(PAGES_PER_BLOCK):
            p = PAGES_PER_BLOCK * b + c
            s_ref[p] = jnp.where(match, s_ref[p] * ATTN_SCALE + row_bias, NEG)

    _stage_new(nk_ref, kn_ref)
    _stage_new(nv_ref, vn_ref)
    rn = lax.broadcasted_iota(jnp.int32, (rows, LANES), 0)
    ok = jnp.logical_and(_head_match(rows, LANES, dec), (lane >> 2) <= (rn & (dec - 1)))
    s_new = jnp.where(ok, _dot_nt(q, nk_ref[...].astype(BF16)) * ATTN_SCALE, NEG)
    _softmax_over_pages(s_ref, s_new, v_refs, nv_ref, o_ref)


def _page_specs(shape, n):
    return [pl.BlockSpec((None,) + shape, functools.partial(lambda b, pt, p: (pt[b, p], 0, 0), p=p)) for p in range(n)]


def _sample_moba(pt, q, kn, vn, ck, cv):
    nb, rows, _ = q.shape
    dec = rows // MOBA_HEADS
    assert dec & (dec - 1) == 0 and dec * KV_HEADS <= LANES and KV_HEADS == 4
    seq = lambda b, pt: (b, 0, 0)
    return pl.pallas_call(
        _sample_moba_kernel,
        grid_spec=pltpu.PrefetchScalarGridSpec(
            num_scalar_prefetch=1,
            grid=(nb,),
            in_specs=[pl.BlockSpec((None, rows, HEAD_DIM), seq),
                      pl.BlockSpec((None, dec * KV_HEADS, HEAD_DIM), seq),
                      pl.BlockSpec((None, dec * KV_HEADS, HEAD_DIM), seq)]
                     + _page_specs((PAGE_ROWS, HEAD_DIM), N_PAGES) + _page_specs((PAGE_ROWS, HEAD_DIM), N_PAGES),
            out_specs=pl.BlockSpec((None, rows, HEAD_DIM), seq),
            scratch_shapes=[
                pltpu.VMEM((N_PAGES, rows, PAGE_ROWS), F32),
                pltpu.VMEM((LANES, HEAD_DIM), F32),
                pltpu.VMEM((LANES, HEAD_DIM), F32),
            ],
        ),
        out_shape=jax.ShapeDtypeStruct((nb, rows, HEAD_DIM), F32),
        compiler_params=_cparams(("parallel",)),
        name="moba_sample",
    )(pt, q, kn, vn, *([ck] * N_PAGES), *([cv] * N_PAGES))


def _sample_dsa_kernel(k_keep, pt_ref, qi_ref, wi_ref, q_ref, kin_ref, kn_ref, vn_ref, *rest):
    i_refs, k_refs, v_refs = rest[:N_PAGES], rest[N_PAGES:2 * N_PAGES], rest[2 * N_PAGES:3 * N_PAGES]
    o_ref, idx_ref, s_ref, nki_ref, nk_ref, nv_ref = rest[3 * N_PAGES:]
    rows = q_ref.shape[0]
    dec = rows // DSA_HEADS
    qi = qi_ref[...].astype(BF16)
    w = wi_ref[...]

    def index_scores(kpage):
        sc = jnp.maximum(_dot_nt(qi, kpage.astype(BF16)), 0.0) * w
        return jnp.sum(sc.reshape(dec, IDX_HEADS, PAGE_SIZE), axis=1)

    idx_ref[...] = jnp.full(idx_ref.shape, -jnp.inf, F32)
    for p in range(N_PAGES):
        idx_ref[p, 0:dec, :] = index_scores(i_refs[p][...])
    _stage_new(nki_ref, kin_ref)
    t_in = lax.broadcasted_iota(jnp.int32, (dec, PAGE_SIZE), 0)
    col = lax.broadcasted_iota(jnp.int32, (dec, PAGE_SIZE), 1)
    idx_ref[N_PAGES, 0:dec, :] = jnp.where(col <= t_in, index_scores(nki_ref[...]), -jnp.inf)

    keys = _sortable_key(idx_ref[...])

    def count_ge(cand):
        c = jnp.sum((keys >= cand).astype(jnp.int32), axis=0)
        return jnp.sum(c, axis=-1, keepdims=True)[None]

    thr = _kth_largest_key(count_ge, (1, SUBLANES, 1), k_keep, two_bits=True)
    keep = jnp.where(jnp.logical_and(keys >= thr, idx_ref[...] > -jnp.inf), 1.0, 0.0)

    q = q_ref[...].astype(BF16)
    match = _head_match(rows, PAGE_ROWS, dec)
    er = lax.broadcasted_iota(jnp.int32, (PAGE_SIZE, PAGE_ROWS), 0)
    ec = lax.broadcasted_iota(jnp.int32, (PAGE_SIZE, PAGE_ROWS), 1)
    expand = jnp.where((ec >> 2) == er, 1.0, 0.0).astype(BF16)

    def visible(page, width):
        per_row = jnp.concatenate([keep[page, 0:dec, :]] * (rows // dec), axis=0).astype(BF16)
        return _dot(per_row, expand[:, 0:width]) > 0.5

    for p in range(N_PAGES):
        sc = _dot_nt(q, k_refs[p][...].astype(BF16)) * ATTN_SCALE
        s_ref[p] = jnp.where(jnp.logical_and(match, visible(p, PAGE_ROWS)), sc, NEG)
    _stage_new(nk_ref, kn_ref)
    _stage_new(nv_ref, vn_ref)
    ok = jnp.logical_and(_head_match(rows, LANES, dec), visible(N_PAGES, LANES))
    s_new = jnp.where(ok, _dot_nt(q, nk_ref[...].astype(BF16)) * ATTN_SCALE, NEG)
    _softmax_over_pages(s_ref, s_new, v_refs, nv_ref, o_ref)


def _sample_dsa(pt, qi, wi, q, kin, kn, vn, ci, ck, cv):
    nb, rows, _ = q.shape
    dec = rows // DSA_HEADS
    assert dec & (dec - 1) == 0 and dec <= SUBLANES and KV_HEADS == 4
    k_keep = max(1, min(DSA_TOPK_MAX, (PAST_LEN + dec) // 4))
    seq = lambda b, pt: (b, 0, 0)
    return pl.pallas_call(
        functools.partial(_sample_dsa_kernel, k_keep),
        grid_spec=pltpu.PrefetchScalarGridSpec(
            num_scalar_prefetch=1,
            grid=(nb,),
            in_specs=[pl.BlockSpec((None, dec * IDX_HEADS, IDX_DIM), seq),
                      pl.BlockSpec((None, dec * IDX_HEADS, 1), seq),
                      pl.BlockSpec((None, rows, HEAD_DIM), seq),
                      pl.BlockSpec((None, dec, IDX_DIM), seq),
                      pl.BlockSpec((None, dec * KV_HEADS, HEAD_DIM), seq),
                      pl.BlockSpec((None, dec * KV_HEADS, HEAD_DIM), seq)]
                     + _page_specs((PAGE_SIZE, IDX_DIM), N_PAGES)
                     + _page_specs((PAGE_ROWS, HEAD_DIM), N_PAGES) + _page_specs((PAGE_ROWS, HEAD_DIM), N_PAGES),
            out_specs=pl.BlockSpec((None, rows, HEAD_DIM), seq),
            scratch_shapes=[
                pltpu.VMEM((N_PAGES + 1, SUBLANES, PAGE_SIZE), F32),
                pltpu.VMEM((N_PAGES, rows, PAGE_ROWS), F32),
                pltpu.VMEM((PAGE_SIZE, IDX_DIM), F32),
                pltpu.VMEM((LANES, HEAD_DIM), F32),
                pltpu.VMEM((LANES, HEAD_DIM), F32),
            ],
        ),
        out_shape=jax.ShapeDtypeStruct((nb, rows, HEAD_DIM), F32),
        compiler_params=_cparams(("parallel",)),
        name="dsa_sample",
    )(pt, qi, wi, q, kin, kn, vn, *([ci] * N_PAGES), *([ck] * N_PAGES), *([cv] * N_PAGES))


def _outproj_kernel(tiles_a, om_ref, od_ref, ms_ref, w_ref, xa_ref, xb_ref, h_ref, acc_ref):
    i = pl.program_id(0)
    k = pl.program_id(2)
    nk = pl.num_programs(2)
    first = i < tiles_a

    @pl.when(k == 0)
    def _():
        acc_ref[...] = jnp.zeros(acc_ref.shape, F32)

    mix = jnp.where(first, jnp.where(k < nk // 2, om_ref[...], od_ref[...]), ms_ref[...])
    acc_ref[...] += _dot(mix, w_ref[...])

    @pl.when(k == nk - 1)
    def _():
        h_ref[...] = jnp.where(first, xa_ref[...], xb_ref[...]) + acc_ref[...]


def _outproj(o_m, o_d, mix_b, w, xa, xb, tm, tn, tk):
    (na, d), nb = xa.shape, xb.shape[0]
    n = na + nb
    nk = d // tk
    half = nk // 2
    tiles_a = na // tm
    assert na % tm == 0 and nb % tm == 0 and nk % 2 == 0 and o_m.shape[1] == half * tk and d % tn == 0
    row_a = lambda i: jnp.minimum(i, tiles_a - 1)
    row_b = lambda i: jnp.maximum(i - tiles_a, 0)
    col_a = lambda i, x: jnp.where(i < tiles_a, x, 0)
    col_b = lambda i, x: jnp.where(i < tiles_a, 0, x)
    return pl.pallas_call(
        functools.partial(_outproj_kernel, tiles_a),
        grid=(n // tm, d // tn, nk),
        in_specs=[
            pl.BlockSpec((tm, tk), lambda i, c, k: (row_a(i), col_a(i, jnp.minimum(k, half - 1)))),
            pl.BlockSpec((tm, tk), lambda i, c, k: (row_a(i), col_a(i, jnp.maximum(k - half, 0)))),
            pl.BlockSpec((tm, tk), lambda i, c, k: (row_b(i), col_b(i, k))),
            pl.BlockSpec((tk, tn), lambda i, c, k: (k, c)),
            pl.BlockSpec((tm, tn), lambda i, c, k: (row_a(i), col_a(i, c))),
            pl.BlockSpec((tm, tn), lambda i, c, k: (row_b(i), col_b(i, c))),
        ],
        out_specs=pl.BlockSpec((tm, tn), lambda i, c, k: (i, c)),
        out_shape=jax.ShapeDtypeStruct((n, d), F32),
        scratch_shapes=[pltpu.VMEM((tm, tn), F32)],
        compiler_params=_cparams(("parallel", "parallel", "arbitrary")),
        name="outproj",
    )(o_m, o_d, mix_b, w, xa, xb)


def _rmsnorm_rows_kernel(x_ref, g_ref, o_ref):
    x = x_ref[...]
    y = x * lax.rsqrt(jnp.mean(x * x, axis=-1, keepdims=True) + NORM_EPS)
    o_ref[...] = (y * g_ref[...]).astype(o_ref.dtype)


def _rmsnorm_rows(x, g, out_dtype, tm):
    n, d = x.shape
    return pl.pallas_call(
        _rmsnorm_rows_kernel,
        grid=(n // tm,),
        in_specs=[pl.BlockSpec((tm, d), lambda i: (i, 0)), pl.BlockSpec((1, d), lambda i: (0, 0))],
        out_specs=pl.BlockSpec((tm, d), lambda i: (i, 0)),
        out_shape=jax.ShapeDtypeStruct((n, d), out_dtype),
        compiler_params=_cparams(("parallel",)),
        name="rmsnorm_ffn",
    )(x, g.reshape(1, d))


def _top_rows(x, k):
    r = x.shape[0]
    ridx = lax.broadcasted_iota(jnp.int32, x.shape, 0)
    outs = []
    for _ in range(k):
        mx = jnp.max(x, axis=0, keepdims=True)
        idx = jnp.min(jnp.where(x == mx, ridx, r), axis=0, keepdims=True)
        outs.append(mx)
        x = jnp.where(ridx == idx, -1.0, x)
    return jnp.concatenate(outs, axis=0)


def _peer_select_kernel(h2_ref, wq_ref, sk_ref, e_ref, thr_ref, iz_ref, acc_ref):
    k = pl.program_id(1)

    @pl.when(k == 0)
    def _():
        acc_ref[...] = jnp.zeros(acc_ref.shape, F32)

    acc_ref[...] += _dot(h2_ref[...], wq_ref[...])

    @pl.when(k == pl.num_programs(1) - 1)
    def _():
        half = PEER_DK // 2
        hk = PEER_TOPK // 2
        for h in range(PEER_HEADS):
            tops = []
            for p in range(2):
                c0 = (h * 2 + p) * half
                q = acc_ref[:, c0:c0 + half].astype(BF16)
                st = _dot_nt(sk_ref[h, p], q)
                e = jnp.exp(st - jnp.max(st, axis=0, keepdims=True))
                e_ref[h, p] = e
                tops.append(_top_rows(e, PEER_TOPK))
            a, b = tops
            cand = [a[x:x + 1, :] * b[0:hk, :] for x in range(hk)]
            cand += [a[0:1, :] * b[hk:, :], a[hk:, :] * b[0:1, :]]
            top = _top_rows(jnp.concatenate(cand, axis=0), PEER_TOPK)
            thr_ref[h:h + 1, :] = top[PEER_TOPK - 1:PEER_TOPK, :]
            iz_ref[h:h + 1, :] = 1.0 / jnp.sum(top, axis=0, keepdims=True)


def _peer_select(h2, wq, sk, tm, tk):
    n, d = h2.shape
    qw = wq.shape[1]
    return pl.pallas_call(
        _peer_select_kernel,
        grid=(n // tm, d // tk),
        in_specs=[
            pl.BlockSpec((tm, tk), lambda i, k: (i, k)),
            pl.BlockSpec((tk, qw), lambda i, k: (k, 0)),
            pl.BlockSpec(sk.shape, lambda i, k: (0, 0, 0, 0)),
        ],
        out_specs=[
            pl.BlockSpec((PEER_HEADS, 2, PEER_N_KEYS, tm), lambda i, k: (0, 0, 0, i)),
            pl.BlockSpec((PEER_HEADS, tm), lambda i, k: (0, i)),
            pl.BlockSpec((PEER_HEADS, tm), lambda i, k: (0, i)),
        ],
        out_shape=[
            jax.ShapeDtypeStruct((PEER_HEADS, 2, PEER_N_KEYS, n), F32),
            jax.ShapeDtypeStruct((PEER_HEADS, n), F32),
            jax.ShapeDtypeStruct((PEER_HEADS, n), F32),
        ],
        scratch_shapes=[pltpu.VMEM((tm, qw), F32)],
        compiler_params=_cparams(("parallel", "arbitrary")),
        name="peer_select",
    )(h2, wq, sk)


PEER_SUB = 256


def _peer_dense_kernel(h2_ref, u_ref, v_ref, e_ref, thr_ref, iz_ref, o_ref):
    e = pl.program_id(1)
    te = u_ref.shape[0]
    tm = h2_ref.shape[0]

    @pl.when(e == 0)
    def _():
        o_ref[...] = jnp.zeros(o_ref.shape, F32)

    h2 = h2_ref[...]
    total = None
    for sub in range(te // PEER_SUB):
        r0 = sub * PEER_SUB
        at = _dot_nt(u_ref[r0:r0 + PEER_SUB, :], h2)
        act = 0.5 * at * (1.0 + lax.erf(at * (2.0 ** -0.5)))
        parts = []
        for c in range(PEER_SUB // PEER_N_KEYS):
            i = (e * te + r0) // PEER_N_KEYS + c
            g = jnp.zeros((PEER_N_KEYS, tm), F32)
            for h in range(PEER_HEADS):
                p = e_ref[h, 0, pl.ds(i, 1), :] * e_ref[h, 1]
                g = g + jnp.where(p >= thr_ref[h:h + 1, :], p, 0.0) * iz_ref[h:h + 1, :]
            parts.append(g)
        ga = (jnp.concatenate(parts, axis=0) * act).astype(BF16)
        d = lax.dot_general(ga, v_ref[r0:r0 + PEER_SUB, :], (((0,), (0,)), ((), ())), preferred_element_type=F32)
        total = d if total is None else total + d
    o_ref[...] += total


def _peer_dense(h2, u, v, e, thr, iz, tm, te):
    n, d = h2.shape
    ne = u.shape[0]
    return pl.pallas_call(
        _peer_dense_kernel,
        grid=(n // tm, ne // te),
        in_specs=[
            pl.BlockSpec((tm, d), lambda i, k: (i, 0)),
            pl.BlockSpec((te, d), lambda i, k: (k, 0)),
            pl.BlockSpec((te, d), lambda i, k: (k, 0)),
            pl.BlockSpec((PEER_HEADS, 2, PEER_N_KEYS, tm), lambda i, k: (0, 0, 0, i)),
            pl.BlockSpec((PEER_HEADS, tm), lambda i, k: (0, i)),
            pl.BlockSpec((PEER_HEADS, tm), lambda i, k: (0, i)),
        ],
        out_specs=pl.BlockSpec((tm, d), lambda i, k: (i, 0)),
        out_shape=jax.ShapeDtypeStruct((n, d), F32),
        compiler_params=_cparams(("parallel", "arbitrary")),
        name="peer_dense",
    )(h2, u, v, e, thr, iz)


def _final_kernel(h_ref, f_ref, g_ref, o_ref):
    x = h_ref[...] + f_ref[...]
    y = x * lax.rsqrt(jnp.mean(x * x, axis=-1, keepdims=True) + NORM_EPS)
    o_ref[...] = y * g_ref[...]


def _final(h, f, g, row0, rows, tm):
    d = h.shape[1]
    assert row0 % tm == 0 and rows % tm == 0
    src = pl.BlockSpec((tm, d), lambda i: (i + row0 // tm, 0))
    return pl.pallas_call(
        _final_kernel,
        grid=(rows // tm,),
        in_specs=[src, src, pl.BlockSpec((1, d), lambda i: (0, 0))],
        out_specs=pl.BlockSpec((tm, d), lambda i: (i, 0)),
        out_shape=jax.ShapeDtypeStruct((rows, d), F32),
        compiler_params=_cparams(("parallel",)),
        name="final_norm",
    )(h, f, g.reshape(1, d))


def _heads_to_rows(q, nb, dec):
    return q.reshape(nb, dec, MOBA_HEADS, HEAD_DIM).transpose(0, 2, 1, 3).reshape(nb, MOBA_HEADS * dec, HEAD_DIM)


def _rows_to_heads(o, nb, dec):
    return o.reshape(nb, MOBA_HEADS, dec, HEAD_DIM).transpose(0, 2, 1, 3).reshape(nb * dec, MOBA_HEADS * HEAD_DIM)


def _sample_mix(proj_s, page_table, c_mk, c_mv, c_dk, c_dv, c_ik, nb, dec):
    ps = proj_s.reshape(nb, dec, PROJ_PAD)
    seg = lambda c, w: ps[:, :, c:c + w]
    kv_rows = lambda c: seg(c, KVW).reshape(nb, dec * KV_HEADS, HEAD_DIM)
    s_m = _sample_moba(page_table, _heads_to_rows(seg(C_QM, MOBA_HEADS * HEAD_DIM), nb, dec),
                       kv_rows(C_KM), kv_rows(C_VM), c_mk, c_mv)
    qi = seg(C_QI, IDX_HEADS * IDX_DIM).reshape(nb, dec * IDX_HEADS, IDX_DIM)
    wi = seg(C_WI, IDX_HEADS).reshape(nb, dec * IDX_HEADS, 1)
    s_d = _sample_dsa(page_table, qi, wi, _heads_to_rows(seg(C_QD, DSA_HEADS * HEAD_DIM), nb, dec),
                      seg(C_KI, IDX_DIM), kv_rows(C_KD), kv_rows(C_VD), c_ik, c_dk, c_dv)
    return jnp.concatenate([_rows_to_heads(s_m, nb, dec), _rows_to_heads(s_d, nb, dec)], axis=-1)


def _prompt_mix(proj, t):
    kmean = _kmean(proj, t)
    kmean = jnp.pad(kmean, ((0, LANES - kmean.shape[0]), (0, 0)))
    o_m = _prompt_attn(True, proj, kmean, t)
    o_d = _prompt_attn(False, proj, _dsa_select(proj, t), t)
    return o_m, o_d


def kernel(x_prompt, x_sample, cache_moba_k, cache_moba_v, cache_dsa_k, cache_dsa_v, cache_idx_k, page_table,
           norm_mix, w_in, w_out, norm_ffn, peer_w_q, peer_subkeys, peer_u, peer_v, norm_final):
    depth = w_in.shape[0]
    assert depth == 1 and x_prompt.shape[0] == 1
    assert PAST_LEN % MOBA_BLOCK == 0 and page_table.shape[1] == N_PAGES
    t = x_prompt.shape[1]
    nb, dec, d = x_sample.shape
    n_tok = t + nb * dec
    tm = 512
    assert n_tok % tm == 0 and t % tm == 0

    x_p = x_prompt.reshape(t, d)
    x_s = x_sample.reshape(nb * dec, d)
    pos = jnp.concatenate([jnp.arange(t, dtype=jnp.int32),
                           PAST_LEN + jnp.tile(jnp.arange(dec, dtype=jnp.int32), nb)])
    cos, sin = _rope_tables(pos)

    n_pool = cache_moba_k.shape[1]
    flat = lambda c: c[0].reshape(n_pool, PAGE_ROWS, HEAD_DIM)

    u = _rmsnorm(x_p, x_s, norm_mix[0], BF16, tm)
    tm_proj = n_tok // 8 if (n_tok // 8) % 16 == 0 and n_tok % 8 == 0 else tm
    proj = _project(u, w_in[0].astype(BF16), cos, sin, tm_proj)
    o_m, o_d = _prompt_mix(proj, t)
    mix_s = _sample_mix(proj[t:], page_table, flat(cache_moba_k), flat(cache_moba_v), flat(cache_dsa_k),
                        flat(cache_dsa_v), cache_idx_k[0], nb, dec)
    h = _outproj(o_m, o_d, mix_s.astype(BF16), w_out[0].astype(BF16), x_p, x_s, tm, d // 2, 512)
    h2 = _rmsnorm_rows(h, norm_ffn[0], BF16, tm)
    e, thr, iz = _peer_select(h2, peer_w_q[0].astype(BF16), peer_subkeys[0].astype(BF16), tm, 512)
    f = _peer_dense(h2, peer_u[0].astype(BF16), peer_v[0].astype(BF16), e, thr, iz, tm, 512)

    y_prompt = _final(h, f, norm_final, 0, t, 256).reshape(1, t, d)
    y_sample = _final(h, f, norm_final, t, nb * dec, 256).reshape(nb, dec, d)
    pp = proj[:t]
    ps = proj[t:]
    kv = lambda a, c: a[:, c:c + KVW]
    outs_p = [kv(pp, c).reshape(1, 1, t, KV_HEADS, HEAD_DIM) for c in (C_KM, C_VM, C_KD, C_VD)]
    outs_p.append(pp[:, C_KI:C_KI + IDX_DIM].reshape(1, 1, t, IDX_DIM))
    outs_s = [kv(ps, c).reshape(1, nb, dec, KV_HEADS, HEAD_DIM) for c in (C_KM, C_VM, C_KD, C_VD)]
    outs_s.append(ps[:, C_KI:C_KI + IDX_DIM].reshape(1, nb, dec, IDX_DIM))
    return (y_prompt, y_sample, *outs_p, *outs_s)
```

```python
import functools

import numpy as np
import jax
import jax.numpy as jnp
from jax import lax
from jax.experimental import pallas as pl
from jax.experimental.pallas import tpu as pltpu

F32 = jnp.float32
BF16 = jnp.bfloat16

D_MODEL = 4096
HEAD_DIM = 128
MOBA_HEADS = 16
DSA_HEADS = 16
KV_HEADS = 4
GROUP = 4
MOBA_BLOCK = 256
MOBA_TOPK = 3
IDX_HEADS = 32
IDX_DIM = 128
DSA_TOPK_MAX = 256
ROPE_THETA = 10000.0
ATTN_SCALE = HEAD_DIM ** -0.5
IDX_W_SCALE = (IDX_HEADS ** -0.5) * (IDX_DIM ** -0.5)
PEER_HEADS = 8
PEER_N_KEYS = 128
PEER_DK = 256
PEER_TOPK = 16
NORM_EPS = 1e-6
PAST_LEN = 2048
PAGE_SIZE = 128

LANES = 128
SUBLANES = 8
NEG = -1e30
INT_MIN = -2 ** 31
VMEM_LIMIT = 56 * 1024 * 1024

C_QI = 0
C_QM = C_QI + IDX_HEADS * IDX_DIM
C_QD = C_QM + MOBA_HEADS * HEAD_DIM
C_KM = C_QD + DSA_HEADS * HEAD_DIM
C_VM = C_KM + KV_HEADS * HEAD_DIM
C_KD = C_VM + KV_HEADS * HEAD_DIM
C_VD = C_KD + KV_HEADS * HEAD_DIM
C_KI = C_VD + KV_HEADS * HEAD_DIM
C_WI = C_KI + IDX_DIM
PROJ_TN = 512
PROJ_PAD = 10752
KVW = KV_HEADS * HEAD_DIM


def _cparams(sem):
    return pltpu.CompilerParams(dimension_semantics=sem, vmem_limit_bytes=VMEM_LIMIT)


def _dot_nt(a, b):
    return lax.dot_general(a, b, (((1,), (1,)), ((), ())), preferred_element_type=F32)


def _dot(a, b):
    return jnp.dot(a, b, preferred_element_type=F32)


def _two_source_specs(tm, d, tiles_a):
    return [pl.BlockSpec((tm, d), lambda i, *_: (jnp.minimum(i, tiles_a - 1), 0)),
            pl.BlockSpec((tm, d), lambda i, *_: (jnp.maximum(i - tiles_a, 0), 0))]


def _rmsnorm_kernel(tiles_a, xa_ref, xb_ref, g_ref, o_ref):
    x = jnp.where(pl.program_id(0) < tiles_a, xa_ref[...], xb_ref[...])
    y = x * lax.rsqrt(jnp.mean(x * x, axis=-1, keepdims=True) + NORM_EPS)
    o_ref[...] = (y * g_ref[...]).astype(o_ref.dtype)


def _rmsnorm(xa, xb, g, out_dtype, tm):
    (na, d), nb = xa.shape, xb.shape[0]
    assert na % tm == 0 and nb % tm == 0
    return pl.pallas_call(
        functools.partial(_rmsnorm_kernel, na // tm),
        grid=((na + nb) // tm,),
        in_specs=_two_source_specs(tm, d, na // tm) + [pl.BlockSpec((1, d), lambda i: (0, 0))],
        out_specs=pl.BlockSpec((tm, d), lambda i: (i, 0)),
        out_shape=jax.ShapeDtypeStruct((na + nb, d), out_dtype),
        compiler_params=_cparams(("parallel",)),
        name="rmsnorm",
    )(xa, xb, g.reshape(1, d))


def _rope_group(a, cos, sin):
    return a * cos + pltpu.roll(a, HEAD_DIM // 2, axis=1) * sin


def _proj_kernel(src_ref, u_ref, w_ref, wt_ref, cos_ref, sin_ref, o_ref):
    j = pl.program_id(1)
    plain = jnp.logical_or(j == C_VM // PROJ_TN, j == C_VD // PROJ_TN)
    last = j == C_KI // PROJ_TN
    groups = PROJ_TN // HEAD_DIM

    @pl.when(plain)
    def _():
        o_ref[...] = _dot(u_ref[...], w_ref[...])

    @pl.when(jnp.logical_not(jnp.logical_or(plain, last)))
    def _():
        acc = _dot(u_ref[...], w_ref[...])
        cos = cos_ref[...]
        sin = sin_ref[...]
        for c in range(groups):
            o_ref[:, c * HEAD_DIM:(c + 1) * HEAD_DIM] = _rope_group(acc[:, c * HEAD_DIM:(c + 1) * HEAD_DIM], cos, sin)

    @pl.when(last)
    def _():
        acc = _dot(u_ref[...], wt_ref[...])
        o_ref[:, 0:HEAD_DIM] = _rope_group(acc[:, 0:HEAD_DIM], cos_ref[...], sin_ref[...])
        o_ref[:, HEAD_DIM:2 * HEAD_DIM] = acc[:, HEAD_DIM:2 * HEAD_DIM] * IDX_W_SCALE
        o_ref[:, 2 * HEAD_DIM:] = acc[:, 2 * HEAD_DIM:]


def _w_in_tile_order():
    model = dict(q_m=0, k_m=MOBA_HEADS * HEAD_DIM)
    model["v_m"] = model["k_m"] + KVW
    model["q_d"] = model["v_m"] + KVW
    model["k_d"] = model["q_d"] + DSA_HEADS * HEAD_DIM
    model["v_d"] = model["k_d"] + KVW
    model["q_i"] = model["v_d"] + KVW
    model["k_i"] = model["q_i"] + IDX_HEADS * IDX_DIM
    ours = (("q_i", C_QI, C_QM), ("q_m", C_QM, C_QD), ("q_d", C_QD, C_KM), ("k_m", C_KM, C_VM), ("v_m", C_VM, C_KD),
            ("k_d", C_KD, C_VD), ("v_d", C_VD, C_KI))
    order = []
    for name, lo, hi in ours:
        assert lo % PROJ_TN == 0 and model[name] % PROJ_TN == 0
        order += [model[name] // PROJ_TN + x for x in range((hi - lo) // PROJ_TN)]
    assert model["k_i"] == C_KI and PROJ_PAD - C_KI == PROJ_TN
    order.append(order[-1])
    return np.asarray(order, np.int32)


def _project(u, w, w_tail, cos, sin, tm):
    n, d = u.shape
    return pl.pallas_call(
        _proj_kernel,
        grid_spec=pltpu.PrefetchScalarGridSpec(
            num_scalar_prefetch=1,
            grid=(n // tm, PROJ_PAD // PROJ_TN),
            in_specs=[
                pl.BlockSpec((tm, d), lambda i, j, src: (i, 0)),
                pl.BlockSpec((d, PROJ_TN), lambda i, j, src: (0, src[j])),
                pl.BlockSpec((d, PROJ_TN), lambda i, j, src: (0, 0)),
                pl.BlockSpec((tm, HEAD_DIM), lambda i, j, src: (i, 0)),
                pl.BlockSpec((tm, HEAD_DIM), lambda i, j, src: (i, 0)),
            ],
            out_specs=pl.BlockSpec((tm, PROJ_TN), lambda i, j, src: (i, j)),
        ),
        out_shape=jax.ShapeDtypeStruct((n, PROJ_PAD), F32),
        compiler_params=_cparams(("parallel", "arbitrary")),
        name="proj_rope",
    )(jnp.asarray(_w_in_tile_order()), u, w, w_tail, cos, sin)


def _rope_tables(pos):
    half = HEAD_DIM // 2
    inv = 1.0 / (ROPE_THETA ** (jnp.arange(half, dtype=F32) * (2.0 / HEAD_DIM)))
    ang = pos.astype(F32)[:, None] * inv[None, :]
    c, s = jnp.cos(ang), jnp.sin(ang)
    return jnp.concatenate([c, c], axis=-1), jnp.concatenate([-s, s], axis=-1)


def _kmean_kernel(k_ref, o_ref):
    nb = o_ref.shape[0]
    k = k_ref[...].reshape(nb, MOBA_BLOCK, KVW)
    o_ref[...] = jnp.sum(k, axis=1) * (1.0 / MOBA_BLOCK)


def _kmean(proj, t):
    nb = t // MOBA_BLOCK
    return pl.pallas_call(
        _kmean_kernel,
        grid=(1,),
        in_specs=[pl.BlockSpec((t, KVW), lambda i: (0, C_KM // KVW))],
        out_specs=pl.BlockSpec((nb, KVW), lambda i: (0, 0)),
        out_shape=jax.ShapeDtypeStruct((nb, KVW), F32),
        compiler_params=_cparams(("arbitrary",)),
        name="moba_kmean",
    )(proj)


def _top_block_ids(gate, lane, n_sel):
    ids = []
    for _ in range(n_sel):
        mx = jnp.max(gate, axis=-1, keepdims=True)
        idx = jnp.min(jnp.where(gate == mx, lane, LANES), axis=-1, keepdims=True)
        ids.append(jnp.where(mx > -jnp.inf, idx, -1))
        gate = jnp.where(lane == idx, -jnp.inf, gate)
    return ids


def _sortable_key(x):
    bits = pltpu.bitcast(x + 0.0, jnp.int32)
    return bits ^ ((bits >> 31) & 0x7FFFFFFF)


def _kth_largest_key(count_ge, shape, k, two_bits=False):
    t = jnp.broadcast_to(jnp.where(count_ge(jnp.zeros(shape, jnp.int32)) >= k, 0, INT_MIN).astype(jnp.int32), shape)

    def one_bit(bit, t):
        cand = t + (jnp.int32(1) << bit)
        return jnp.where(count_ge(cand) >= k, cand, t)

    if not two_bits:
        return lax.fori_loop(0, 31, lambda it, t: one_bit(30 - it, t), t)

    def pair(it, t):
        step = jnp.int32(1) << (29 - 2 * it)
        c1, c2, c3 = t + step, t + 2 * step, t + 3 * step
        n1, n2, n3 = count_ge(c1), count_ge(c2), count_ge(c3)
        return jnp.where(n3 >= k, c3, jnp.where(n2 >= k, c2, jnp.where(n1 >= k, c1, t)))

    return one_bit(0, lax.fori_loop(0, 15, pair, t))


PTQ = 256
PTK = 512


def _prompt_attn_kernel(moba, q_ref, k_ref, v_ref, aux_ref, o_ref, qs_ref, m_ref, acc_ref, sel_ref):
    i = pl.program_id(0)
    j = pl.program_id(1)
    rows = GROUP * PTQ
    last_j = (i * PTQ + PTQ - 1) // PTK
    chunks = PTK // LANES

    @pl.when(j == 0)
    def _():
        for n in range(KV_HEADS):
            for g in range(GROUP):
                h = n * GROUP + g
                qs_ref[n, g * PTQ:(g + 1) * PTQ, :] = q_ref[:, h * HEAD_DIM:(h + 1) * HEAD_DIM].astype(BF16)
        m_ref[...] = jnp.full(m_ref.shape, NEG, F32)
        acc_ref[...] = jnp.zeros(acc_ref.shape, F32)
        if moba:
            blk = lax.broadcasted_iota(jnp.int32, (LANES, rows), 0)
            for n in range(KV_HEADS):
                km = aux_ref[:, n * HEAD_DIM:(n + 1) * HEAD_DIM].astype(BF16)
                gate = jnp.where(blk < i, _dot_nt(km, qs_ref[n]), -jnp.inf)
                bits = jnp.zeros((1, rows), jnp.int32)
                for _ in range(MOBA_TOPK):
                    mx = jnp.max(gate, axis=0, keepdims=True)
                    idx = jnp.min(jnp.where(gate == mx, blk, LANES), axis=0, keepdims=True)
                    bits = bits | jnp.where(mx > -jnp.inf, jnp.int32(1) << jnp.minimum(idx, 31), 0)
                    gate = jnp.where(blk == idx, -jnp.inf, gate)
                lo = jnp.broadcast_to((bits & 0xFFFF).astype(F32), (LANES, rows)).T
                hi = jnp.broadcast_to((bits >> 16).astype(F32), (LANES, rows)).T
                sel_ref[n] = lo.astype(jnp.int32) | (hi.astype(jnp.int32) << 16)

    def step(diagonal):
        k = k_ref[...].astype(BF16)
        v = v_ref[...].astype(BF16)
        ones = jnp.ones((PTK, HEAD_DIM), BF16)
        if moba and diagonal:
            lane = lax.broadcasted_iota(jnp.int32, (rows, LANES), 1)
            t_in = lax.broadcasted_iota(jnp.int32, (rows, LANES), 0) & (PTQ - 1)
        if not moba:
            bias = jnp.concatenate([aux_ref[...].T] * GROUP, axis=0)
        for n in range(KV_HEADS):
            s = _dot_nt(qs_ref[n], k[:, n * HEAD_DIM:(n + 1) * HEAD_DIM]) * ATTN_SCALE
            if moba and diagonal:
                bits = sel_ref[n]
                limits = []
                for b in range(PTK // MOBA_BLOCK):
                    kb = j * (PTK // MOBA_BLOCK) + b
                    picked = (bits >> jnp.minimum(kb, 31)) & 1
                    limits.append(jnp.where(kb == i, t_in, jnp.where(kb < i, picked * MOBA_BLOCK, 0) - 1))
                parts = []
                for c in range(chunks):
                    col = lane + (c * LANES) % MOBA_BLOCK
                    parts.append(jnp.where(col <= limits[(c * LANES) // MOBA_BLOCK],
                                           s[:, c * LANES:(c + 1) * LANES], NEG))
            elif moba:
                bits = sel_ref[n]
                row_bias = [jnp.where(((bits >> (j * (PTK // MOBA_BLOCK) + b)) & 1) == 1, 0.0, NEG)
                            for b in range(PTK // MOBA_BLOCK)]
                parts = [s[:, c * LANES:(c + 1) * LANES] + row_bias[(c * LANES) // MOBA_BLOCK] for c in range(chunks)]
            else:
                s = s + bias
                parts = [s[:, c * LANES:(c + 1) * LANES] for c in range(chunks)]
            mx = parts[0]
            for c in range(1, chunks):
                mx = jnp.maximum(mx, parts[c])
            m_prev = m_ref[n]
            m_new = jnp.maximum(m_prev, jnp.max(mx, axis=-1, keepdims=True))
            alpha = jnp.exp(m_prev - m_new)
            p = jnp.concatenate([jnp.exp(x - m_new) for x in parts], axis=1).astype(BF16)
            v_ones = jnp.concatenate([v[:, n * HEAD_DIM:(n + 1) * HEAD_DIM], ones], axis=1)
            acc_ref[n] = jnp.concatenate([alpha, alpha], axis=1) * acc_ref[n] + _dot(p, v_ones)
            m_ref[n] = m_new

    if moba:
        pl.when(j < last_j)(functools.partial(step, False))
        pl.when(j == last_j)(functools.partial(step, True))
    else:
        pl.when(j <= last_j)(functools.partial(step, False))

    @pl.when(j == last_j)
    def _():
        for n in range(KV_HEADS):
            for g in range(GROUP):
                h = n * GROUP + g
                r = slice(g * PTQ, (g + 1) * PTQ)
                o_ref[:, h * HEAD_DIM:(h + 1) * HEAD_DIM] = (acc_ref[n, r, 0:HEAD_DIM]
                                                             / acc_ref[n, r, HEAD_DIM:]).astype(o_ref.dtype)


def _prompt_attn(moba, proj, aux, t):
    nq, nk = t // PTQ, t // PTK
    assert t // MOBA_BLOCK <= 32
    qw = MOBA_HEADS * HEAD_DIM
    c_q, c_k, c_v = (C_QM, C_KM, C_VM) if moba else (C_QD, C_KD, C_VD)
    jj = lambda i, j: jnp.minimum(j, (i * PTQ + PTQ - 1) // PTK)
    if moba:
        aux_spec = pl.BlockSpec((LANES, KVW), lambda i, j: (0, 0))
    else:
        aux_spec = pl.BlockSpec((PTK, PTQ), lambda i, j: (jj(i, j), i))
    rows = GROUP * PTQ
    return pl.pallas_call(
        functools.partial(_prompt_attn_kernel, moba),
        grid=(nq, nk),
        in_specs=[
            pl.BlockSpec((PTQ, qw), lambda i, j: (i, c_q // qw)),
            pl.BlockSpec((PTK, KVW), lambda i, j: (jj(i, j), c_k // KVW)),
            pl.BlockSpec((PTK, KVW), lambda i, j: (jj(i, j), c_v // KVW)),
            aux_spec,
        ],
        out_specs=pl.BlockSpec((PTQ, qw), lambda i, j: (i, 0)),
        out_shape=jax.ShapeDtypeStruct((t, qw), BF16),
        scratch_shapes=[
            pltpu.VMEM((KV_HEADS, rows, HEAD_DIM), BF16),
            pltpu.VMEM((KV_HEADS, rows, LANES), F32),
            pltpu.VMEM((KV_HEADS, rows, HEAD_DIM + LANES), F32),
            pltpu.VMEM((KV_HEADS, rows, LANES), jnp.int32),
        ],
        compiler_params=_cparams(("parallel", "arbitrary")),
        name="moba_prompt" if moba else "dsa_prompt",
    )(proj, proj, proj, aux)


DTK = 512


COUNT_ROWS = 32


def _dsa_select_kernel(k_keep, qi_ref, ki_ref, wt_ref, bias_ref, qs_ref, key_ref):
    i = pl.program_id(0)
    j = pl.program_id(1)
    nj = pl.num_programs(1)
    length = key_ref.shape[0]

    @pl.when(j == 0)
    def _():
        qs_ref[...] = qi_ref[...].astype(BF16)
        key_ref[...] = jnp.full(key_ref.shape, INT_MIN, jnp.int32)

    @pl.when(j * DTK <= i * PTQ + PTQ - 1)
    def _():
        kb = ki_ref[...].astype(BF16)
        acc = jnp.zeros((DTK, PTQ), F32)
        for h in range(IDX_HEADS):
            s = _dot_nt(kb, qs_ref[:, h * IDX_DIM:(h + 1) * IDX_DIM])
            acc = acc + wt_ref[h:h + 1, :] * jnp.maximum(s, 0.0)
        key_pos = j * DTK + lax.broadcasted_iota(jnp.int32, (DTK, PTQ), 0)
        q_pos = i * PTQ + lax.broadcasted_iota(jnp.int32, (DTK, PTQ), 1)
        x = jnp.where(key_pos <= q_pos, acc, -jnp.inf)
        key_ref[pl.ds(pl.multiple_of(j * DTK, DTK), DTK), :] = _sortable_key(x)

    @pl.when(j == nj - 1)
    def _():
        n_tiles = (i * PTQ + PTQ - 1) // DTK + 1

        def count_ge(cand):
            def tile(c, acc):
                keys = key_ref[pl.ds(pl.multiple_of(c * DTK, DTK), DTK), :]
                hit = jnp.where(keys.reshape(DTK // COUNT_ROWS, COUNT_ROWS, PTQ) >= cand, 1, 0)
                return acc + jnp.sum(hit, axis=0)

            acc = lax.fori_loop(0, n_tiles, tile, jnp.zeros((COUNT_ROWS, PTQ), jnp.int32))
            return jnp.sum(acc, axis=0, keepdims=True)

        thr = _kth_largest_key(count_ge, (1, PTQ), k_keep)
        key_pos = lax.broadcasted_iota(jnp.int32, (length, PTQ), 0)
        q_pos = i * PTQ + lax.broadcasted_iota(jnp.int32, (length, PTQ), 1)
        ok = jnp.logical_and(key_ref[...] >= thr, key_pos <= q_pos)
        bias_ref[...] = jnp.where(ok, 0.0, NEG)


def _dsa_select(proj, t):
    nq, nj = t // PTQ, t // DTK
    k_keep = max(1, min(DSA_TOPK_MAX, t // 4))
    qw = IDX_HEADS * IDX_DIM
    last_j = lambda i: (i * PTQ + PTQ - 1) // DTK
    w_t = proj[:t, C_WI:C_WI + IDX_HEADS].T
    return pl.pallas_call(
        functools.partial(_dsa_select_kernel, k_keep),
        grid=(nq, nj),
        in_specs=[
            pl.BlockSpec((PTQ, qw), lambda i, j: (i, C_QI // qw)),
            pl.BlockSpec((DTK, IDX_DIM), lambda i, j: (jnp.minimum(j, last_j(i)), C_KI // IDX_DIM)),
            pl.BlockSpec((IDX_HEADS, PTQ), lambda i, j: (0, i)),
        ],
        out_specs=pl.BlockSpec((t, PTQ), lambda i, j: (0, i)),
        out_shape=jax.ShapeDtypeStruct((t, t), F32),
        scratch_shapes=[pltpu.VMEM((PTQ, qw), BF16), pltpu.VMEM((t, PTQ), jnp.int32)],
        compiler_params=_cparams(("parallel", "arbitrary")),
        name="dsa_select",
    )(proj, proj, w_t)


N_PAGES = PAST_LEN // PAGE_SIZE
PAGES_PER_BLOCK = MOBA_BLOCK // PAGE_SIZE
PAGE_ROWS = PAGE_SIZE * KV_HEADS


def _head_match(rows, cols, dec):
    r = lax.broadcasted_iota(jnp.int32, (rows, cols), 0)
    c = lax.broadcasted_iota(jnp.int32, (rows, cols), 1)
    shift = (GROUP * dec).bit_length() - 1
    return (r >> shift) == (c & (KV_HEADS - 1))


def _stage_new(dst_ref, src_ref):
    dst_ref[...] = jnp.zeros(dst_ref.shape, F32)
    dst_ref[0:src_ref.shape[0], :] = src_ref[...]


def _softmax_over_pages(s_ref, s_new, v_refs, nv_ref, o_ref):
    m = jnp.max(s_new, axis=-1, keepdims=True)
    for p in range(N_PAGES):
        m = jnp.maximum(m, jnp.max(s_ref[p], axis=-1, keepdims=True))
    pn = jnp.exp(s_new - m)
    l = jnp.sum(pn, axis=-1, keepdims=True)
    acc = _dot(pn.astype(BF16), nv_ref[...].astype(BF16))
    for p in range(N_PAGES):
        pr = jnp.exp(s_ref[p] - m)
        l = l + jnp.sum(pr, axis=-1, keepdims=True)
        acc = acc + _dot(pr.astype(BF16), v_refs[p][...].astype(BF16))
    o_ref[...] = acc / l


def _sample_moba_kernel(pt_ref, q_ref, kn_ref, vn_ref, *rest):
    k_refs, v_refs = rest[:N_PAGES], rest[N_PAGES:2 * N_PAGES]
    o_ref, s_ref, nk_ref, nv_ref = rest[2 * N_PAGES:]
    rows = q_ref.shape[0]
    dec = rows // MOBA_HEADS
    n_past_blocks = PAST_LEN // MOBA_BLOCK
    q = q_ref[...].astype(BF16)
    match = _head_match(rows, PAGE_ROWS, dec)
    lane = lax.broadcasted_iota(jnp.int32, (rows, LANES), 1)

    for p in range(N_PAGES):
        s_ref[p] = _dot_nt(q, k_refs[p][...].astype(BF16))

    gate = jnp.full((rows, LANES), -jnp.inf, F32)
    for b in range(n_past_blocks):
        tot = s_ref[PAGES_PER_BLOCK * b]
        for c in range(1, PAGES_PER_BLOCK):
            tot = tot + s_ref[PAGES_PER_BLOCK * b + c]
        g = jnp.sum(jnp.where(match, tot, 0.0), axis=-1, keepdims=True) * (1.0 / MOBA_BLOCK)
        gate = jnp.where(lane == b, g, gate)
    ids = _top_block_ids(gate, lane, MOBA_TOPK)

    for b in range(n_past_blocks):
        picked = ids[0] == b
        for idx in ids[1:]:
            picked = jnp.logical_or(picked, idx == b)
        row_bias = jnp.where(picked, 0.0, NEG)
        for c in range(PAGES_PER_BLOCK):
            p = PAGES_PER_BLOCK * b + c
            s_ref[p] = jnp.where(match, s_ref[p] * ATTN_SCALE + row_bias, NEG)

    _stage_new(nk_ref, kn_ref)
    _stage_new(nv_ref, vn_ref)
    rn = lax.broadcasted_iota(jnp.int32, (rows, LANES), 0)
    ok = jnp.logical_and(_head_match(rows, LANES, dec), (lane >> 2) <= (rn & (dec - 1)))
    s_new = jnp.where(ok, _dot_nt(q, nk_ref[...].astype(BF16)) * ATTN_SCALE, NEG)
    _softmax_over_pages(s_ref, s_new, v_refs, nv_ref, o_ref)


def _page_specs(shape, n):
    return [pl.BlockSpec((None,) + shape, functools.partial(lambda b, pt, p: (pt[b, p], 0, 0), p=p)) for p in range(n)]


def _sample_moba(pt, q, kn, vn, ck, cv):
    nb, rows, _ = q.shape
    dec = rows // MOBA_HEADS
    assert dec & (dec - 1) == 0 and dec * KV_HEADS <= LANES and KV_HEADS == 4
    seq = lambda b, pt: (b, 0, 0)
    return pl.pallas_call(
        _sample_moba_kernel,
        grid_spec=pltpu.PrefetchScalarGridSpec(
            num_scalar_prefetch=1,
            grid=(nb,),
            in_specs=[pl.BlockSpec((None, rows, HEAD_DIM), seq),
                      pl.BlockSpec((None, dec * KV_HEADS, HEAD_DIM), seq),
                      pl.BlockSpec((None, dec * KV_HEADS, HEAD_DIM), seq)]
                     + _page_specs((PAGE_ROWS, HEAD_DIM), N_PAGES) + _page_specs((PAGE_ROWS, HEAD_DIM), N_PAGES),
            out_specs=pl.BlockSpec((None, rows, HEAD_DIM), seq),
            scratch_shapes=[
                pltpu.VMEM((N_PAGES, rows, PAGE_ROWS), F32),
                pltpu.VMEM((LANES, HEAD_DIM), F32),
                pltpu.VMEM((LANES, HEAD_DIM), F32),
            ],
        ),
        out_shape=jax.ShapeDtypeStruct((nb, rows, HEAD_DIM), F32),
        compiler_params=_cparams(("parallel",)),
        name="moba_sample",
    )(pt, q, kn, vn, *([ck] * N_PAGES), *([cv] * N_PAGES))


def _sample_dsa_kernel(k_keep, pt_ref, qi_ref, wi_ref, q_ref, kin_ref, kn_ref, vn_ref, *rest):
    i_refs, k_refs, v_refs = rest[:N_PAGES], rest[N_PAGES:2 * N_PAGES], rest[2 * N_PAGES:3 * N_PAGES]
    o_ref, idx_ref, s_ref, nki_ref, nk_ref, nv_ref = rest[3 * N_PAGES:]
    rows = q_ref.shape[0]
    dec = rows // DSA_HEADS
    qi = qi_ref[...].astype(BF16)
    w = wi_ref[...]

    def index_scores(kpage):
        sc = jnp.maximum(_dot_nt(qi, kpage.astype(BF16)), 0.0) * w
        return jnp.sum(sc.reshape(dec, IDX_HEADS, PAGE_SIZE), axis=1)

    idx_ref[...] = jnp.full(idx_ref.shape, -jnp.inf, F32)
    for p in range(N_PAGES):
        idx_ref[p, 0:dec, :] = index_scores(i_refs[p][...])
    _stage_new(nki_ref, kin_ref)
    t_in = lax.broadcasted_iota(jnp.int32, (dec, PAGE_SIZE), 0)
    col = lax.broadcasted_iota(jnp.int32, (dec, PAGE_SIZE), 1)
    idx_ref[N_PAGES, 0:dec, :] = jnp.where(col <= t_in, index_scores(nki_ref[...]), -jnp.inf)

    keys = _sortable_key(idx_ref[...])

    def count_ge(cand):
        c = jnp.sum((keys >= cand).astype(jnp.int32), axis=0)
        return jnp.sum(c, axis=-1, keepdims=True)[None]

    thr = _kth_largest_key(count_ge, (1, SUBLANES, 1), k_keep, two_bits=True)
    keep = jnp.where(jnp.logical_and(keys >= thr, idx_ref[...] > -jnp.inf), 1.0, 0.0)

    q = q_ref[...].astype(BF16)
    match = _head_match(rows, PAGE_ROWS, dec)
    er = lax.broadcasted_iota(jnp.int32, (PAGE_SIZE, PAGE_ROWS), 0)
    ec = lax.broadcasted_iota(jnp.int32, (PAGE_SIZE, PAGE_ROWS), 1)
    expand = jnp.where((ec >> 2) == er, 1.0, 0.0).astype(BF16)

    def visible(page, width):
        per_row = jnp.concatenate([keep[page, 0:dec, :]] * (rows // dec), axis=0).astype(BF16)
        return _dot(per_row, expand[:, 0:width]) > 0.5

    for p in range(N_PAGES):
        sc = _dot_nt(q, k_refs[p][...].astype(BF16)) * ATTN_SCALE
        s_ref[p] = jnp.where(jnp.logical_and(match, visible(p, PAGE_ROWS)), sc, NEG)
    _stage_new(nk_ref, kn_ref)
    _stage_new(nv_ref, vn_ref)
    ok = jnp.logical_and(_head_match(rows, LANES, dec), visible(N_PAGES, LANES))
    s_new = jnp.where(ok, _dot_nt(q, nk_ref[...].astype(BF16)) * ATTN_SCALE, NEG)
    _softmax_over_pages(s_ref, s_new, v_refs, nv_ref, o_ref)


def _sample_dsa(pt, qi, wi, q, kin, kn, vn, ci, ck, cv):
    nb, rows, _ = q.shape
    dec = rows // DSA_HEADS
    assert dec & (dec - 1) == 0 and dec <= SUBLANES and KV_HEADS == 4
    k_keep = max(1, min(DSA_TOPK_MAX, (PAST_LEN + dec) // 4))
    seq = lambda b, pt: (b, 0, 0)
    return pl.pallas_call(
        functools.partial(_sample_dsa_kernel, k_keep),
        grid_spec=pltpu.PrefetchScalarGridSpec(
            num_scalar_prefetch=1,
            grid=(nb,),
            in_specs=[pl.BlockSpec((None, dec * IDX_HEADS, IDX_DIM), seq),
                      pl.BlockSpec((None, dec * IDX_HEADS, 1), seq),
                      pl.BlockSpec((None, rows, HEAD_DIM), seq),
                      pl.BlockSpec((None, dec, IDX_DIM), seq),
                      pl.BlockSpec((None, dec * KV_HEADS, HEAD_DIM), seq),
                      pl.BlockSpec((None, dec * KV_HEADS, HEAD_DIM), seq)]
                     + _page_specs((PAGE_SIZE, IDX_DIM), N_PAGES)
                     + _page_specs((PAGE_ROWS, HEAD_DIM), N_PAGES) + _page_specs((PAGE_ROWS, HEAD_DIM), N_PAGES),
            out_specs=pl.BlockSpec((None, rows, HEAD_DIM), seq),
            scratch_shapes=[
                pltpu.VMEM((N_PAGES + 1, SUBLANES, PAGE_SIZE), F32),
                pltpu.VMEM((N_PAGES, rows, PAGE_ROWS), F32),
                pltpu.VMEM((PAGE_SIZE, IDX_DIM), F32),
                pltpu.VMEM((LANES, HEAD_DIM), F32),
                pltpu.VMEM((LANES, HEAD_DIM), F32),
            ],
        ),
        out_shape=jax.ShapeDtypeStruct((nb, rows, HEAD_DIM), F32),
        compiler_params=_cparams(("parallel",)),
        name="dsa_sample",
    )(pt, qi, wi, q, kin, kn, vn, *([ci] * N_PAGES), *([ck] * N_PAGES), *([cv] * N_PAGES))


def _outproj_kernel(tiles_a, om_ref, od_ref, ms_ref, w_ref, xa_ref, xb_ref, h_ref, acc_ref):
    i = pl.program_id(0)
    k = pl.program_id(2)
    nk = pl.num_programs(2)
    first = i < tiles_a

    @pl.when(k == 0)
    def _():
        acc_ref[...] = jnp.zeros(acc_ref.shape, F32)

    mix = jnp.where(first, jnp.where(k < nk // 2, om_ref[...], od_ref[...]), ms_ref[...])
    acc_ref[...] += _dot(mix, w_ref[...])

    @pl.when(k == nk - 1)
    def _():
        h_ref[...] = jnp.where(first, xa_ref[...], xb_ref[...]) + acc_ref[...]


def _outproj(o_m, o_d, mix_b, w, xa, xb, tm, tn, tk):
    (na, d), nb = xa.shape, xb.shape[0]
    n = na + nb
    nk = d // tk
    half = nk // 2
    tiles_a = na // tm
    assert na % tm == 0 and nb % tm == 0 and nk % 2 == 0 and o_m.shape[1] == half * tk and d % tn == 0
    row_a = lambda i: jnp.minimum(i, tiles_a - 1)
    row_b = lambda i: jnp.maximum(i - tiles_a, 0)
    col_a = lambda i, x: jnp.where(i < tiles_a, x, 0)
    col_b = lambda i, x: jnp.where(i < tiles_a, 0, x)
    return pl.pallas_call(
        functools.partial(_outproj_kernel, tiles_a),
        grid=(n // tm, d // tn, nk),
        in_specs=[
            pl.BlockSpec((tm, tk), lambda i, c, k: (row_a(i), col_a(i, jnp.minimum(k, half - 1)))),
            pl.BlockSpec((tm, tk), lambda i, c, k: (row_a(i), col_a(i, jnp.maximum(k - half, 0)))),
            pl.BlockSpec((tm, tk), lambda i, c, k: (row_b(i), col_b(i, k))),
            pl.BlockSpec((tk, tn), lambda i, c, k: (k, c)),
            pl.BlockSpec((tm, tn), lambda i, c, k: (row_a(i), col_a(i, c))),
            pl.BlockSpec((tm, tn), lambda i, c, k: (row_b(i), col_b(i, c))),
        ],
        out_specs=pl.BlockSpec((tm, tn), lambda i, c, k: (i, c)),
        out_shape=jax.ShapeDtypeStruct((n, d), F32),
        scratch_shapes=[pltpu.VMEM((tm, tn), F32)],
        compiler_params=_cparams(("parallel", "parallel", "arbitrary")),
        name="outproj",
    )(o_m, o_d, mix_b, w, xa, xb)


def _rmsnorm_rows_kernel(x_ref, g_ref, o_ref):
    x = x_ref[...]
    y = x * lax.rsqrt(jnp.mean(x * x, axis=-1, keepdims=True) + NORM_EPS)
    o_ref[...] = (y * g_ref[...]).astype(o_ref.dtype)


def _rmsnorm_rows(x, g, out_dtype, tm):
    n, d = x.shape
    return pl.pallas_call(
        _rmsnorm_rows_kernel,
        grid=(n // tm,),
        in_specs=[pl.BlockSpec((tm, d), lambda i: (i, 0)), pl.BlockSpec((1, d), lambda i: (0, 0))],
        out_specs=pl.BlockSpec((tm, d), lambda i: (i, 0)),
        out_shape=jax.ShapeDtypeStruct((n, d), out_dtype),
        compiler_params=_cparams(("parallel",)),
        name="rmsnorm_ffn",
    )(x, g.reshape(1, d))


def _top_rows(x, k):
    r = x.shape[0]
    ridx = lax.broadcasted_iota(jnp.int32, x.shape, 0)
    outs = []
    for _ in range(k):
        mx = jnp.max(x, axis=0, keepdims=True)
        idx = jnp.min(jnp.where(x == mx, ridx, r), axis=0, keepdims=True)
        outs.append(mx)
        x = jnp.where(ridx == idx, -1.0, x)
    return jnp.concatenate(outs, axis=0)


def _peer_select_kernel(h2_ref, wq_ref, sk_ref, e_ref, thr_ref, iz_ref, acc_ref):
    k = pl.program_id(1)

    @pl.when(k == 0)
    def _():
        acc_ref[...] = jnp.zeros(acc_ref.shape, F32)

    acc_ref[...] += _dot(h2_ref[...], wq_ref[...])

    @pl.when(k == pl.num_programs(1) - 1)
    def _():
        half = PEER_DK // 2
        hk = PEER_TOPK // 2
        for h in range(PEER_HEADS):
            tops = []
            for p in range(2):
                c0 = (h * 2 + p) * half
                q = acc_ref[:, c0:c0 + half].astype(BF16)
                st = _dot_nt(sk_ref[h, p], q)
                e = jnp.exp(st - jnp.max(st, axis=0, keepdims=True))
                e_ref[h, p] = e
                tops.append(_top_rows(e, PEER_TOPK))
            a, b = tops
            cand = [a[x:x + 1, :] * b[0:hk, :] for x in range(hk)]
            cand += [a[0:1, :] * b[hk:, :], a[hk:, :] * b[0:1, :]]
            top = _top_rows(jnp.concatenate(cand, axis=0), PEER_TOPK)
            thr_ref[h:h + 1, :] = top[PEER_TOPK - 1:PEER_TOPK, :]
            iz_ref[h:h + 1, :] = 1.0 / jnp.sum(top, axis=0, keepdims=True)


def _peer_select(h2, wq, sk, tm, tk):
    n, d = h2.shape
    qw = wq.shape[1]
    return pl.pallas_call(
        _peer_select_kernel,
        grid=(n // tm, d // tk),
        in_specs=[
            pl.BlockSpec((tm, tk), lambda i, k: (i, k)),
            pl.BlockSpec((tk, qw), lambda i, k: (k, 0)),
            pl.BlockSpec(sk.shape, lambda i, k: (0, 0, 0, 0)),
        ],
        out_specs=[
            pl.BlockSpec((PEER_HEADS, 2, PEER_N_KEYS, tm), lambda i, k: (0, 0, 0, i)),
            pl.BlockSpec((PEER_HEADS, tm), lambda i, k: (0, i)),
            pl.BlockSpec((PEER_HEADS, tm), lambda i, k: (0, i)),
        ],
        out_shape=[
            jax.ShapeDtypeStruct((PEER_HEADS, 2, PEER_N_KEYS, n), F32),
            jax.ShapeDtypeStruct((PEER_HEADS, n), F32),
            jax.ShapeDtypeStruct((PEER_HEADS, n), F32),
        ],
        scratch_shapes=[pltpu.VMEM((tm, qw), F32)],
        compiler_params=_cparams(("parallel", "arbitrary")),
        name="peer_select",
    )(h2, wq, sk)


PEER_SUB = 256


def _peer_dense_kernel(h2_ref, u_ref, v_ref, e_ref, thr_ref, iz_ref, o_ref):
    e = pl.program_id(1)
    te = u_ref.shape[0]
    tm = h2_ref.shape[0]

    @pl.when(e == 0)
    def _():
        o_ref[...] = jnp.zeros(o_ref.shape, F32)

    at_all = _dot_nt(u_ref[...], h2_ref[...])
    total = None
    for sub in range(te // PEER_SUB):
        r0 = sub * PEER_SUB
        at = at_all[r0:r0 + PEER_SUB, :]
        act = 0.5 * at * (1.0 + lax.erf(at * (2.0 ** -0.5)))
        parts = []
        for c in range(PEER_SUB // PEER_N_KEYS):
            i = (e * te + r0) // PEER_N_KEYS + c
            g = jnp.zeros((PEER_N_KEYS, tm), F32)
            for h in range(PEER_HEADS):
                p = e_ref[h, 0, pl.ds(i, 1), :] * e_ref[h, 1]
                g = g + jnp.where(p >= thr_ref[h:h + 1, :], p, 0.0) * iz_ref[h:h + 1, :]
            parts.append(g)
        ga = (jnp.concatenate(parts, axis=0) * act).astype(BF16)
        d = lax.dot_general(ga, v_ref[r0:r0 + PEER_SUB, :], (((0,), (0,)), ((), ())), preferred_element_type=F32)
        total = d if total is None else total + d
    o_ref[...] += total


def _peer_dense(h2, u, v, e, thr, iz, tm, te):
    n, d = h2.shape
    ne = u.shape[0]
    return pl.pallas_call(
        _peer_dense_kernel,
        grid=(n // tm, ne // te),
        in_specs=[
            pl.BlockSpec((tm, d), lambda i, k: (i, 0)),
            pl.BlockSpec((te, d), lambda i, k: (k, 0)),
            pl.BlockSpec((te, d), lambda i, k: (k, 0)),
            pl.BlockSpec((PEER_HEADS, 2, PEER_N_KEYS, tm), lambda i, k: (0, 0, 0, i)),
            pl.BlockSpec((PEER_HEADS, tm), lambda i, k: (0, i)),
            pl.BlockSpec((PEER_HEADS, tm), lambda i, k: (0, i)),
        ],
        out_specs=pl.BlockSpec((tm, d), lambda i, k: (i, 0)),
        out_shape=jax.ShapeDtypeStruct((n, d), F32),
        compiler_params=_cparams(("parallel", "arbitrary")),
        name="peer_dense",
    )(h2, u, v, e, thr, iz)


def _final_kernel(h_ref, f_ref, g_ref, o_ref):
    x = h_ref[...] + f_ref[...]
    y = x * lax.rsqrt(jnp.mean(x * x, axis=-1, keepdims=True) + NORM_EPS)
    o_ref[...] = y * g_ref[...]


def _final(h, f, g, row0, rows, tm):
    d = h.shape[1]
    assert row0 % tm == 0 and rows % tm == 0
    src = pl.BlockSpec((tm, d), lambda i: (i + row0 // tm, 0))
    return pl.pallas_call(
        _final_kernel,
        grid=(rows // tm,),
        in_specs=[src, src, pl.BlockSpec((1, d), lambda i: (0, 0))],
        out_specs=pl.BlockSpec((tm, d), lambda i: (i, 0)),
        out_shape=jax.ShapeDtypeStruct((rows, d), F32),
        compiler_params=_cparams(("parallel",)),
        name="final_norm",
    )(h, f, g.reshape(1, d))


def _heads_to_rows(q, nb, dec):
    return q.reshape(nb, dec, MOBA_HEADS, HEAD_DIM).transpose(0, 2, 1, 3).reshape(nb, MOBA_HEADS * dec, HEAD_DIM)


def _rows_to_heads(o, nb, dec):
    return o.reshape(nb, MOBA_HEADS, dec, HEAD_DIM).transpose(0, 2, 1, 3).reshape(nb * dec, MOBA_HEADS * HEAD_DIM)


def _sample_mix(proj_s, page_table, c_mk, c_mv, c_dk, c_dv, c_ik, nb, dec):
    ps = proj_s.reshape(nb, dec, PROJ_PAD)
    seg = lambda c, w: ps[:, :, c:c + w]
    kv_rows = lambda c: seg(c, KVW).reshape(nb, dec * KV_HEADS, HEAD_DIM)
    s_m = _sample_moba(page_table, _heads_to_rows(seg(C_QM, MOBA_HEADS * HEAD_DIM), nb, dec),
                       kv_rows(C_KM), kv_rows(C_VM), c_mk, c_mv)
    qi = seg(C_QI, IDX_HEADS * IDX_DIM).reshape(nb, dec * IDX_HEADS, IDX_DIM)
    wi = seg(C_WI, IDX_HEADS).reshape(nb, dec * IDX_HEADS, 1)
    s_d = _sample_dsa(page_table, qi, wi, _heads_to_rows(seg(C_QD, DSA_HEADS * HEAD_DIM), nb, dec),
                      seg(C_KI, IDX_DIM), kv_rows(C_KD), kv_rows(C_VD), c_ik, c_dk, c_dv)
    return jnp.concatenate([_rows_to_heads(s_m, nb, dec), _rows_to_heads(s_d, nb, dec)], axis=-1)


def _prompt_mix(proj, t):
    kmean = _kmean(proj, t)
    kmean = jnp.pad(kmean, ((0, LANES - kmean.shape[0]), (0, 0)))
    o_m = _prompt_attn(True, proj, kmean, t)
    o_d = _prompt_attn(False, proj, _dsa_select(proj, t), t)
    return o_m, o_d


def kernel(x_prompt, x_sample, cache_moba_k, cache_moba_v, cache_dsa_k, cache_dsa_v, cache_idx_k, page_table,
           norm_mix, w_in, w_out, norm_ffn, peer_w_q, peer_subkeys, peer_u, peer_v, norm_final):
    depth = w_in.shape[0]
    assert depth == 1 and x_prompt.shape[0] == 1
    assert PAST_LEN % MOBA_BLOCK == 0 and page_table.shape[1] == N_PAGES
    t = x_prompt.shape[1]
    nb, dec, d = x_sample.shape
    n_tok = t + nb * dec
    tm = 512
    assert n_tok % tm == 0 and t % tm == 0

    x_p = x_prompt.reshape(t, d)
    x_s = x_sample.reshape(nb * dec, d)
    pos = jnp.concatenate([jnp.arange(t, dtype=jnp.int32),
                           PAST_LEN + jnp.tile(jnp.arange(dec, dtype=jnp.int32), nb)])
    cos, sin = _rope_tables(pos)

    n_pool = cache_moba_k.shape[1]
    flat = lambda c: c[0].reshape(n_pool, PAGE_ROWS, HEAD_DIM)

    u = _rmsnorm(x_p, x_s, norm_mix[0], BF16, tm)
    tm_proj = n_tok // 8 if (n_tok // 8) % 16 == 0 and n_tok % 8 == 0 else tm
    w_tail = w_in[0][:, C_KI:]
    w_tail = jnp.pad(w_tail, ((0, 0), (0, PROJ_TN - w_tail.shape[1]))).astype(BF16)
    proj = _project(u, w_in[0].astype(BF16), w_tail, cos, sin, tm_proj)
    o_m, o_d = _prompt_mix(proj, t)
    mix_s = _sample_mix(proj[t:], page_table, flat(cache_moba_k), flat(cache_moba_v), flat(cache_dsa_k),
                        flat(cache_dsa_v), cache_idx_k[0], nb, dec)
    h = _outproj(o_m, o_d, mix_s.astype(BF16), w_out[0].astype(BF16), x_p, x_s, tm, d // 2, 512)
    h2 = _rmsnorm_rows(h, norm_ffn[0], BF16, tm)
    e, thr, iz = _peer_select(h2, peer_w_q[0].astype(BF16), peer_subkeys[0].astype(BF16), tm, 512)
    f = _peer_dense(h2, peer_u[0].astype(BF16), peer_v[0].astype(BF16), e, thr, iz, tm, 512)

    y_prompt = _final(h, f, norm_final, 0, t, 256).reshape(1, t, d)
    y_sample = _final(h, f, norm_final, t, nb * dec, 256).reshape(nb, dec, d)
    pp = proj[:t]
    ps = proj[t:]
    kv = lambda a, c: a[:, c:c + KVW]
    outs_p = [kv(pp, c).reshape(1, 1, t, KV_HEADS, HEAD_DIM) for c in (C_KM, C_VM, C_KD, C_VD)]
    outs_p.append(pp[:, C_KI:C_KI + IDX_DIM].reshape(1, 1, t, IDX_DIM))
    outs_s = [kv(ps, c).reshape(1, nb, dec, KV_HEADS, HEAD_DIM) for c in (C_KM, C_VM, C_KD, C_VD)]
    outs_s.append(ps[:, C_KI:C_KI + IDX_DIM].reshape(1, nb, dec, IDX_DIM))
    return (y_prompt, y_sample, *outs_p, *outs_s)
```

```python
import functools

import numpy as np
import jax
import jax.numpy as jnp
from jax import lax
from jax.experimental import pallas as pl
from jax.experimental.pallas import tpu as pltpu

F32 = jnp.float32
BF16 = jnp.bfloat16

D_MODEL = 4096
HEAD_DIM = 128
MOBA_HEADS = 16
DSA_HEADS = 16
KV_HEADS = 4
GROUP = 4
MOBA_BLOCK = 256
MOBA_TOPK = 3
IDX_HEADS = 32
IDX_DIM = 128
DSA_TOPK_MAX = 256
ROPE_THETA = 10000.0
ATTN_SCALE = HEAD_DIM ** -0.5
IDX_W_SCALE = (IDX_HEADS ** -0.5) * (IDX_DIM ** -0.5)
PEER_HEADS = 8
PEER_N_KEYS = 128
PEER_DK = 256
PEER_TOPK = 16
NORM_EPS = 1e-6
PAST_LEN = 2048
PAGE_SIZE = 128

LANES = 128
SUBLANES = 8
NEG = -1e30
INT_MIN = -2 ** 31
VMEM_LIMIT = 56 * 1024 * 1024

C_QI = 0
C_QM = C_QI + IDX_HEADS * IDX_DIM
C_QD = C_QM + MOBA_HEADS * HEAD_DIM
C_KM = C_QD + DSA_HEADS * HEAD_DIM
C_VM = C_KM + KV_HEADS * HEAD_DIM
C_KD = C_VM + KV_HEADS * HEAD_DIM
C_VD = C_KD + KV_HEADS * HEAD_DIM
C_KI = C_VD + KV_HEADS * HEAD_DIM
C_WI = C_KI + IDX_DIM
PROJ_TN = 512
PROJ_PAD = 10752
KVW = KV_HEADS * HEAD_DIM


def _cparams(sem):
    return pltpu.CompilerParams(dimension_semantics=sem, vmem_limit_bytes=VMEM_LIMIT)


def _dot_nt(a, b):
    return lax.dot_general(a, b, (((1,), (1,)), ((), ())), preferred_element_type=F32)


def _dot(a, b):
    return jnp.dot(a, b, preferred_element_type=F32)


def _two_source_specs(tm, d, tiles_a):
    return [pl.BlockSpec((tm, d), lambda i, *_: (jnp.minimum(i, tiles_a - 1), 0)),
            pl.BlockSpec((tm, d), lambda i, *_: (jnp.maximum(i - tiles_a, 0), 0))]


def _rmsnorm_kernel(tiles_a, xa_ref, xb_ref, g_ref, o_ref):
    x = jnp.where(pl.program_id(0) < tiles_a, xa_ref[...], xb_ref[...])
    y = x * lax.rsqrt(jnp.mean(x * x, axis=-1, keepdims=True) + NORM_EPS)
    o_ref[...] = (y * g_ref[...]).astype(o_ref.dtype)


def _rmsnorm(xa, xb, g, out_dtype, tm):
    (na, d), nb = xa.shape, xb.shape[0]
    assert na % tm == 0 and nb % tm == 0
    return pl.pallas_call(
        functools.partial(_rmsnorm_kernel, na // tm),
        grid=((na + nb) // tm,),
        in_specs=_two_source_specs(tm, d, na // tm) + [pl.BlockSpec((1, d), lambda i: (0, 0))],
        out_specs=pl.BlockSpec((tm, d), lambda i: (i, 0)),
        out_shape=jax.ShapeDtypeStruct((na + nb, d), out_dtype),
        compiler_params=_cparams(("parallel",)),
        name="rmsnorm",
    )(xa, xb, g.reshape(1, d))


def _rope_group(a, cos, sin):
    return a * cos + pltpu.roll(a, HEAD_DIM // 2, axis=1) * sin


def _proj_kernel(src_ref, u_ref, w_ref, wt_ref, cos_ref, sin_ref, o_ref):
    j = pl.program_id(1)
    plain = jnp.logical_or(j == C_VM // PROJ_TN, j == C_VD // PROJ_TN)
    last = j == C_KI // PROJ_TN
    groups = PROJ_TN // HEAD_DIM

    @pl.when(plain)
    def _():
        o_ref[...] = _dot(u_ref[...], w_ref[...])

    @pl.when(jnp.logical_not(jnp.logical_or(plain, last)))
    def _():
        acc = _dot(u_ref[...], w_ref[...])
        cos = cos_ref[...]
        sin = sin_ref[...]
        for c in range(groups):
            o_ref[:, c * HEAD_DIM:(c + 1) * HEAD_DIM] = _rope_group(acc[:, c * HEAD_DIM:(c + 1) * HEAD_DIM], cos, sin)

    @pl.when(last)
    def _():
        acc = _dot(u_ref[...], wt_ref[...])
        o_ref[:, 0:HEAD_DIM] = _rope_group(acc[:, 0:HEAD_DIM], cos_ref[...], sin_ref[...])
        o_ref[:, HEAD_DIM:2 * HEAD_DIM] = acc[:, HEAD_DIM:2 * HEAD_DIM] * IDX_W_SCALE
        o_ref[:, 2 * HEAD_DIM:] = acc[:, 2 * HEAD_DIM:]


def _w_in_tile_order():
    model = dict(q_m=0, k_m=MOBA_HEADS * HEAD_DIM)
    model["v_m"] = model["k_m"] + KVW
    model["q_d"] = model["v_m"] + KVW
    model["k_d"] = model["q_d"] + DSA_HEADS * HEAD_DIM
    model["v_d"] = model["k_d"] + KVW
    model["q_i"] = model["v_d"] + KVW
    model["k_i"] = model["q_i"] + IDX_HEADS * IDX_DIM
    ours = (("q_i", C_QI, C_QM), ("q_m", C_QM, C_QD), ("q_d", C_QD, C_KM), ("k_m", C_KM, C_VM), ("v_m", C_VM, C_KD),
            ("k_d", C_KD, C_VD), ("v_d", C_VD, C_KI))
    order = []
    for name, lo, hi in ours:
        assert lo % PROJ_TN == 0 and model[name] % PROJ_TN == 0
        order += [model[name] // PROJ_TN + x for x in range((hi - lo) // PROJ_TN)]
    assert model["k_i"] == C_KI and PROJ_PAD - C_KI == PROJ_TN
    order.append(order[-1])
    return np.asarray(order, np.int32)


def _project(u, w, w_tail, cos, sin, tm):
    n, d = u.shape
    return pl.pallas_call(
        _proj_kernel,
        grid_spec=pltpu.PrefetchScalarGridSpec(
            num_scalar_prefetch=1,
            grid=(n // tm, PROJ_PAD // PROJ_TN),
            in_specs=[
                pl.BlockSpec((tm, d), lambda i, j, src: (i, 0)),
                pl.BlockSpec((d, PROJ_TN), lambda i, j, src: (0, src[j])),
                pl.BlockSpec((d, PROJ_TN), lambda i, j, src: (0, 0)),
                pl.BlockSpec((tm, HEAD_DIM), lambda i, j, src: (i, 0)),
                pl.BlockSpec((tm, HEAD_DIM), lambda i, j, src: (i, 0)),
            ],
            out_specs=pl.BlockSpec((tm, PROJ_TN), lambda i, j, src: (i, j)),
        ),
        out_shape=jax.ShapeDtypeStruct((n, PROJ_PAD), F32),
        compiler_params=_cparams(("parallel", "arbitrary")),
        name="proj_rope",
    )(jnp.asarray(_w_in_tile_order()), u, w, w_tail, cos, sin)


def _rope_tables(pos):
    half = HEAD_DIM // 2
    inv = 1.0 / (ROPE_THETA ** (jnp.arange(half, dtype=F32) * (2.0 / HEAD_DIM)))
    ang = pos.astype(F32)[:, None] * inv[None, :]
    c, s = jnp.cos(ang), jnp.sin(ang)
    return jnp.concatenate([c, c], axis=-1), jnp.concatenate([-s, s], axis=-1)


def _kmean_kernel(k_ref, o_ref):
    nb = o_ref.shape[0]
    k = k_ref[...].reshape(nb, MOBA_BLOCK, KVW)
    o_ref[...] = jnp.sum(k, axis=1) * (1.0 / MOBA_BLOCK)


def _kmean(proj, t):
    nb = t // MOBA_BLOCK
    return pl.pallas_call(
        _kmean_kernel,
        grid=(1,),
        in_specs=[pl.BlockSpec((t, KVW), lambda i: (0, C_KM // KVW))],
        out_specs=pl.BlockSpec((nb, KVW), lambda i: (0, 0)),
        out_shape=jax.ShapeDtypeStruct((nb, KVW), F32),
        compiler_params=_cparams(("arbitrary",)),
        name="moba_kmean",
    )(proj)


def _top_block_ids(gate, lane, n_sel):
    ids = []
    for _ in range(n_sel):
        mx = jnp.max(gate, axis=-1, keepdims=True)
        idx = jnp.min(jnp.where(gate == mx, lane, LANES), axis=-1, keepdims=True)
        ids.append(jnp.where(mx > -jnp.inf, idx, -1))
        gate = jnp.where(lane == idx, -jnp.inf, gate)
    return ids


def _sortable_key(x):
    bits = pltpu.bitcast(x + 0.0, jnp.int32)
    return bits ^ ((bits >> 31) & 0x7FFFFFFF)


def _kth_largest_key(count_ge, shape, k, two_bits=False):
    t = jnp.broadcast_to(jnp.where(count_ge(jnp.zeros(shape, jnp.int32)) >= k, 0, INT_MIN).astype(jnp.int32), shape)

    def one_bit(bit, t):
        cand = t + (jnp.int32(1) << bit)
        return jnp.where(count_ge(cand) >= k, cand, t)

    if not two_bits:
        return lax.fori_loop(0, 31, lambda it, t: one_bit(30 - it, t), t)

    def pair(it, t):
        step = jnp.int32(1) << (29 - 2 * it)
        c1, c2, c3 = t + step, t + 2 * step, t + 3 * step
        n1, n2, n3 = count_ge(c1), count_ge(c2), count_ge(c3)
        return jnp.where(n3 >= k, c3, jnp.where(n2 >= k, c2, jnp.where(n1 >= k, c1, t)))

    return one_bit(0, lax.fori_loop(0, 15, pair, t))


PTQ = 256
PTK = 512


def _prompt_attn_kernel(moba, q_ref, k_ref, v_ref, aux_ref, o_ref, qs_ref, m_ref, acc_ref, sel_ref):
    i = pl.program_id(0)
    j = pl.program_id(1)
    rows = GROUP * PTQ
    last_j = (i * PTQ + PTQ - 1) // PTK
    chunks = PTK // LANES

    @pl.when(j == 0)
    def _():
        for n in range(KV_HEADS):
            for g in range(GROUP):
                h = n * GROUP + g
                qs_ref[n, g * PTQ:(g + 1) * PTQ, :] = q_ref[:, h * HEAD_DIM:(h + 1) * HEAD_DIM].astype(BF16)
        m_ref[...] = jnp.full(m_ref.shape, NEG, F32)
        acc_ref[...] = jnp.zeros(acc_ref.shape, F32)
        if moba:
            blk = lax.broadcasted_iota(jnp.int32, (LANES, rows), 0)
            for n in range(KV_HEADS):
                km = aux_ref[:, n * HEAD_DIM:(n + 1) * HEAD_DIM].astype(BF16)
                gate = jnp.where(blk < i, _dot_nt(km, qs_ref[n]), -jnp.inf)
                bits = jnp.zeros((1, rows), jnp.int32)
                for _ in range(MOBA_TOPK):
                    mx = jnp.max(gate, axis=0, keepdims=True)
                    idx = jnp.min(jnp.where(gate == mx, blk, LANES), axis=0, keepdims=True)
                    bits = bits | jnp.where(mx > -jnp.inf, jnp.int32(1) << jnp.minimum(idx, 31), 0)
                    gate = jnp.where(blk == idx, -jnp.inf, gate)
                lo = jnp.broadcast_to((bits & 0xFFFF).astype(F32), (LANES, rows)).T
                hi = jnp.broadcast_to((bits >> 16).astype(F32), (LANES, rows)).T
                sel_ref[n] = lo.astype(jnp.int32) | (hi.astype(jnp.int32) << 16)

    def step(diagonal):
        k = k_ref[...].astype(BF16)
        v = v_ref[...].astype(BF16)
        ones = jnp.ones((PTK, HEAD_DIM), BF16)
        if moba and diagonal:
            lane = lax.broadcasted_iota(jnp.int32, (rows, LANES), 1)
            t_in = lax.broadcasted_iota(jnp.int32, (rows, LANES), 0) & (PTQ - 1)
        if not moba:
            bias = jnp.concatenate([aux_ref[...].T] * GROUP, axis=0)
        for n in range(KV_HEADS):
            s = _dot_nt(qs_ref[n], k[:, n * HEAD_DIM:(n + 1) * HEAD_DIM]) * ATTN_SCALE
            if moba and diagonal:
                bits = sel_ref[n]
                limits = []
                for b in range(PTK // MOBA_BLOCK):
                    kb = j * (PTK // MOBA_BLOCK) + b
                    picked = (bits >> jnp.minimum(kb, 31)) & 1
                    limits.append(jnp.where(kb == i, t_in, jnp.where(kb < i, picked * MOBA_BLOCK, 0) - 1))
                parts = []
                for c in range(chunks):
                    col = lane + (c * LANES) % MOBA_BLOCK
                    parts.append(jnp.where(col <= limits[(c * LANES) // MOBA_BLOCK],
                                           s[:, c * LANES:(c + 1) * LANES], NEG))
            elif moba:
                bits = sel_ref[n]
                row_bias = [jnp.where(((bits >> (j * (PTK // MOBA_BLOCK) + b)) & 1) == 1, 0.0, NEG)
                            for b in range(PTK // MOBA_BLOCK)]
                parts = [s[:, c * LANES:(c + 1) * LANES] + row_bias[(c * LANES) // MOBA_BLOCK] for c in range(chunks)]
            else:
                s = s + bias
                parts = [s[:, c * LANES:(c + 1) * LANES] for c in range(chunks)]
            mx = parts[0]
            for c in range(1, chunks):
                mx = jnp.maximum(mx, parts[c])
            m_prev = m_ref[n]
            m_new = jnp.maximum(m_prev, jnp.max(mx, axis=-1, keepdims=True))
            alpha = jnp.exp(m_prev - m_new)
            p = jnp.concatenate([jnp.exp(x - m_new) for x in parts], axis=1).astype(BF16)
            v_ones = jnp.concatenate([v[:, n * HEAD_DIM:(n + 1) * HEAD_DIM], ones], axis=1)
            acc_ref[n] = jnp.concatenate([alpha, alpha], axis=1) * acc_ref[n] + _dot(p, v_ones)
            m_ref[n] = m_new

    if moba:
        pl.when(j < last_j)(functools.partial(step, False))
        pl.when(j == last_j)(functools.partial(step, True))
    else:
        pl.when(j <= last_j)(functools.partial(step, False))

    @pl.when(j == last_j)
    def _():
        for n in range(KV_HEADS):
            for g in range(GROUP):
                h = n * GROUP + g
                r = slice(g * PTQ, (g + 1) * PTQ)
                o_ref[:, h * HEAD_DIM:(h + 1) * HEAD_DIM] = (acc_ref[n, r, 0:HEAD_DIM]
                                                             / acc_ref[n, r, HEAD_DIM:]).astype(o_ref.dtype)


def _prompt_attn(moba, proj, aux, t):
    nq, nk = t // PTQ, t // PTK
    assert t // MOBA_BLOCK <= 32
    qw = MOBA_HEADS * HEAD_DIM
    c_q, c_k, c_v = (C_QM, C_KM, C_VM) if moba else (C_QD, C_KD, C_VD)
    jj = lambda i, j: jnp.minimum(j, (i * PTQ + PTQ - 1) // PTK)
    if moba:
        aux_spec = pl.BlockSpec((LANES, KVW), lambda i, j: (0, 0))
    else:
        aux_spec = pl.BlockSpec((PTK, PTQ), lambda i, j: (jj(i, j), i))
    rows = GROUP * PTQ
    return pl.pallas_call(
        functools.partial(_prompt_attn_kernel, moba),
        grid=(nq, nk),
        in_specs=[
            pl.BlockSpec((PTQ, qw), lambda i, j: (i, c_q // qw)),
            pl.BlockSpec((PTK, KVW), lambda i, j: (jj(i, j), c_k // KVW)),
            pl.BlockSpec((PTK, KVW), lambda i, j: (jj(i, j), c_v // KVW)),
            aux_spec,
        ],
        out_specs=pl.BlockSpec((PTQ, qw), lambda i, j: (i, 0)),
        out_shape=jax.ShapeDtypeStruct((t, qw), BF16),
        scratch_shapes=[
            pltpu.VMEM((KV_HEADS, rows, HEAD_DIM), BF16),
            pltpu.VMEM((KV_HEADS, rows, LANES), F32),
            pltpu.VMEM((KV_HEADS, rows, HEAD_DIM + LANES), F32),
            pltpu.VMEM((KV_HEADS, rows, LANES), jnp.int32),
        ],
        compiler_params=_cparams(("parallel", "arbitrary")),
        name="moba_prompt" if moba else "dsa_prompt",
    )(proj, proj, proj, aux)


DTK = 512


COUNT_ROWS = 32
IDX_HEAD_GROUP = 4


def _dsa_select_kernel(k_keep, qi_ref, ki_ref, wt_ref, bias_ref, qs_ref, key_ref):
    i = pl.program_id(0)
    j = pl.program_id(1)
    nj = pl.num_programs(1)
    length = key_ref.shape[0]

    @pl.when(j == 0)
    def _():
        qs_ref[...] = qi_ref[...].astype(BF16)
        key_ref[...] = jnp.full(key_ref.shape, INT_MIN, jnp.int32)

    @pl.when(j * DTK <= i * PTQ + PTQ - 1)
    def _():
        kb = ki_ref[...].astype(BF16)
        acc = jnp.zeros((DTK, PTQ), F32)
        for h0 in range(0, IDX_HEADS, IDX_HEAD_GROUP):
            part = None
            for h in range(h0, h0 + IDX_HEAD_GROUP):
                s = _dot_nt(kb, qs_ref[:, h * IDX_DIM:(h + 1) * IDX_DIM])
                term = wt_ref[h:h + 1, :] * jnp.maximum(s, 0.0)
                part = term if part is None else part + term
            acc = acc + part
        key_pos = j * DTK + lax.broadcasted_iota(jnp.int32, (DTK, PTQ), 0)
        q_pos = i * PTQ + lax.broadcasted_iota(jnp.int32, (DTK, PTQ), 1)
        x = jnp.where(key_pos <= q_pos, acc, -jnp.inf)
        key_ref[pl.ds(pl.multiple_of(j * DTK, DTK), DTK), :] = _sortable_key(x)

    @pl.when(j == nj - 1)
    def _():
        n_tiles = (i * PTQ + PTQ - 1) // DTK + 1

        def count_ge(cand):
            def tile(c, acc):
                keys = key_ref[pl.ds(pl.multiple_of(c * DTK, DTK), DTK), :]
                hit = jnp.where(keys.reshape(DTK // COUNT_ROWS, COUNT_ROWS, PTQ) >= cand, 1, 0)
                return acc + jnp.sum(hit, axis=0)

            acc = lax.fori_loop(0, n_tiles, tile, jnp.zeros((COUNT_ROWS, PTQ), jnp.int32))
            return jnp.sum(acc, axis=0, keepdims=True)

        thr = _kth_largest_key(count_ge, (1, PTQ), k_keep)
        key_pos = lax.broadcasted_iota(jnp.int32, (length, PTQ), 0)
        q_pos = i * PTQ + lax.broadcasted_iota(jnp.int32, (length, PTQ), 1)
        ok = jnp.logical_and(key_ref[...] >= thr, key_pos <= q_pos)
        bias_ref[...] = jnp.where(ok, 0.0, NEG)


def _dsa_select(proj, t):
    nq, nj = t // PTQ, t // DTK
    k_keep = max(1, min(DSA_TOPK_MAX, t // 4))
    qw = IDX_HEADS * IDX_DIM
    last_j = lambda i: (i * PTQ + PTQ - 1) // DTK
    w_t = proj[:t, C_WI:C_WI + IDX_HEADS].T
    return pl.pallas_call(
        functools.partial(_dsa_select_kernel, k_keep),
        grid=(nq, nj),
        in_specs=[
            pl.BlockSpec((PTQ, qw), lambda i, j: (i, C_QI // qw)),
            pl.BlockSpec((DTK, IDX_DIM), lambda i, j: (jnp.minimum(j, last_j(i)), C_KI // IDX_DIM)),
            pl.BlockSpec((IDX_HEADS, PTQ), lambda i, j: (0, i)),
        ],
        out_specs=pl.BlockSpec((t, PTQ), lambda i, j: (0, i)),
        out_shape=jax.ShapeDtypeStruct((t, t), F32),
        scratch_shapes=[pltpu.VMEM((PTQ, qw), BF16), pltpu.VMEM((t, PTQ), jnp.int32)],
        compiler_params=_cparams(("parallel", "arbitrary")),
        name="dsa_select",
    )(proj, proj, w_t)


N_PAGES = PAST_LEN // PAGE_SIZE
PAGES_PER_BLOCK = MOBA_BLOCK // PAGE_SIZE
PAGE_ROWS = PAGE_SIZE * KV_HEADS


def _head_match(rows, cols, dec):
    r = lax.broadcasted_iota(jnp.int32, (rows, cols), 0)
    c = lax.broadcasted_iota(jnp.int32, (rows, cols), 1)
    shift = (GROUP * dec).bit_length() - 1
    return (r >> shift) == (c & (KV_HEADS - 1))


def _stage_new(dst_ref, src_ref):
    dst_ref[...] = jnp.zeros(dst_ref.shape, F32)
    dst_ref[0:src_ref.shape[0], :] = src_ref[...]


def _softmax_over_pages(s_ref, s_new, v_refs, nv_ref, o_ref):
    m = jnp.max(s_new, axis=-1, keepdims=True)
    for p in range(N_PAGES):
        m = jnp.maximum(m, jnp.max(s_ref[p], axis=-1, keepdims=True))
    pn = jnp.exp(s_new - m)
    l = jnp.sum(pn, axis=-1, keepdims=True)
    acc = _dot(pn.astype(BF16), nv_ref[...].astype(BF16))
    for p in range(N_PAGES):
        pr = jnp.exp(s_ref[p] - m)
        l = l + jnp.sum(pr, axis=-1, keepdims=True)
        acc = acc + _dot(pr.astype(BF16), v_refs[p][...].astype(BF16))
    o_ref[...] = acc / l


def _sample_moba_kernel(pt_ref, q_ref, kn_ref, vn_ref, *rest):
    k_refs, v_refs = rest[:N_PAGES], rest[N_PAGES:2 * N_PAGES]
    o_ref, s_ref, nk_ref, nv_ref = rest[2 * N_PAGES:]
    rows = q_ref.shape[0]
    dec = rows // MOBA_HEADS
    n_past_blocks = PAST_LEN // MOBA_BLOCK
    q = q_ref[...].astype(BF16)
    match = _head_match(rows, PAGE_ROWS, dec)
    lane = lax.broadcasted_iota(jnp.int32, (rows, LANES), 1)

    for p in range(N_PAGES):
        s_ref[p] = _dot_nt(q, k_refs[p][...].astype(BF16))

    gate = jnp.full((rows, LANES), -jnp.inf, F32)
    for b in range(n_past_blocks):
        tot = s_ref[PAGES_PER_BLOCK * b]
        for c in range(1, PAGES_PER_BLOCK):
            tot = tot + s_ref[PAGES_PER_BLOCK * b + c]
        g = jnp.sum(jnp.where(match, tot, 0.0), axis=-1, keepdims=True) * (1.0 / MOBA_BLOCK)
        gate = jnp.where(lane == b, g, gate)
    ids = _top_block_ids(gate, lane, MOBA_TOPK)

    for b in range(n_past_blocks):
        picked = ids[0] == b
        for idx in ids[1:]:
            picked = jnp.logical_or(picked, idx == b)
        row_bias = jnp.where(picked, 0.0, NEG)
        for c in range(PAGES_PER_BLOCK):
            p = PAGES_PER_BLOCK * b + c
            s_ref[p] = jnp.where(match, s_ref[p] * ATTN_SCALE + row_bias, NEG)

    _stage_new(nk_ref, kn_ref)
    _stage_new(nv_ref, vn_ref)
    rn = lax.broadcasted_iota(jnp.int32, (rows, LANES), 0)
    ok = jnp.logical_and(_head_match(rows, LANES, dec), (lane >> 2) <= (rn & (dec - 1)))
    s_new = jnp.where(ok, _dot_nt(q, nk_ref[...].astype(BF16)) * ATTN_SCALE, NEG)
    _softmax_over_pages(s_ref, s_new, v_refs, nv_ref, o_ref)


def _page_specs(shape, n):
    return [pl.BlockSpec((None,) + shape, functools.partial(lambda b, pt, p: (pt[b, p], 0, 0), p=p)) for p in range(n)]


def _sample_moba(pt, q, kn, vn, ck, cv):
    nb, rows, _ = q.shape
    dec = rows // MOBA_HEADS
    assert dec & (dec - 1) == 0 and dec * KV_HEADS <= LANES and KV_HEADS == 4
    seq = lambda b, pt: (b, 0, 0)
    return pl.pallas_call(
        _sample_moba_kernel,
        grid_spec=pltpu.PrefetchScalarGridSpec(
            num_scalar_prefetch=1,
            grid=(nb,),
            in_specs=[pl.BlockSpec((None, rows, HEAD_DIM), seq),
                      pl.BlockSpec((None, dec * KV_HEADS, HEAD_DIM), seq),
                      pl.BlockSpec((None, dec * KV_HEADS, HEAD_DIM), seq)]
                     + _page_specs((PAGE_ROWS, HEAD_DIM), N_PAGES) + _page_specs((PAGE_ROWS, HEAD_DIM), N_PAGES),
            out_specs=pl.BlockSpec((None, rows, HEAD_DIM), seq),
            scratch_shapes=[
                pltpu.VMEM((N_PAGES, rows, PAGE_ROWS), F32),
                pltpu.VMEM((LANES, HEAD_DIM), F32),
                pltpu.VMEM((LANES, HEAD_DIM), F32),
            ],
        ),
        out_shape=jax.ShapeDtypeStruct((nb, rows, HEAD_DIM), F32),
        compiler_params=_cparams(("parallel",)),
        name="moba_sample",
    )(pt, q, kn, vn, *([ck] * N_PAGES), *([cv] * N_PAGES))


def _sample_dsa_kernel(k_keep, pt_ref, qi_ref, wi_ref, q_ref, kin_ref, kn_ref, vn_ref, *rest):
    i_refs, k_refs, v_refs = rest[:N_PAGES], rest[N_PAGES:2 * N_PAGES], rest[2 * N_PAGES:3 * N_PAGES]
    o_ref, idx_ref, s_ref, nki_ref, nk_ref, nv_ref = rest[3 * N_PAGES:]
    rows = q_ref.shape[0]
    dec = rows // DSA_HEADS
    qi = qi_ref[...].astype(BF16)
    w = wi_ref[...]

    def index_scores(kpage):
        sc = jnp.maximum(_dot_nt(qi, kpage.astype(BF16)), 0.0) * w
        return jnp.sum(sc.reshape(dec, IDX_HEADS, PAGE_SIZE), axis=1)

    idx_ref[...] = jnp.full(idx_ref.shape, -jnp.inf, F32)
    for p in range(N_PAGES):
        idx_ref[p, 0:dec, :] = index_scores(i_refs[p][...])
    _stage_new(nki_ref, kin_ref)
    t_in = lax.broadcasted_iota(jnp.int32, (dec, PAGE_SIZE), 0)
    col = lax.broadcasted_iota(jnp.int32, (dec, PAGE_SIZE), 1)
    idx_ref[N_PAGES, 0:dec, :] = jnp.where(col <= t_in, index_scores(nki_ref[...]), -jnp.inf)

    keys = _sortable_key(idx_ref[...])

    def count_ge(cand):
        c = jnp.sum((keys >= cand).astype(jnp.int32), axis=0)
        return jnp.sum(c, axis=-1, keepdims=True)[None]

    thr = _kth_largest_key(count_ge, (1, SUBLANES, 1), k_keep, two_bits=True)
    keep = jnp.where(jnp.logical_and(keys >= thr, idx_ref[...] > -jnp.inf), 1.0, 0.0)

    q = q_ref[...].astype(BF16)
    match = _head_match(rows, PAGE_ROWS, dec)
    er = lax.broadcasted_iota(jnp.int32, (PAGE_SIZE, PAGE_ROWS), 0)
    ec = lax.broadcasted_iota(jnp.int32, (PAGE_SIZE, PAGE_ROWS), 1)
    expand = jnp.where((ec >> 2) == er, 1.0, 0.0).astype(BF16)

    def visible(page, width):
        per_row = jnp.concatenate([keep[page, 0:dec, :]] * (rows // dec), axis=0).astype(BF16)
        return _dot(per_row, expand[:, 0:width]) > 0.5

    for p in range(N_PAGES):
        sc = _dot_nt(q, k_refs[p][...].astype(BF16)) * ATTN_SCALE
        s_ref[p] = jnp.where(jnp.logical_and(match, visible(p, PAGE_ROWS)), sc, NEG)
    _stage_new(nk_ref, kn_ref)
    _stage_new(nv_ref, vn_ref)
    ok = jnp.logical_and(_head_match(rows, LANES, dec), visible(N_PAGES, LANES))
    s_new = jnp.where(ok, _dot_nt(q, nk_ref[...].astype(BF16)) * ATTN_SCALE, NEG)
    _softmax_over_pages(s_ref, s_new, v_refs, nv_ref, o_ref)


def _sample_dsa(pt, qi, wi, q, kin, kn, vn, ci, ck, cv):
    nb, rows, _ = q.shape
    dec = rows // DSA_HEADS
    assert dec & (dec - 1) == 0 and dec <= SUBLANES and KV_HEADS == 4
    k_keep = max(1, min(DSA_TOPK_MAX, (PAST_LEN + dec) // 4))
    seq = lambda b, pt: (b, 0, 0)
    return pl.pallas_call(
        functools.partial(_sample_dsa_kernel, k_keep),
        grid_spec=pltpu.PrefetchScalarGridSpec(
            num_scalar_prefetch=1,
            grid=(nb,),
            in_specs=[pl.BlockSpec((None, dec * IDX_HEADS, IDX_DIM), seq),
                      pl.BlockSpec((None, dec * IDX_HEADS, 1), seq),
                      pl.BlockSpec((None, rows, HEAD_DIM), seq),
                      pl.BlockSpec((None, dec, IDX_DIM), seq),
                      pl.BlockSpec((None, dec * KV_HEADS, HEAD_DIM), seq),
                      pl.BlockSpec((None, dec * KV_HEADS, HEAD_DIM), seq)]
                     + _page_specs((PAGE_SIZE, IDX_DIM), N_PAGES)
                     + _page_specs((PAGE_ROWS, HEAD_DIM), N_PAGES) + _page_specs((PAGE_ROWS, HEAD_DIM), N_PAGES),
            out_specs=pl.BlockSpec((None, rows, HEAD_DIM), seq),
            scratch_shapes=[
                pltpu.VMEM((N_PAGES + 1, SUBLANES, PAGE_SIZE), F32),
                pltpu.VMEM((N_PAGES, rows, PAGE_ROWS), F32),
                pltpu.VMEM((PAGE_SIZE, IDX_DIM), F32),
                pltpu.VMEM((LANES, HEAD_DIM), F32),
                pltpu.VMEM((LANES, HEAD_DIM), F32),
            ],
        ),
        out_shape=jax.ShapeDtypeStruct((nb, rows, HEAD_DIM), F32),
        compiler_params=_cparams(("parallel",)),
        name="dsa_sample",
    )(pt, qi, wi, q, kin, kn, vn, *([ci] * N_PAGES), *([ck] * N_PAGES), *([cv] * N_PAGES))


def _outproj_kernel(tiles_a, om_ref, od_ref, ms_ref, w_ref, xa_ref, xb_ref, h_ref, acc_ref):
    i = pl.program_id(0)
    k = pl.program_id(2)
    nk = pl.num_programs(2)
    first = i < tiles_a

    @pl.when(k == 0)
    def _():
        acc_ref[...] = jnp.zeros(acc_ref.shape, F32)

    mix = jnp.where(first, jnp.where(k < nk // 2, om_ref[...], od_ref[...]), ms_ref[...])
    acc_ref[...] += _dot(mix, w_ref[...])

    @pl.when(k == nk - 1)
    def _():
        h_ref[...] = jnp.where(first, xa_ref[...], xb_ref[...]) + acc_ref[...]


def _outproj(o_m, o_d, mix_b, w, xa, xb, tm, tn, tk):
    (na, d), nb = xa.shape, xb.shape[0]
    n = na + nb
    nk = d // tk
    half = nk // 2
    tiles_a = na // tm
    assert na % tm == 0 and nb % tm == 0 and nk % 2 == 0 and o_m.shape[1] == half * tk and d % tn == 0
    row_a = lambda i: jnp.minimum(i, tiles_a - 1)
    row_b = lambda i: jnp.maximum(i - tiles_a, 0)
    col_a = lambda i, x: jnp.where(i < tiles_a, x, 0)
    col_b = lambda i, x: jnp.where(i < tiles_a, 0, x)
    return pl.pallas_call(
        functools.partial(_outproj_kernel, tiles_a),
        grid=(n // tm, d // tn, nk),
        in_specs=[
            pl.BlockSpec((tm, tk), lambda i, c, k: (row_a(i), col_a(i, jnp.minimum(k, half - 1)))),
            pl.BlockSpec((tm, tk), lambda i, c, k: (row_a(i), col_a(i, jnp.maximum(k - half, 0)))),
            pl.BlockSpec((tm, tk), lambda i, c, k: (row_b(i), col_b(i, k))),
            pl.BlockSpec((tk, tn), lambda i, c, k: (k, c)),
            pl.BlockSpec((tm, tn), lambda i, c, k: (row_a(i), col_a(i, c))),
            pl.BlockSpec((tm, tn), lambda i, c, k: (row_b(i), col_b(i, c))),
        ],
        out_specs=pl.BlockSpec((tm, tn), lambda i, c, k: (i, c)),
        out_shape=jax.ShapeDtypeStruct((n, d), F32),
        scratch_shapes=[pltpu.VMEM((tm, tn), F32)],
        compiler_params=_cparams(("parallel", "parallel", "arbitrary")),
        name="outproj",
    )(o_m, o_d, mix_b, w, xa, xb)


def _rmsnorm_rows_kernel(x_ref, g_ref, o_ref):
    x = x_ref[...]
    y = x * lax.rsqrt(jnp.mean(x * x, axis=-1, keepdims=True) + NORM_EPS)
    o_ref[...] = (y * g_ref[...]).astype(o_ref.dtype)


def _rmsnorm_rows(x, g, out_dtype, tm):
    n, d = x.shape
    return pl.pallas_call(
        _rmsnorm_rows_kernel,
        grid=(n // tm,),
        in_specs=[pl.BlockSpec((tm, d), lambda i: (i, 0)), pl.BlockSpec((1, d), lambda i: (0, 0))],
        out_specs=pl.BlockSpec((tm, d), lambda i: (i, 0)),
        out_shape=jax.ShapeDtypeStruct((n, d), out_dtype),
        compiler_params=_cparams(("parallel",)),
        name="rmsnorm_ffn",
    )(x, g.reshape(1, d))


def _top_rows(x, k):
    r = x.shape[0]
    ridx = lax.broadcasted_iota(jnp.int32, x.shape, 0)
    outs = []
    for _ in range(k):
        mx = jnp.max(x, axis=0, keepdims=True)
        idx = jnp.min(jnp.where(x == mx, ridx, r), axis=0, keepdims=True)
        outs.append(mx)
        x = jnp.where(ridx == idx, -1.0, x)
    return jnp.concatenate(outs, axis=0)


def _peer_select_kernel(h2_ref, wq_ref, sk_ref, e_ref, thr_ref, iz_ref, acc_ref):
    k = pl.program_id(1)

    @pl.when(k == 0)
    def _():
        acc_ref[...] = jnp.zeros(acc_ref.shape, F32)

    acc_ref[...] += _dot(h2_ref[...], wq_ref[...])

    @pl.when(k == pl.num_programs(1) - 1)
    def _():
        half = PEER_DK // 2
        hk = PEER_TOPK // 2
        for h in range(PEER_HEADS):
            tops = []
            for p in range(2):
                c0 = (h * 2 + p) * half
                q = acc_ref[:, c0:c0 + half].astype(BF16)
                st = _dot_nt(sk_ref[h, p], q)
                e = jnp.exp(st - jnp.max(st, axis=0, keepdims=True))
                e_ref[h, p] = e
                tops.append(_top_rows(e, PEER_TOPK))
            a, b = tops
            cand = [a[x:x + 1, :] * b[0:hk, :] for x in range(hk)]
            cand += [a[0:1, :] * b[hk:, :], a[hk:, :] * b[0:1, :]]
            top = _top_rows(jnp.concatenate(cand, axis=0), PEER_TOPK)
            thr_ref[h:h + 1, :] = top[PEER_TOPK - 1:PEER_TOPK, :]
            iz_ref[h:h + 1, :] = 1.0 / jnp.sum(top, axis=0, keepdims=True)


def _peer_select(h2, wq, sk, tm, tk):
    n, d = h2.shape
    qw = wq.shape[1]
    return pl.pallas_call(
        _peer_select_kernel,
        grid=(n // tm, d // tk),
        in_specs=[
            pl.BlockSpec((tm, tk), lambda i, k: (i, k)),
            pl.BlockSpec((tk, qw), lambda i, k: (k, 0)),
            pl.BlockSpec(sk.shape, lambda i, k: (0, 0, 0, 0)),
        ],
        out_specs=[
            pl.BlockSpec((PEER_HEADS, 2, PEER_N_KEYS, tm), lambda i, k: (0, 0, 0, i)),
            pl.BlockSpec((PEER_HEADS, tm), lambda i, k: (0, i)),
            pl.BlockSpec((PEER_HEADS, tm), lambda i, k: (0, i)),
        ],
        out_shape=[
            jax.ShapeDtypeStruct((PEER_HEADS, 2, PEER_N_KEYS, n), F32),
            jax.ShapeDtypeStruct((PEER_HEADS, n), F32),
            jax.ShapeDtypeStruct((PEER_HEADS, n), F32),
        ],
        scratch_shapes=[pltpu.VMEM((tm, qw), F32)],
        compiler_params=_cparams(("parallel", "arbitrary")),
        name="peer_select",
    )(h2, wq, sk)


PEER_SUB = 256


def _peer_dense_kernel(h2_ref, u_ref, v_ref, e_ref, thr_ref, iz_ref, o_ref):
    e = pl.program_id(1)
    te = u_ref.shape[0]
    tm = h2_ref.shape[0]

    @pl.when(e == 0)
    def _():
        o_ref[...] = jnp.zeros(o_ref.shape, F32)

    h2 = h2_ref[...]
    total = None
    for sub in range(te // PEER_SUB):
        r0 = sub * PEER_SUB
        at = _dot_nt(u_ref[r0:r0 + PEER_SUB, :], h2)
        act = 0.5 * at * (1.0 + lax.erf(at * (2.0 ** -0.5)))
        parts = []
        for c in range(PEER_SUB // PEER_N_KEYS):
            i = (e * te + r0) // PEER_N_KEYS + c
            g = jnp.zeros((PEER_N_KEYS, tm), F32)
            for h in range(PEER_HEADS):
                p = e_ref[h, 0, pl.ds(i, 1), :] * e_ref[h, 1]
                g = g + jnp.where(p >= thr_ref[h:h + 1, :], p, 0.0) * iz_ref[h:h + 1, :]
            parts.append(g)
        ga = (jnp.concatenate(parts, axis=0) * act).astype(BF16)
        d = lax.dot_general(ga, v_ref[r0:r0 + PEER_SUB, :], (((0,), (0,)), ((), ())), preferred_element_type=F32)
        total = d if total is None else total + d
    o_ref[...] += total


def _peer_dense(h2, u, v, e, thr, iz, tm, te):
    n, d = h2.shape
    ne = u.shape[0]
    return pl.pallas_call(
        _peer_dense_kernel,
        grid=(n // tm, ne // te),
        in_specs=[
            pl.BlockSpec((tm, d), lambda i, k: (i, 0)),
            pl.BlockSpec((te, d), lambda i, k: (k, 0)),
            pl.BlockSpec((te, d), lambda i, k: (k, 0)),
            pl.BlockSpec((PEER_HEADS, 2, PEER_N_KEYS, tm), lambda i, k: (0, 0, 0, i)),
            pl.BlockSpec((PEER_HEADS, tm), lambda i, k: (0, i)),
            pl.BlockSpec((PEER_HEADS, tm), lambda i, k: (0, i)),
        ],
        out_specs=pl.BlockSpec((tm, d), lambda i, k: (i, 0)),
        out_shape=jax.ShapeDtypeStruct((n, d), F32),
        compiler_params=_cparams(("parallel", "arbitrary")),
        name="peer_dense",
    )(h2, u, v, e, thr, iz)


def _final_kernel(h_ref, f_ref, g_ref, o_ref):
    x = h_ref[...] + f_ref[...]
    y = x * lax.rsqrt(jnp.mean(x * x, axis=-1, keepdims=True) + NORM_EPS)
    o_ref[...] = y * g_ref[...]


def _final(h, f, g, row0, rows, tm):
    d = h.shape[1]
    assert row0 % tm == 0 and rows % tm == 0
    src = pl.BlockSpec((tm, d), lambda i: (i + row0 // tm, 0))
    return pl.pallas_call(
        _final_kernel,
        grid=(rows // tm,),
        in_specs=[src, src, pl.BlockSpec((1, d), lambda i: (0, 0))],
        out_specs=pl.BlockSpec((tm, d), lambda i: (i, 0)),
        out_shape=jax.ShapeDtypeStruct((rows, d), F32),
        compiler_params=_cparams(("parallel",)),
        name="final_norm",
    )(h, f, g.reshape(1, d))


def _heads_to_rows(q, nb, dec):
    return q.reshape(nb, dec, MOBA_HEADS, HEAD_DIM).transpose(0, 2, 1, 3).reshape(nb, MOBA_HEADS * dec, HEAD_DIM)


def _rows_to_heads(o, nb, dec):
    return o.reshape(nb, MOBA_HEADS, dec, HEAD_DIM).transpose(0, 2, 1, 3).reshape(nb * dec, MOBA_HEADS * HEAD_DIM)


def _sample_mix(proj_s, page_table, c_mk, c_mv, c_dk, c_dv, c_ik, nb, dec):
    ps = proj_s.reshape(nb, dec, PROJ_PAD)
    seg = lambda c, w: ps[:, :, c:c + w]
    kv_rows = lambda c: seg(c, KVW).reshape(nb, dec * KV_HEADS, HEAD_DIM)
    s_m = _sample_moba(page_table, _heads_to_rows(seg(C_QM, MOBA_HEADS * HEAD_DIM), nb, dec),
                       kv_rows(C_KM), kv_rows(C_VM), c_mk, c_mv)
    qi = seg(C_QI, IDX_HEADS * IDX_DIM).reshape(nb, dec * IDX_HEADS, IDX_DIM)
    wi = seg(C_WI, IDX_HEADS).reshape(nb, dec * IDX_HEADS, 1)
    s_d = _sample_dsa(page_table, qi, wi, _heads_to_rows(seg(C_QD, DSA_HEADS * HEAD_DIM), nb, dec),
                      seg(C_KI, IDX_DIM), kv_rows(C_KD), kv_rows(C_VD), c_ik, c_dk, c_dv)
    return jnp.concatenate([_rows_to_heads(s_m, nb, dec), _rows_to_heads(s_d, nb, dec)], axis=-1)


def _prompt_mix(proj, t):
    kmean = _kmean(proj, t)
    kmean = jnp.pad(kmean, ((0, LANES - kmean.shape[0]), (0, 0)))
    o_m = _prompt_attn(True, proj, kmean, t)
    o_d = _prompt_attn(False, proj, _dsa_select(proj, t), t)
    return o_m, o_d


def kernel(x_prompt, x_sample, cache_moba_k, cache_moba_v, cache_dsa_k, cache_dsa_v, cache_idx_k, page_table,
           norm_mix, w_in, w_out, norm_ffn, peer_w_q, peer_subkeys, peer_u, peer_v, norm_final):
    depth = w_in.shape[0]
    assert depth == 1 and x_prompt.shape[0] == 1
    assert PAST_LEN % MOBA_BLOCK == 0 and page_table.shape[1] == N_PAGES
    t = x_prompt.shape[1]
    nb, dec, d = x_sample.shape
    n_tok = t + nb * dec
    tm = 512
    assert n_tok % tm == 0 and t % tm == 0

    x_p = x_prompt.reshape(t, d)
    x_s = x_sample.reshape(nb * dec, d)
    pos = jnp.concatenate([jnp.arange(t, dtype=jnp.int32),
                           PAST_LEN + jnp.tile(jnp.arange(dec, dtype=jnp.int32), nb)])
    cos, sin = _rope_tables(pos)

    n_pool = cache_moba_k.shape[1]
    flat = lambda c: c[0].reshape(n_pool, PAGE_ROWS, HEAD_DIM)

    u = _rmsnorm(x_p, x_s, norm_mix[0], BF16, tm)
    tm_proj = n_tok // 8 if (n_tok // 8) % 16 == 0 and n_tok % 8 == 0 else tm
    w_tail = w_in[0][:, C_KI:]
    w_tail = jnp.pad(w_tail, ((0, 0), (0, PROJ_TN - w_tail.shape[1]))).astype(BF16)
    proj = _project(u, w_in[0].astype(BF16), w_tail, cos, sin, tm_proj)
    o_m, o_d = _prompt_mix(proj, t)
    mix_s = _sample_mix(proj[t:], page_table, flat(cache_moba_k), flat(cache_moba_v), flat(cache_dsa_k),
                        flat(cache_dsa_v), cache_idx_k[0], nb, dec)
    h = _outproj(o_m, o_d, mix_s.astype(BF16), w_out[0].astype(BF16), x_p, x_s, tm, d // 2, 512)
    h2 = _rmsnorm_rows(h, norm_ffn[0], BF16, tm)
    e, thr, iz = _peer_select(h2, peer_w_q[0].astype(BF16), peer_subkeys[0].astype(BF16), tm, 512)
    f = _peer_dense(h2, peer_u[0].astype(BF16), peer_v[0].astype(BF16), e, thr, iz, tm, 512)

    y_prompt = _final(h, f, norm_final, 0, t, 256).reshape(1, t, d)
    y_sample = _final(h, f, norm_final, t, nb * dec, 256).reshape(nb, dec, d)
    pp = proj[:t]
    ps = proj[t:]
    kv = lambda a, c: a[:, c:c + KVW]
    outs_p = [kv(pp, c).reshape(1, 1, t, KV_HEADS, HEAD_DIM) for c in (C_KM, C_VM, C_KD, C_VD)]
    outs_p.append(pp[:, C_KI:C_KI + IDX_DIM].reshape(1, 1, t, IDX_DIM))
    outs_s = [kv(ps, c).reshape(1, nb, dec, KV_HEADS, HEAD_DIM) for c in (C_KM, C_VM, C_KD, C_VD)]
    outs_s.append(ps[:, C_KI:C_KI + IDX_DIM].reshape(1, nb, dec, IDX_DIM))
    return (y_prompt, y_sample, *outs_p, *outs_s)
```

```python
import functools

import numpy as np
import jax
import jax.numpy as jnp
from jax import lax
from jax.experimental import pallas as pl
from jax.experimental.pallas import tpu as pltpu

F32 = jnp.float32
BF16 = jnp.bfloat16

D_MODEL = 4096
HEAD_DIM = 128
MOBA_HEADS = 16
DSA_HEADS = 16
KV_HEADS = 4
GROUP = 4
MOBA_BLOCK = 256
MOBA_TOPK = 3
IDX_HEADS = 32
IDX_DIM = 128
DSA_TOPK_MAX = 256
ROPE_THETA = 10000.0
ATTN_SCALE = HEAD_DIM ** -0.5
IDX_W_SCALE = (IDX_HEADS ** -0.5) * (IDX_DIM ** -0.5)
PEER_HEADS = 8
PEER_N_KEYS = 128
PEER_DK = 256
PEER_TOPK = 16
NORM_EPS = 1e-6
PAST_LEN = 2048
PAGE_SIZE = 128

LANES = 128
SUBLANES = 8
NEG = -1e30
INT_MIN = -2 ** 31
VMEM_LIMIT = 56 * 1024 * 1024

C_QI = 0
C_QM = C_QI + IDX_HEADS * IDX_DIM
C_QD = C_QM + MOBA_HEADS * HEAD_DIM
C_KM = C_QD + DSA_HEADS * HEAD_DIM
C_VM = C_KM + KV_HEADS * HEAD_DIM
C_KD = C_VM + KV_HEADS * HEAD_DIM
C_VD = C_KD + KV_HEADS * HEAD_DIM
C_KI = C_VD + KV_HEADS * HEAD_DIM
C_WI = C_KI + IDX_DIM
PROJ_TN = 512
PROJ_PAD = 10752
KVW = KV_HEADS * HEAD_DIM


def _cparams(sem):
    return pltpu.CompilerParams(dimension_semantics=sem, vmem_limit_bytes=VMEM_LIMIT)


def _dot_nt(a, b):
    return lax.dot_general(a, b, (((1,), (1,)), ((), ())), preferred_element_type=F32)


def _dot(a, b):
    return jnp.dot(a, b, preferred_element_type=F32)


def _two_source_specs(tm, d, tiles_a):
    return [pl.BlockSpec((tm, d), lambda i, *_: (jnp.minimum(i, tiles_a - 1), 0)),
            pl.BlockSpec((tm, d), lambda i, *_: (jnp.maximum(i - tiles_a, 0), 0))]


def _rmsnorm_kernel(tiles_a, xa_ref, xb_ref, g_ref, o_ref):
    x = jnp.where(pl.program_id(0) < tiles_a, xa_ref[...], xb_ref[...])
    y = x * lax.rsqrt(jnp.mean(x * x, axis=-1, keepdims=True) + NORM_EPS)
    o_ref[...] = (y * g_ref[...]).astype(o_ref.dtype)


def _rmsnorm(xa, xb, g, out_dtype, tm):
    (na, d), nb = xa.shape, xb.shape[0]
    assert na % tm == 0 and nb % tm == 0
    return pl.pallas_call(
        functools.partial(_rmsnorm_kernel, na // tm),
        grid=((na + nb) // tm,),
        in_specs=_two_source_specs(tm, d, na // tm) + [pl.BlockSpec((1, d), lambda i: (0, 0))],
        out_specs=pl.BlockSpec((tm, d), lambda i: (i, 0)),
        out_shape=jax.ShapeDtypeStruct((na + nb, d), out_dtype),
        compiler_params=_cparams(("parallel",)),
        name="rmsnorm",
    )(xa, xb, g.reshape(1, d))


def _rope_group(a, cos, sin):
    return a * cos + pltpu.roll(a, HEAD_DIM // 2, axis=1) * sin


def _proj_kernel(src_ref, u_ref, w_ref, wt_ref, cos_ref, sin_ref, o_ref):
    j = pl.program_id(1)
    plain = jnp.logical_or(j == C_VM // PROJ_TN, j == C_VD // PROJ_TN)
    last = j == C_KI // PROJ_TN
    groups = PROJ_TN // HEAD_DIM

    @pl.when(plain)
    def _():
        o_ref[...] = _dot(u_ref[...], w_ref[...])

    @pl.when(jnp.logical_not(jnp.logical_or(plain, last)))
    def _():
        acc = _dot(u_ref[...], w_ref[...])
        cos = cos_ref[...]
        sin = sin_ref[...]
        for c in range(groups):
            o_ref[:, c * HEAD_DIM:(c + 1) * HEAD_DIM] = _rope_group(acc[:, c * HEAD_DIM:(c + 1) * HEAD_DIM], cos, sin)

    @pl.when(last)
    def _():
        acc = _dot(u_ref[...], wt_ref[...])
        o_ref[:, 0:HEAD_DIM] = _rope_group(acc[:, 0:HEAD_DIM], cos_ref[...], sin_ref[...])
        o_ref[:, HEAD_DIM:2 * HEAD_DIM] = acc[:, HEAD_DIM:2 * HEAD_DIM] * IDX_W_SCALE
        o_ref[:, 2 * HEAD_DIM:] = acc[:, 2 * HEAD_DIM:]


def _w_in_tile_order():
    model = dict(q_m=0, k_m=MOBA_HEADS * HEAD_DIM)
    model["v_m"] = model["k_m"] + KVW
    model["q_d"] = model["v_m"] + KVW
    model["k_d"] = model["q_d"] + DSA_HEADS * HEAD_DIM
    model["v_d"] = model["k_d"] + KVW
    model["q_i"] = model["v_d"] + KVW
    model["k_i"] = model["q_i"] + IDX_HEADS * IDX_DIM
    ours = (("q_i", C_QI, C_QM), ("q_m", C_QM, C_QD), ("q_d", C_QD, C_KM), ("k_m", C_KM, C_VM), ("v_m", C_VM, C_KD),
            ("k_d", C_KD, C_VD), ("v_d", C_VD, C_KI))
    order = []
    for name, lo, hi in ours:
        assert lo % PROJ_TN == 0 and model[name] % PROJ_TN == 0
        order += [model[name] // PROJ_TN + x for x in range((hi - lo) // PROJ_TN)]
    assert model["k_i"] == C_KI and PROJ_PAD - C_KI == PROJ_TN
    order.append(order[-1])
    return np.asarray(order, np.int32)


def _project(u, w, w_tail, cos, sin, tm):
    n, d = u.shape
    return pl.pallas_call(
        _proj_kernel,
        grid_spec=pltpu.PrefetchScalarGridSpec(
            num_scalar_prefetch=1,
            grid=(n // tm, PROJ_PAD // PROJ_TN),
            in_specs=[
                pl.BlockSpec((tm, d), lambda i, j, src: (i, 0)),
                pl.BlockSpec((d, PROJ_TN), lambda i, j, src: (0, src[j])),
                pl.BlockSpec((d, PROJ_TN), lambda i, j, src: (0, 0)),
                pl.BlockSpec((tm, HEAD_DIM), lambda i, j, src: (i, 0)),
                pl.BlockSpec((tm, HEAD_DIM), lambda i, j, src: (i, 0)),
            ],
            out_specs=pl.BlockSpec((tm, PROJ_TN), lambda i, j, src: (i, j)),
        ),
        out_shape=jax.ShapeDtypeStruct((n, PROJ_PAD), F32),
        compiler_params=_cparams(("parallel", "arbitrary")),
        name="proj_rope",
    )(jnp.asarray(_w_in_tile_order()), u, w, w_tail, cos, sin)


def _rope_tables(pos):
    half = HEAD_DIM // 2
    inv = 1.0 / (ROPE_THETA ** (jnp.arange(half, dtype=F32) * (2.0 / HEAD_DIM)))
    ang = pos.astype(F32)[:, None] * inv[None, :]
    c, s = jnp.cos(ang), jnp.sin(ang)
    return jnp.concatenate([c, c], axis=-1), jnp.concatenate([-s, s], axis=-1)


def _kmean_kernel(k_ref, o_ref):
    nb = o_ref.shape[0]
    k = k_ref[...].reshape(nb, MOBA_BLOCK, KVW)
    o_ref[...] = jnp.sum(k, axis=1) * (1.0 / MOBA_BLOCK)


def _kmean(proj, t):
    nb = t // MOBA_BLOCK
    return pl.pallas_call(
        _kmean_kernel,
        grid=(1,),
        in_specs=[pl.BlockSpec((t, KVW), lambda i: (0, C_KM // KVW))],
        out_specs=pl.BlockSpec((nb, KVW), lambda i: (0, 0)),
        out_shape=jax.ShapeDtypeStruct((nb, KVW), F32),
        compiler_params=_cparams(("arbitrary",)),
        name="moba_kmean",
    )(proj)


def _top_block_ids(gate, lane, n_sel):
    ids = []
    for _ in range(n_sel):
        mx = jnp.max(gate, axis=-1, keepdims=True)
        idx = jnp.min(jnp.where(gate == mx, lane, LANES), axis=-1, keepdims=True)
        ids.append(jnp.where(mx > -jnp.inf, idx, -1))
        gate = jnp.where(lane == idx, -jnp.inf, gate)
    return ids


def _sortable_key(x):
    bits = pltpu.bitcast(x + 0.0, jnp.int32)
    return bits ^ ((bits >> 31) & 0x7FFFFFFF)


def _kth_largest_key(count_ge, shape, k, two_bits=False):
    t = jnp.broadcast_to(jnp.where(count_ge(jnp.zeros(shape, jnp.int32)) >= k, 0, INT_MIN).astype(jnp.int32), shape)

    def one_bit(bit, t):
        cand = t + (jnp.int32(1) << bit)
        return jnp.where(count_ge(cand) >= k, cand, t)

    if not two_bits:
        return lax.fori_loop(0, 31, lambda it, t: one_bit(30 - it, t), t)

    def pair(it, t):
        step = jnp.int32(1) << (29 - 2 * it)
        c1, c2, c3 = t + step, t + 2 * step, t + 3 * step
        n1, n2, n3 = count_ge(c1), count_ge(c2), count_ge(c3)
        return jnp.where(n3 >= k, c3, jnp.where(n2 >= k, c2, jnp.where(n1 >= k, c1, t)))

    return one_bit(0, lax.fori_loop(0, 15, pair, t))


PTQ = 256
PTK = 512


def _prompt_attn_kernel(moba, q_ref, k_ref, v_ref, aux_ref, o_ref, qs_ref, m_ref, acc_ref, sel_ref):
    i = pl.program_id(0)
    j = pl.program_id(1)
    rows = GROUP * PTQ
    last_j = (i * PTQ + PTQ - 1) // PTK
    chunks = PTK // LANES

    @pl.when(j == 0)
    def _():
        for n in range(KV_HEADS):
            for g in range(GROUP):
                h = n * GROUP + g
                qs_ref[n, g * PTQ:(g + 1) * PTQ, :] = (q_ref[:, h * HEAD_DIM:(h + 1) * HEAD_DIM] * ATTN_SCALE).astype(BF16)
        m_ref[...] = jnp.full(m_ref.shape, NEG, F32)
        acc_ref[...] = jnp.zeros(acc_ref.shape, F32)
        if moba:
            blk = lax.broadcasted_iota(jnp.int32, (LANES, rows), 0)
            for n in range(KV_HEADS):
                km = aux_ref[:, n * HEAD_DIM:(n + 1) * HEAD_DIM].astype(BF16)
                gate = jnp.where(blk < i, _dot_nt(km, qs_ref[n]), -jnp.inf)
                bits = jnp.zeros((1, rows), jnp.int32)
                for _ in range(MOBA_TOPK):
                    mx = jnp.max(gate, axis=0, keepdims=True)
                    idx = jnp.min(jnp.where(gate == mx, blk, LANES), axis=0, keepdims=True)
                    bits = bits | jnp.where(mx > -jnp.inf, jnp.int32(1) << jnp.minimum(idx, 31), 0)
                    gate = jnp.where(blk == idx, -jnp.inf, gate)
                lo = jnp.broadcast_to((bits & 0xFFFF).astype(F32), (LANES, rows)).T
                hi = jnp.broadcast_to((bits >> 16).astype(F32), (LANES, rows)).T
                sel_ref[n] = lo.astype(jnp.int32) | (hi.astype(jnp.int32) << 16)

    def step(diagonal):
        k = k_ref[...].astype(BF16)
        v = v_ref[...].astype(BF16)
        ones = jnp.ones((PTK, HEAD_DIM), BF16)
        if moba and diagonal:
            lane = lax.broadcasted_iota(jnp.int32, (rows, LANES), 1)
            t_in = lax.broadcasted_iota(jnp.int32, (rows, LANES), 0) & (PTQ - 1)
        if not moba:
            bias = jnp.concatenate([aux_ref[...].T] * GROUP, axis=0)
        for n in range(KV_HEADS):
            s = _dot_nt(qs_ref[n], k[:, n * HEAD_DIM:(n + 1) * HEAD_DIM])
            if moba and diagonal:
                bits = sel_ref[n]
                limits = []
                for b in range(PTK // MOBA_BLOCK):
                    kb = j * (PTK // MOBA_BLOCK) + b
                    picked = (bits >> jnp.minimum(kb, 31)) & 1
                    limits.append(jnp.where(kb == i, t_in, jnp.where(kb < i, picked * MOBA_BLOCK, 0) - 1))
                parts = []
                for c in range(chunks):
                    col = lane + (c * LANES) % MOBA_BLOCK
                    parts.append(jnp.where(col <= limits[(c * LANES) // MOBA_BLOCK],
                                           s[:, c * LANES:(c + 1) * LANES], NEG))
            elif moba:
                bits = sel_ref[n]
                row_bias = [jnp.where(((bits >> (j * (PTK // MOBA_BLOCK) + b)) & 1) == 1, 0.0, NEG)
                            for b in range(PTK // MOBA_BLOCK)]
                parts = [s[:, c * LANES:(c + 1) * LANES] + row_bias[(c * LANES) // MOBA_BLOCK] for c in range(chunks)]
            else:
                s = s + bias
                parts = [s[:, c * LANES:(c + 1) * LANES] for c in range(chunks)]
            mx = parts[0]
            for c in range(1, chunks):
                mx = jnp.maximum(mx, parts[c])
            m_prev = m_ref[n]
            m_new = jnp.maximum(m_prev, jnp.max(mx, axis=-1, keepdims=True))
            alpha = jnp.exp(m_prev - m_new)
            p = jnp.concatenate([jnp.exp(x - m_new) for x in parts], axis=1).astype(BF16)
            v_ones = jnp.concatenate([v[:, n * HEAD_DIM:(n + 1) * HEAD_DIM], ones], axis=1)
            acc_ref[n] = jnp.concatenate([alpha, alpha], axis=1) * acc_ref[n] + _dot(p, v_ones)
            m_ref[n] = m_new

    if moba:
        pl.when(j < last_j)(functools.partial(step, False))
        pl.when(j == last_j)(functools.partial(step, True))
    else:
        pl.when(j <= last_j)(functools.partial(step, False))

    @pl.when(j == last_j)
    def _():
        for n in range(KV_HEADS):
            for g in range(GROUP):
                h = n * GROUP + g
                r = slice(g * PTQ, (g + 1) * PTQ)
                o_ref[:, h * HEAD_DIM:(h + 1) * HEAD_DIM] = (acc_ref[n, r, 0:HEAD_DIM]
                                                             / acc_ref[n, r, HEAD_DIM:]).astype(o_ref.dtype)


def _prompt_attn(moba, proj, aux, t):
    nq, nk = t // PTQ, t // PTK
    assert t // MOBA_BLOCK <= 32
    qw = MOBA_HEADS * HEAD_DIM
    c_q, c_k, c_v = (C_QM, C_KM, C_VM) if moba else (C_QD, C_KD, C_VD)
    jj = lambda i, j: jnp.minimum(j, (i * PTQ + PTQ - 1) // PTK)
    if moba:
        aux_spec = pl.BlockSpec((LANES, KVW), lambda i, j: (0, 0))
    else:
        aux_spec = pl.BlockSpec((PTK, PTQ), lambda i, j: (jj(i, j), i))
    rows = GROUP * PTQ
    return pl.pallas_call(
        functools.partial(_prompt_attn_kernel, moba),
        grid=(nq, nk),
        in_specs=[
            pl.BlockSpec((PTQ, qw), lambda i, j: (i, c_q // qw)),
            pl.BlockSpec((PTK, KVW), lambda i, j: (jj(i, j), c_k // KVW)),
            pl.BlockSpec((PTK, KVW), lambda i, j: (jj(i, j), c_v // KVW)),
            aux_spec,
        ],
        out_specs=pl.BlockSpec((PTQ, qw), lambda i, j: (i, 0)),
        out_shape=jax.ShapeDtypeStruct((t, qw), BF16),
        scratch_shapes=[
            pltpu.VMEM((KV_HEADS, rows, HEAD_DIM), BF16),
            pltpu.VMEM((KV_HEADS, rows, LANES), F32),
            pltpu.VMEM((KV_HEADS, rows, HEAD_DIM + LANES), F32),
            pltpu.VMEM((KV_HEADS, rows, LANES), jnp.int32),
        ],
        compiler_params=_cparams(("parallel", "arbitrary")),
        name="moba_prompt" if moba else "dsa_prompt",
    )(proj, proj, proj, aux)


DTK = 512


COUNT_ROWS = 32


def _dsa_select_kernel(k_keep, qi_ref, ki_ref, wt_ref, bias_ref, qs_ref, key_ref):
    i = pl.program_id(0)
    j = pl.program_id(1)
    nj = pl.num_programs(1)
    length = key_ref.shape[0]

    @pl.when(j == 0)
    def _():
        qs_ref[...] = qi_ref[...].astype(BF16)
        key_ref[...] = jnp.full(key_ref.shape, INT_MIN, jnp.int32)

    @pl.when(j * DTK <= i * PTQ + PTQ - 1)
    def _():
        kb = ki_ref[...].astype(BF16)
        acc = jnp.zeros((DTK, PTQ), F32)
        for h in range(IDX_HEADS):
            s = _dot_nt(kb, qs_ref[:, h * IDX_DIM:(h + 1) * IDX_DIM])
            acc = acc + wt_ref[h:h + 1, :] * jnp.maximum(s, 0.0)
        key_pos = j * DTK + lax.broadcasted_iota(jnp.int32, (DTK, PTQ), 0)
        q_pos = i * PTQ + lax.broadcasted_iota(jnp.int32, (DTK, PTQ), 1)
        x = jnp.where(key_pos <= q_pos, acc, -jnp.inf)
        key_ref[pl.ds(pl.multiple_of(j * DTK, DTK), DTK), :] = _sortable_key(x)

    @pl.when(j == nj - 1)
    def _():
        n_tiles = (i * PTQ + PTQ - 1) // DTK + 1

        def count_ge(cand):
            def tile(c, acc):
                keys = key_ref[pl.ds(pl.multiple_of(c * DTK, DTK), DTK), :]
                hit = jnp.where(keys.reshape(DTK // COUNT_ROWS, COUNT_ROWS, PTQ) >= cand, 1, 0)
                return acc + jnp.sum(hit, axis=0)

            acc = lax.fori_loop(0, n_tiles, tile, jnp.zeros((COUNT_ROWS, PTQ), jnp.int32))
            return jnp.sum(acc, axis=0, keepdims=True)

        thr = _kth_largest_key(count_ge, (1, PTQ), k_keep)
        key_pos = lax.broadcasted_iota(jnp.int32, (length, PTQ), 0)
        q_pos = i * PTQ + lax.broadcasted_iota(jnp.int32, (length, PTQ), 1)
        ok = jnp.logical_and(key_ref[...] >= thr, key_pos <= q_pos)
        bias_ref[...] = jnp.where(ok, 0.0, NEG)


def _dsa_select(proj, t):
    nq, nj = t // PTQ, t // DTK
    k_keep = max(1, min(DSA_TOPK_MAX, t // 4))
    qw = IDX_HEADS * IDX_DIM
    last_j = lambda i: (i * PTQ + PTQ - 1) // DTK
    w_t = proj[:t, C_WI:C_WI + IDX_HEADS].T
    return pl.pallas_call(
        functools.partial(_dsa_select_kernel, k_keep),
        grid=(nq, nj),
        in_specs=[
            pl.BlockSpec((PTQ, qw), lambda i, j: (i, C_QI // qw)),
            pl.BlockSpec((DTK, IDX_DIM), lambda i, j: (jnp.minimum(j, last_j(i)), C_KI // IDX_DIM)),
            pl.BlockSpec((IDX_HEADS, PTQ), lambda i, j: (0, i)),
        ],
        out_specs=pl.BlockSpec((t, PTQ), lambda i, j: (0, i)),
        out_shape=jax.ShapeDtypeStruct((t, t), F32),
        scratch_shapes=[pltpu.VMEM((PTQ, qw), BF16), pltpu.VMEM((t, PTQ), jnp.int32)],
        compiler_params=_cparams(("parallel", "arbitrary")),
        name="dsa_select",
    )(proj, proj, w_t)


N_PAGES = PAST_LEN // PAGE_SIZE
PAGES_PER_BLOCK = MOBA_BLOCK // PAGE_SIZE
PAGE_ROWS = PAGE_SIZE * KV_HEADS


def _head_match(rows, cols, dec):
    r = lax.broadcasted_iota(jnp.int32, (rows, cols), 0)
    c = lax.broadcasted_iota(jnp.int32, (rows, cols), 1)
    shift = (GROUP * dec).bit_length() - 1
    return (r >> shift) == (c & (KV_HEADS - 1))


def _stage_new(dst_ref, src_ref):
    dst_ref[...] = jnp.zeros(dst_ref.shape, F32)
    dst_ref[0:src_ref.shape[0], :] = src_ref[...]


def _softmax_over_pages(s_ref, s_new, v_refs, nv_ref, o_ref):
    m = jnp.max(s_new, axis=-1, keepdims=True)
    for p in range(N_PAGES):
        m = jnp.maximum(m, jnp.max(s_ref[p], axis=-1, keepdims=True))
    pn = jnp.exp(s_new - m)
    l = jnp.sum(pn, axis=-1, keepdims=True)
    acc = _dot(pn.astype(BF16), nv_ref[...].astype(BF16))
    for p in range(N_PAGES):
        pr = jnp.exp(s_ref[p] - m)
        l = l + jnp.sum(pr, axis=-1, keepdims=True)
        acc = acc + _dot(pr.astype(BF16), v_refs[p][...].astype(BF16))
    o_ref[...] = acc / l


def _sample_moba_kernel(pt_ref, q_ref, kn_ref, vn_ref, *rest):
    k_refs, v_refs = rest[:N_PAGES], rest[N_PAGES:2 * N_PAGES]
    o_ref, s_ref, nk_ref, nv_ref = rest[2 * N_PAGES:]
    rows = q_ref.shape[0]
    dec = rows // MOBA_HEADS
    n_past_blocks = PAST_LEN // MOBA_BLOCK
    q = q_ref[...].astype(BF16)
    match = _head_match(rows, PAGE_ROWS, dec)
    lane = lax.broadcasted_iota(jnp.int32, (rows, LANES), 1)

    for p in range(N_PAGES):
        s_ref[p] = _dot_nt(q, k_refs[p][...].astype(BF16))

    gate = jnp.full((rows, LANES), -jnp.inf, F32)
    for b in range(n_past_blocks):
        tot = s_ref[PAGES_PER_BLOCK * b]
        for c in range(1, PAGES_PER_BLOCK):
            tot = tot + s_ref[PAGES_PER_BLOCK * b + c]
        g = jnp.sum(jnp.where(match, tot, 0.0), axis=-1, keepdims=True) * (1.0 / MOBA_BLOCK)
        gate = jnp.where(lane == b, g, gate)
    ids = _top_block_ids(gate, lane, MOBA_TOPK)

    for b in range(n_past_blocks):
        picked = ids[0] == b
        for idx in ids[1:]:
            picked = jnp.logical_or(picked, idx == b)
        row_bias = jnp.where(picked, 0.0, NEG)
        for c in range(PAGES_PER_BLOCK):
            p = PAGES_PER_BLOCK * b + c
            s_ref[p] = jnp.where(match, s_ref[p] * ATTN_SCALE + row_bias, NEG)

    _stage_new(nk_ref, kn_ref)
    _stage_new(nv_ref, vn_ref)
    rn = lax.broadcasted_iota(jnp.int32, (rows, LANES), 0)
    ok = jnp.logical_and(_head_match(rows, LANES, dec), (lane >> 2) <= (rn & (dec - 1)))
    s_new = jnp.where(ok, _dot_nt(q, nk_ref[...].astype(BF16)) * ATTN_SCALE, NEG)
    _softmax_over_pages(s_ref, s_new, v_refs, nv_ref, o_ref)


def _page_specs(shape, n):
    return [pl.BlockSpec((None,) + shape, functools.partial(lambda b, pt, p: (pt[b, p], 0, 0), p=p)) for p in range(n)]


def _sample_moba(pt, q, kn, vn, ck, cv):
    nb, rows, _ = q.shape
    dec = rows // MOBA_HEADS
    assert dec & (dec - 1) == 0 and dec * KV_HEADS <= LANES and KV_HEADS == 4
    seq = lambda b, pt: (b, 0, 0)
    return pl.pallas_call(
        _sample_moba_kernel,
        grid_spec=pltpu.PrefetchScalarGridSpec(
            num_scalar_prefetch=1,
            grid=(nb,),
            in_specs=[pl.BlockSpec((None, rows, HEAD_DIM), seq),
                      pl.BlockSpec((None, dec * KV_HEADS, HEAD_DIM), seq),
                      pl.BlockSpec((None, dec * KV_HEADS, HEAD_DIM), seq)]
                     + _page_specs((PAGE_ROWS, HEAD_DIM), N_PAGES) + _page_specs((PAGE_ROWS, HEAD_DIM), N_PAGES),
            out_specs=pl.BlockSpec((None, rows, HEAD_DIM), seq),
            scratch_shapes=[
                pltpu.VMEM((N_PAGES, rows, PAGE_ROWS), F32),
                pltpu.VMEM((LANES, HEAD_DIM), F32),
                pltpu.VMEM((LANES, HEAD_DIM), F32),
            ],
        ),
        out_shape=jax.ShapeDtypeStruct((nb, rows, HEAD_DIM), F32),
        compiler_params=_cparams(("parallel",)),
        name="moba_sample",
    )(pt, q, kn, vn, *([ck] * N_PAGES), *([cv] * N_PAGES))


def _sample_dsa_kernel(k_keep, pt_ref, qi_ref, wi_ref, q_ref, kin_ref, kn_ref, vn_ref, *rest):
    i_refs, k_refs, v_refs = rest[:N_PAGES], rest[N_PAGES:2 * N_PAGES], rest[2 * N_PAGES:3 * N_PAGES]
    o_ref, idx_ref, s_ref, nki_ref, nk_ref, nv_ref = rest[3 * N_PAGES:]
    rows = q_ref.shape[0]
    dec = rows // DSA_HEADS
    qi = qi_ref[...].astype(BF16)
    w = wi_ref[...]

    def index_scores(kpage):
        sc = jnp.maximum(_dot_nt(qi, kpage.astype(BF16)), 0.0) * w
        return jnp.sum(sc.reshape(dec, IDX_HEADS, PAGE_SIZE), axis=1)

    idx_ref[...] = jnp.full(idx_ref.shape, -jnp.inf, F32)
    for p in range(N_PAGES):
        idx_ref[p, 0:dec, :] = index_scores(i_refs[p][...])
    _stage_new(nki_ref, kin_ref)
    t_in = lax.broadcasted_iota(jnp.int32, (dec, PAGE_SIZE), 0)
    col = lax.broadcasted_iota(jnp.int32, (dec, PAGE_SIZE), 1)
    idx_ref[N_PAGES, 0:dec, :] = jnp.where(col <= t_in, index_scores(nki_ref[...]), -jnp.inf)

    keys = _sortable_key(idx_ref[...])

    def count_ge(cand):
        c = jnp.sum((keys >= cand).astype(jnp.int32), axis=0)
        return jnp.sum(c, axis=-1, keepdims=True)[None]

    thr = _kth_largest_key(count_ge, (1, SUBLANES, 1), k_keep, two_bits=True)
    keep = jnp.where(jnp.logical_and(keys >= thr, idx_ref[...] > -jnp.inf), 1.0, 0.0)

    q = q_ref[...].astype(BF16)
    match = _head_match(rows, PAGE_ROWS, dec)
    er = lax.broadcasted_iota(jnp.int32, (PAGE_SIZE, PAGE_ROWS), 0)
    ec = lax.broadcasted_iota(jnp.int32, (PAGE_SIZE, PAGE_ROWS), 1)
    expand = jnp.where((ec >> 2) == er, 1.0, 0.0).astype(BF16)

    def visible(page, width):
        per_row = jnp.concatenate([keep[page, 0:dec, :]] * (rows // dec), axis=0).astype(BF16)
        return _dot(per_row, expand[:, 0:width]) > 0.5

    for p in range(N_PAGES):
        sc = _dot_nt(q, k_refs[p][...].astype(BF16)) * ATTN_SCALE
        s_ref[p] = jnp.where(jnp.logical_and(match, visible(p, PAGE_ROWS)), sc, NEG)
    _stage_new(nk_ref, kn_ref)
    _stage_new(nv_ref, vn_ref)
    ok = jnp.logical_and(_head_match(rows, LANES, dec), visible(N_PAGES, LANES))
    s_new = jnp.where(ok, _dot_nt(q, nk_ref[...].astype(BF16)) * ATTN_SCALE, NEG)
    _softmax_over_pages(s_ref, s_new, v_refs, nv_ref, o_ref)


def _sample_dsa(pt, qi, wi, q, kin, kn, vn, ci, ck, cv):
    nb, rows, _ = q.shape
    dec = rows // DSA_HEADS
    assert dec & (dec - 1) == 0 and dec <= SUBLANES and KV_HEADS == 4
    k_keep = max(1, min(DSA_TOPK_MAX, (PAST_LEN + dec) // 4))
    seq = lambda b, pt: (b, 0, 0)
    return pl.pallas_call(
        functools.partial(_sample_dsa_kernel, k_keep),
        grid_spec=pltpu.PrefetchScalarGridSpec(
            num_scalar_prefetch=1,
            grid=(nb,),
            in_specs=[pl.BlockSpec((None, dec * IDX_HEADS, IDX_DIM), seq),
                      pl.BlockSpec((None, dec * IDX_HEADS, 1), seq),
                      pl.BlockSpec((None, rows, HEAD_DIM), seq),
                      pl.BlockSpec((None, dec, IDX_DIM), seq),
                      pl.BlockSpec((None, dec * KV_HEADS, HEAD_DIM), seq),
                      pl.BlockSpec((None, dec * KV_HEADS, HEAD_DIM), seq)]
                     + _page_specs((PAGE_SIZE, IDX_DIM), N_PAGES)
                     + _page_specs((PAGE_ROWS, HEAD_DIM), N_PAGES) + _page_specs((PAGE_ROWS, HEAD_DIM), N_PAGES),
            out_specs=pl.BlockSpec((None, rows, HEAD_DIM), seq),
            scratch_shapes=[
                pltpu.VMEM((N_PAGES + 1, SUBLANES, PAGE_SIZE), F32),
                pltpu.VMEM((N_PAGES, rows, PAGE_ROWS), F32),
                pltpu.VMEM((PAGE_SIZE, IDX_DIM), F32),
                pltpu.VMEM((LANES, HEAD_DIM), F32),
                pltpu.VMEM((LANES, HEAD_DIM), F32),
            ],
        ),
        out_shape=jax.ShapeDtypeStruct((nb, rows, HEAD_DIM), F32),
        compiler_params=_cparams(("parallel",)),
        name="dsa_sample",
    )(pt, qi, wi, q, kin, kn, vn, *([ci] * N_PAGES), *([ck] * N_PAGES), *([cv] * N_PAGES))


def _outproj_kernel(tiles_a, om_ref, od_ref, ms_ref, w_ref, xa_ref, xb_ref, h_ref, acc_ref):
    i = pl.program_id(0)
    k = pl.program_id(2)
    nk = pl.num_programs(2)
    first = i < tiles_a

    @pl.when(k == 0)
    def _():
        acc_ref[...] = jnp.zeros(acc_ref.shape, F32)

    mix = jnp.where(first, jnp.where(k < nk // 2, om_ref[...], od_ref[...]), ms_ref[...])
    acc_ref[...] += _dot(mix, w_ref[...])

    @pl.when(k == nk - 1)
    def _():
        h_ref[...] = jnp.where(first, xa_ref[...], xb_ref[...]) + acc_ref[...]


def _outproj(o_m, o_d, mix_b, w, xa, xb, tm, tn, tk):
    (na, d), nb = xa.shape, xb.shape[0]
    n = na + nb
    nk = d // tk
    half = nk // 2
    tiles_a = na // tm
    assert na % tm == 0 and nb % tm == 0 and nk % 2 == 0 and o_m.shape[1] == half * tk and d % tn == 0
    row_a = lambda i: jnp.minimum(i, tiles_a - 1)
    row_b = lambda i: jnp.maximum(i - tiles_a, 0)
    col_a = lambda i, x: jnp.where(i < tiles_a, x, 0)
    col_b = lambda i, x: jnp.where(i < tiles_a, 0, x)
    return pl.pallas_call(
        functools.partial(_outproj_kernel, tiles_a),
        grid=(n // tm, d // tn, nk),
        in_specs=[
            pl.BlockSpec((tm, tk), lambda i, c, k: (row_a(i), col_a(i, jnp.minimum(k, half - 1)))),
            pl.BlockSpec((tm, tk), lambda i, c, k: (row_a(i), col_a(i, jnp.maximum(k - half, 0)))),
            pl.BlockSpec((tm, tk), lambda i, c, k: (row_b(i), col_b(i, k))),
            pl.BlockSpec((tk, tn), lambda i, c, k: (k, c)),
            pl.BlockSpec((tm, tn), lambda i, c, k: (row_a(i), col_a(i, c))),
            pl.BlockSpec((tm, tn), lambda i, c, k: (row_b(i), col_b(i, c))),
        ],
        out_specs=pl.BlockSpec((tm, tn), lambda i, c, k: (i, c)),
        out_shape=jax.ShapeDtypeStruct((n, d), F32),
        scratch_shapes=[pltpu.VMEM((tm, tn), F32)],
        compiler_params=_cparams(("parallel", "parallel", "arbitrary")),
        name="outproj",
    )(o_m, o_d, mix_b, w, xa, xb)


def _rmsnorm_rows_kernel(x_ref, g_ref, o_ref):
    x = x_ref[...]
    y = x * lax.rsqrt(jnp.mean(x * x, axis=-1, keepdims=True) + NORM_EPS)
    o_ref[...] = (y * g_ref[...]).astype(o_ref.dtype)


def _rmsnorm_rows(x, g, out_dtype, tm):
    n, d = x.shape
    return pl.pallas_call(
        _rmsnorm_rows_kernel,
        grid=(n // tm,),
        in_specs=[pl.BlockSpec((tm, d), lambda i: (i, 0)), pl.BlockSpec((1, d), lambda i: (0, 0))],
        out_specs=pl.BlockSpec((tm, d), lambda i: (i, 0)),
        out_shape=jax.ShapeDtypeStruct((n, d), out_dtype),
        compiler_params=_cparams(("parallel",)),
        name="rmsnorm_ffn",
    )(x, g.reshape(1, d))


def _top_rows(x, k):
    r = x.shape[0]
    ridx = lax.broadcasted_iota(jnp.int32, x.shape, 0)
    outs = []
    for _ in range(k):
        mx = jnp.max(x, axis=0, keepdims=True)
        idx = jnp.min(jnp.where(x == mx, ridx, r), axis=0, keepdims=True)
        outs.append(mx)
        x = jnp.where(ridx == idx, -1.0, x)
    return jnp.concatenate(outs, axis=0)


def _peer_select_kernel(h2_ref, wq_ref, sk_ref, e_ref, thr_ref, iz_ref, acc_ref):
    k = pl.program_id(1)

    @pl.when(k == 0)
    def _():
        acc_ref[...] = jnp.zeros(acc_ref.shape, F32)

    acc_ref[...] += _dot(h2_ref[...], wq_ref[...])

    @pl.when(k == pl.num_programs(1) - 1)
    def _():
        half = PEER_DK // 2
        hk = PEER_TOPK // 2
        for h in range(PEER_HEADS):
            tops = []
            for p in range(2):
                c0 = (h * 2 + p) * half
                q = acc_ref[:, c0:c0 + half].astype(BF16)
                st = _dot_nt(sk_ref[h, p], q)
                e = jnp.exp(st - jnp.max(st, axis=0, keepdims=True))
                e_ref[h, p] = e
                tops.append(_top_rows(e, PEER_TOPK))
            a, b = tops
            cand = [a[x:x + 1, :] * b[0:hk, :] for x in range(hk)]
            cand += [a[0:1, :] * b[hk:, :], a[hk:, :] * b[0:1, :]]
            top = _top_rows(jnp.concatenate(cand, axis=0), PEER_TOPK)
            thr_ref[h:h + 1, :] = top[PEER_TOPK - 1:PEER_TOPK, :]
            iz_ref[h:h + 1, :] = 1.0 / jnp.sum(top, axis=0, keepdims=True)


def _peer_select(h2, wq, sk, tm, tk):
    n, d = h2.shape
    qw = wq.shape[1]
    return pl.pallas_call(
        _peer_select_kernel,
        grid=(n // tm, d // tk),
        in_specs=[
            pl.BlockSpec((tm, tk), lambda i, k: (i, k)),
            pl.BlockSpec((tk, qw), lambda i, k: (k, 0)),
            pl.BlockSpec(sk.shape, lambda i, k: (0, 0, 0, 0)),
        ],
        out_specs=[
            pl.BlockSpec((PEER_HEADS, 2, PEER_N_KEYS, tm), lambda i, k: (0, 0, 0, i)),
            pl.BlockSpec((PEER_HEADS, tm), lambda i, k: (0, i)),
            pl.BlockSpec((PEER_HEADS, tm), lambda i, k: (0, i)),
        ],
        out_shape=[
            jax.ShapeDtypeStruct((PEER_HEADS, 2, PEER_N_KEYS, n), F32),
            jax.ShapeDtypeStruct((PEER_HEADS, n), F32),
            jax.ShapeDtypeStruct((PEER_HEADS, n), F32),
        ],
        scratch_shapes=[pltpu.VMEM((tm, qw), F32)],
        compiler_params=_cparams(("parallel", "arbitrary")),
        name="peer_select",
    )(h2, wq, sk)


PEER_SUB = 256


def _peer_dense_kernel(h2_ref, u_ref, v_ref, e_ref, thr_ref, iz_ref, o_ref):
    e = pl.program_id(1)
    te = u_ref.shape[0]
    tm = h2_ref.shape[0]

    @pl.when(e == 0)
    def _():
        o_ref[...] = jnp.zeros(o_ref.shape, F32)

    h2 = h2_ref[...]
    total = None
    for sub in range(te // PEER_SUB):
        r0 = sub * PEER_SUB
        at = _dot_nt(u_ref[r0:r0 + PEER_SUB, :], h2)
        act = 0.5 * at * (1.0 + lax.erf(at * (2.0 ** -0.5)))
        parts = []
        for c in range(PEER_SUB // PEER_N_KEYS):
            i = (e * te + r0) // PEER_N_KEYS + c
            g = jnp.zeros((PEER_N_KEYS, tm), F32)
            for h in range(PEER_HEADS):
                p = e_ref[h, 0, pl.ds(i, 1), :] * e_ref[h, 1]
                g = g + jnp.where(p >= thr_ref[h:h + 1, :], p, 0.0) * iz_ref[h:h + 1, :]
            parts.append(g)
        ga = (jnp.concatenate(parts, axis=0) * act).astype(BF16)
        d = lax.dot_general(ga, v_ref[r0:r0 + PEER_SUB, :], (((0,), (0,)), ((), ())), preferred_element_type=F32)
        total = d if total is None else total + d
    o_ref[...] += total


def _peer_dense(h2, u, v, e, thr, iz, tm, te):
    n, d = h2.shape
    ne = u.shape[0]
    return pl.pallas_call(
        _peer_dense_kernel,
        grid=(n // tm, ne // te),
        in_specs=[
            pl.BlockSpec((tm, d), lambda i, k: (i, 0)),
            pl.BlockSpec((te, d), lambda i, k: (k, 0)),
            pl.BlockSpec((te, d), lambda i, k: (k, 0)),
            pl.BlockSpec((PEER_HEADS, 2, PEER_N_KEYS, tm), lambda i, k: (0, 0, 0, i)),
            pl.BlockSpec((PEER_HEADS, tm), lambda i, k: (0, i)),
            pl.BlockSpec((PEER_HEADS, tm), lambda i, k: (0, i)),
        ],
        out_specs=pl.BlockSpec((tm, d), lambda i, k: (i, 0)),
        out_shape=jax.ShapeDtypeStruct((n, d), F32),
        compiler_params=_cparams(("parallel", "arbitrary")),
        name="peer_dense",
    )(h2, u, v, e, thr, iz)


def _final_kernel(h_ref, f_ref, g_ref, o_ref):
    x = h_ref[...] + f_ref[...]
    y = x * lax.rsqrt(jnp.mean(x * x, axis=-1, keepdims=True) + NORM_EPS)
    o_ref[...] = y * g_ref[...]


def _final(h, f, g, row0, rows, tm):
    d = h.shape[1]
    assert row0 % tm == 0 and rows % tm == 0
    src = pl.BlockSpec((tm, d), lambda i: (i + row0 // tm, 0))
    return pl.pallas_call(
        _final_kernel,
        grid=(rows // tm,),
        in_specs=[src, src, pl.BlockSpec((1, d), lambda i: (0, 0))],
        out_specs=pl.BlockSpec((tm, d), lambda i: (i, 0)),
        out_shape=jax.ShapeDtypeStruct((rows, d), F32),
        compiler_params=_cparams(("parallel",)),
        name="final_norm",
    )(h, f, g.reshape(1, d))


def _heads_to_rows(q, nb, dec):
    return q.reshape(nb, dec, MOBA_HEADS, HEAD_DIM).transpose(0, 2, 1, 3).reshape(nb, MOBA_HEADS * dec, HEAD_DIM)


def _rows_to_heads(o, nb, dec):
    return o.reshape(nb, MOBA_HEADS, dec, HEAD_DIM).transpose(0, 2, 1, 3).reshape(nb * dec, MOBA_HEADS * HEAD_DIM)


def _sample_mix(proj_s, page_table, c_mk, c_mv, c_dk, c_dv, c_ik, nb, dec):
    ps = proj_s.reshape(nb, dec, PROJ_PAD)
    seg = lambda c, w: ps[:, :, c:c + w]
    kv_rows = lambda c: seg(c, KVW).reshape(nb, dec * KV_HEADS, HEAD_DIM)
    s_m = _sample_moba(page_table, _heads_to_rows(seg(C_QM, MOBA_HEADS * HEAD_DIM), nb, dec),
                       kv_rows(C_KM), kv_rows(C_VM), c_mk, c_mv)
    qi = seg(C_QI, IDX_HEADS * IDX_DIM).reshape(nb, dec * IDX_HEADS, IDX_DIM)
    wi = seg(C_WI, IDX_HEADS).reshape(nb, dec * IDX_HEADS, 1)
    s_d = _sample_dsa(page_table, qi, wi, _heads_to_rows(seg(C_QD, DSA_HEADS * HEAD_DIM), nb, dec),
                      seg(C_KI, IDX_DIM), kv_rows(C_KD), kv_rows(C_VD), c_ik, c_dk, c_dv)
    return jnp.concatenate([_rows_to_heads(s_m, nb, dec), _rows_to_heads(s_d, nb, dec)], axis=-1)


def _prompt_mix(proj, t):
    kmean = _kmean(proj, t)
    kmean = jnp.pad(kmean, ((0, LANES - kmean.shape[0]), (0, 0)))
    o_m = _prompt_attn(True, proj, kmean, t)
    o_d = _prompt_attn(False, proj, _dsa_select(proj, t), t)
    return o_m, o_d


def kernel(x_prompt, x_sample, cache_moba_k, cache_moba_v, cache_dsa_k, cache_dsa_v, cache_idx_k, page_table,
           norm_mix, w_in, w_out, norm_ffn, peer_w_q, peer_subkeys, peer_u, peer_v, norm_final):
    depth = w_in.shape[0]
    assert depth == 1 and x_prompt.shape[0] == 1
    assert PAST_LEN % MOBA_BLOCK == 0 and page_table.shape[1] == N_PAGES
    t = x_prompt.shape[1]
    nb, dec, d = x_sample.shape
    n_tok = t + nb * dec
    tm = 512
    assert n_tok % tm == 0 and t % tm == 0

    x_p = x_prompt.reshape(t, d)
    x_s = x_sample.reshape(nb * dec, d)
    pos = jnp.concatenate([jnp.arange(t, dtype=jnp.int32),
                           PAST_LEN + jnp.tile(jnp.arange(dec, dtype=jnp.int32), nb)])
    cos, sin = _rope_tables(pos)

    n_pool = cache_moba_k.shape[1]
    flat = lambda c: c[0].reshape(n_pool, PAGE_ROWS, HEAD_DIM)

    u = _rmsnorm(x_p, x_s, norm_mix[0], BF16, tm)
    tm_proj = n_tok // 8 if (n_tok // 8) % 16 == 0 and n_tok % 8 == 0 else tm
    w_tail = w_in[0][:, C_KI:]
    w_tail = jnp.pad(w_tail, ((0, 0), (0, PROJ_TN - w_tail.shape[1]))).astype(BF16)
    proj = _project(u, w_in[0].astype(BF16), w_tail, cos, sin, tm_proj)
    o_m, o_d = _prompt_mix(proj, t)
    mix_s = _sample_mix(proj[t:], page_table, flat(cache_moba_k), flat(cache_moba_v), flat(cache_dsa_k),
                        flat(cache_dsa_v), cache_idx_k[0], nb, dec)
    h = _outproj(o_m, o_d, mix_s.astype(BF16), w_out[0].astype(BF16), x_p, x_s, tm, d // 2, 512)
    h2 = _rmsnorm_rows(h, norm_ffn[0], BF16, tm)
    e, thr, iz = _peer_select(h2, peer_w_q[0].astype(BF16), peer_subkeys[0].astype(BF16), tm, 512)
    f = _peer_dense(h2, peer_u[0].astype(BF16), peer_v[0].astype(BF16), e, thr, iz, tm, 512)

    y_prompt = _final(h, f, norm_final, 0, t, 256).reshape(1, t, d)
    y_sample = _final(h, f, norm_final, t, nb * dec, 256).reshape(nb, dec, d)
    pp = proj[:t]
    ps = proj[t:]
    kv = lambda a, c: a[:, c:c + KVW]
    outs_p = [kv(pp, c).reshape(1, 1, t, KV_HEADS, HEAD_DIM) for c in (C_KM, C_VM, C_KD, C_VD)]
    outs_p.append(pp[:, C_KI:C_KI + IDX_DIM].reshape(1, 1, t, IDX_DIM))
    outs_s = [kv(ps, c).reshape(1, nb, dec, KV_HEADS, HEAD_DIM) for c in (C_KM, C_VM, C_KD, C_VD)]
    outs_s.append(ps[:, C_KI:C_KI + IDX_DIM].reshape(1, nb, dec, IDX_DIM))
    return (y_prompt, y_sample, *outs_p, *outs_s)
```
